```python
import math
import jax
import jax.numpy as jnp
from jax import lax
import numpy as np

D_MODEL = 1024
BATCH = 32
SEQ = 256
DEPTH = 2
DEC_BATCH = 8
DEC_SEQ = 2048
PAST_LEN = 256

GRID_W = 64
EPS = 1e-6
MLA_HEADS = 8
Q_LORA = 384
KV_LORA = 256
QK_NOPE = 64
QK_ROPE = 32
V_HEAD = 64
MLA_W = MLA_HEADS * V_HEAD
ROPE_THETA = 10000.0
Q_BLOCK = 128
CONV_W = 512
CONV_K = 3
GDN_HEADS = 8
GDN_DK = 64
GDN_DV = 64
GDN_QK_W = GDN_HEADS * GDN_DK
GDN_V_W = GDN_HEADS * GDN_DV
GDN_CHUNK = 64
N_BRANCH = 3
IN_SIZES = (Q_LORA, KV_LORA, QK_ROPE, MLA_W,
            CONV_W, CONV_W, CONV_W, CONV_W,
            GDN_QK_W, GDN_QK_W, GDN_V_W, GDN_V_W,
            GDN_HEADS, GDN_HEADS, GDN_HEADS, GDN_HEADS,
            N_BRANCH * D_MODEL)
D_IN = sum(IN_SIZES)

kernel_name = 'hybrid_mla_conv_gdn_diffusion_step'


def rms_norm(x, g):
    xf = x.astype(jnp.float32)
    y = xf * lax.rsqrt(jnp.mean(xf * xf, axis=-1, keepdims=True) + EPS)
    return (y * g.astype(jnp.float32)).astype(x.dtype)


def l2_normalize(x):
    return x * lax.rsqrt(jnp.sum(x * x, axis=-1, keepdims=True) + EPS)


def split_columns(u):
    parts, start = [], 0
    for n in IN_SIZES:
        parts.append(u[..., start:start + n])
        start += n
    return parts


def axial_rope_tables(n_tokens):
    rows = n_tokens // GRID_W
    t = jnp.arange(rows * GRID_W)
    row = (t // GRID_W).astype(jnp.float32)
    col = (t % GRID_W).astype(jnp.float32)
    n_freq = QK_ROPE // 4
    inv_freq = ROPE_THETA ** (-jnp.arange(n_freq, dtype=jnp.float32) / n_freq)
    ang = jnp.concatenate([row[:, None] * inv_freq, col[:, None] * inv_freq], axis=-1)
    return jnp.cos(ang), jnp.sin(ang)


def apply_rope(x, cos, sin):
    xf = x.astype(jnp.float32).reshape(x.shape[:-1] + (QK_ROPE // 2, 2))
    x0, x1 = xf[..., 0], xf[..., 1]
    out = jnp.stack([x0 * cos - x1 * sin, x0 * sin + x1 * cos], axis=-1)
    return out.reshape(x.shape).astype(x.dtype)


def short_conv(x, w):
    s = x.shape[1]
    pad = CONV_K // 2
    xp = jnp.pad(x, ((0, 0), (pad, pad), (0, 0)))
    y = xp[:, 0:s] * w[0]
    for i in range(1, CONV_K):
        y = y + xp[:, i:i + s] * w[i]
    return y


def adaln_projection(x, cond, p):
    mod = jax.nn.silu(cond) @ p['w_ada'] + p['b_ada']
    shift, scale, gate = jnp.split(mod[:, None, :], 3, axis=-1)
    h = rms_norm(x, p['norm_g']) * (1.0 + scale) + shift
    return split_columns(h @ p['w_in']), gate


def mla_queries_and_latent(cq, ckv, p):
    b, s, _ = cq.shape
    q = (rms_norm(cq, p['q_norm_g']) @ p['w_uq']).reshape(b, s, MLA_HEADS, QK_NOPE + QK_ROPE)
    return q[..., :QK_NOPE], q[..., QK_NOPE:], rms_norm(ckv, p['kv_norm_g'])


def mla_expand(ckv_n, w_ukv):
    b, s, _ = ckv_n.shape
    kv = (ckv_n @ w_ukv).reshape(b, s, MLA_HEADS, QK_NOPE + V_HEAD)
    return kv[..., :QK_NOPE], kv[..., QK_NOPE:]


def mla_attention(q_nope, q_rope, k_nope, k_rope, v):
    b, sq, h, _ = q_nope.shape
    nb = sq // Q_BLOCK
    scale = (QK_NOPE + QK_ROPE) ** -0.5

    def to_blocks(a):
        return jnp.moveaxis(a.reshape((b, nb, Q_BLOCK) + a.shape[2:]), 1, 0)

    def block(qs):
        qn, qr = qs
        s = (jnp.einsum('bqhd,bkhd->bhqk', qn, k_nope)
             + jnp.einsum('bqhr,bkr->bhqk', qr, k_rope)).astype(jnp.float32) * scale
        pr = jax.nn.softmax(s, axis=-1).astype(v.dtype)
        return jnp.einsum('bhqk,bkhd->bqhd', pr, v)

    o = lax.map(block, (to_blocks(q_nope), to_blocks(q_rope)))
    return jnp.moveaxis(o, 0, 1).reshape(b, sq, h * V_HEAD)


def conv_branch(b_in, c_in, x_in, gate, w):
    return b_in * short_conv(c_in * x_in, w) * jax.nn.silu(gate)


def gated_delta_chunked(q, k, v, g, beta, h0):
    b, s, h, dk = q.shape
    dv = v.shape[-1]
    c = GDN_CHUNK
    n = s // c

    def chunks(a):
        a = a.reshape((b, n, c, h) + a.shape[3:])
        return jnp.moveaxis(jnp.moveaxis(a, 1, 0), 3, 2)

    qc, kc, vc, bc = chunks(q), chunks(k), chunks(v), chunks(beta)
    gc = jnp.cumsum(chunks(g), axis=-1)
    idx = jnp.arange(c)
    lower = idx[:, None] >= idx[None, :]
    strict = idx[:, None] > idx[None, :]
    decay = jnp.exp(jnp.where(lower, gc[..., :, None] - gc[..., None, :], -jnp.inf))
    kb = kc * bc[..., None]
    a_mat = jnp.where(strict, jnp.einsum('nbhik,nbhjk->nbhij', kb, kc) * decay, 0.0) + jnp.eye(c, dtype=jnp.float32)
    rhs = jnp.concatenate([vc * bc[..., None], kb * jnp.exp(gc)[..., None]], axis=-1)
    sol = lax.linalg.triangular_solve(a_mat, rhs, left_side=True, lower=True, unit_diagonal=True)
    u, w = sol[..., :dv], sol[..., dv:]
    attn = jnp.where(lower, jnp.einsum('nbhik,nbhjk->nbhij', qc, kc) * decay, 0.0)
    q_dec = qc * jnp.exp(gc)[..., None]
    g_last = gc[..., -1]
    k_dec = kc * jnp.exp(g_last[..., None] - gc)[..., None]

    def step(state, xs):
        u_i, w_i, qd_i, kd_i, at_i, gl_i = xs
        v_new = u_i - jnp.einsum('bhck,bhkv->bhcv', w_i, state)
        o_i = jnp.einsum('bhck,bhkv->bhcv', qd_i, state) + jnp.einsum('bhij,bhjv->bhiv', at_i, v_new)
        state = state * jnp.exp(gl_i)[..., None, None] + jnp.einsum('bhck,bhcv->bhkv', kd_i, v_new)
        return state, o_i

    h_fin, o = lax.scan(step, h0, (u, w, q_dec, k_dec, attn, g_last))
    o = jnp.swapaxes(jnp.moveaxis(o, 0, 1), 2, 3).reshape(b, s, h, dv)
    return o, h_fin


def gdn_branch(q_in, k_in, v_in, z, a_f, a_b, b_f, b_b, h0, p):
    b, s, _ = q_in.shape
    f32 = jnp.float32
    qkv = jax.nn.silu(short_conv(jnp.concatenate([q_in, k_in, v_in], axis=-1), p['conv_qkv_w'])).astype(f32)
    q = l2_normalize(qkv[..., :GDN_QK_W].reshape(b, s, GDN_HEADS, GDN_DK)) * (GDN_DK ** -0.5)
    k = l2_normalize(qkv[..., GDN_QK_W:2 * GDN_QK_W].reshape(b, s, GDN_HEADS, GDN_DK))
    v = qkv[..., 2 * GDN_QK_W:].reshape(b, s, GDN_HEADS, GDN_DV)
    a_log = p['a_log'].astype(f32)
    dt_bias = p['dt_bias'].astype(f32)
    g_f = -jnp.exp(a_log[0]) * jax.nn.softplus(a_f.astype(f32) + dt_bias[0])
    g_b = -jnp.exp(a_log[1]) * jax.nn.softplus(a_b.astype(f32) + dt_bias[1])
    beta_f = jax.nn.sigmoid(b_f.astype(f32))
    beta_b = jax.nn.sigmoid(b_b.astype(f32))
    h0 = h0.astype(f32)
    o_f, h_f = gated_delta_chunked(q, k, v, g_f, beta_f, h0[:, 0])
    rev = lambda a: jnp.flip(a, axis=1)
    o_r, h_r = gated_delta_chunked(rev(q), rev(k), rev(v), rev(g_b), rev(beta_b), h0[:, 1])
    o = rms_norm(o_f + rev(o_r), p['gdn_norm_g']).reshape(b, s, GDN_V_W).astype(z.dtype) * jax.nn.silu(z)
    return o, jnp.stack([h_f, h_r], axis=1)


def merge_branches(o_a, o_b, o_c, merge_g, p):
    s_a, s_b, s_c = jnp.split(jax.nn.sigmoid(merge_g), N_BRANCH, axis=-1)
    m = s_a * (o_a @ p['w_pa']) + s_b * (o_b @ p['w_pb']) + s_c * (o_c @ p['w_pc'])
    return m @ p['w_o']


def context_layer(x, cond, p):
    b, s, _ = x.shape
    parts, res_gate = adaln_projection(x, cond, p)
    cq, ckv, kpe, gate_a, b_in, c_in, x_in, gate_b, q_in, k_in, v_in, z, a_f, a_b, b_f, b_b, merge_g = parts
    q_nope, q_rope, ckv_n = mla_queries_and_latent(cq, ckv, p)
    k_nope, v = mla_expand(ckv_n, p['w_ukv'])
    o_a = mla_attention(q_nope, q_rope, k_nope, kpe, v) * jax.nn.silu(gate_a)
    o_b = conv_branch(b_in, c_in, x_in, gate_b, p['conv_b_w'])
    h0 = jnp.zeros((b, 2, GDN_HEADS, GDN_DK, GDN_DV), jnp.float32)
    o_c, h_fin = gdn_branch(q_in, k_in, v_in, z, a_f, a_b, b_f, b_b, h0, p)
    x = x + res_gate * merge_branches(o_a, o_b, o_c, merge_g, p)
    return x, ckv_n, kpe, h_fin.astype(x.dtype)


def latent_layer(x, cond, ckv_ctx, kpe_ctx, h_ctx, cos, sin, p):
    parts, res_gate = adaln_projection(x, cond, p)
    cq, ckv, kpe, gate_a, b_in, c_in, x_in, gate_b, q_in, k_in, v_in, z, a_f, a_b, b_f, b_b, merge_g = parts
    q_nope, q_rope, ckv_n = mla_queries_and_latent(cq, ckv, p)
    q_rope = apply_rope(q_rope, cos[:, None, :], sin[:, None, :])
    kpe = apply_rope(kpe, cos, sin)
    k_nope_l, v_l = mla_expand(ckv_n, p['w_ukv'])
    k_nope_c, v_c = mla_expand(ckv_ctx.astype(x.dtype), p['w_ukv'])
    k_nope = jnp.concatenate([k_nope_c, k_nope_l], axis=1)
    k_rope = jnp.concatenate([kpe_ctx.astype(x.dtype), kpe], axis=1)
    v = jnp.concatenate([v_c, v_l], axis=1)
    o_a = mla_attention(q_nope, q_rope, k_nope, k_rope, v) * jax.nn.silu(gate_a)
    o_b = conv_branch(b_in, c_in, x_in, gate_b, p['conv_b_w'])
    o_c, _ = gdn_branch(q_in, k_in, v_in, z, a_f, a_b, b_f, b_b, h_ctx, p)
    return x + res_gate * merge_branches(o_a, o_b, o_c, merge_g, p)


def setup_inputs(seed: int = 0) -> dict:
    key = jax.random.key(seed)
    ks = jax.random.split(key, 32)
    f32 = jnp.float32
    D = D_MODEL

    def nrm(k, shape, scale):
        return jax.random.normal(k, shape, f32) * scale

    a_vals = jax.random.uniform(ks[17], (DEPTH, 2, GDN_HEADS), f32, 1.0, 16.0)
    dt = jnp.exp(jax.random.uniform(ks[18], (DEPTH, 2, GDN_HEADS), f32, math.log(1e-3), math.log(1e-1)))
    return {
        'x_prompt': nrm(ks[0], (BATCH, SEQ, D), 1.0),
        'x_sample': nrm(ks[1], (DEC_BATCH, DEC_SEQ, D), 1.0),
        'c': nrm(ks[2], (DEC_BATCH, D), 1.0),
        'cache_ckv': nrm(ks[3], (DEC_BATCH, DEPTH, PAST_LEN, KV_LORA), 1.0),
        'cache_kpe': nrm(ks[4], (DEC_BATCH, DEPTH, PAST_LEN, QK_ROPE), 1.0),
        'state_gdn': nrm(ks[5], (DEC_BATCH, DEPTH, 2, GDN_HEADS, GDN_DK, GDN_DV), 0.1),
        'c_ctx': nrm(ks[6], (D,), 1.0),
        'norm_g': 1.0 + nrm(ks[7], (DEPTH, D), 0.02),
        'w_ada': nrm(ks[8], (DEPTH, D, 3 * D), 0.5 * D ** -0.5),
        'b_ada': nrm(ks[9], (DEPTH, 3 * D), 0.02),
        'w_in': nrm(ks[10], (DEPTH, D, D_IN), D ** -0.5),
        'q_norm_g': 1.0 + nrm(ks[11], (DEPTH, Q_LORA), 0.02),
        'kv_norm_g': 1.0 + nrm(ks[12], (DEPTH, KV_LORA), 0.02),
        'w_uq': nrm(ks[13], (DEPTH, Q_LORA, MLA_HEADS * (QK_NOPE + QK_ROPE)), Q_LORA ** -0.5),
        'w_ukv': nrm(ks[14], (DEPTH, KV_LORA, MLA_HEADS * (QK_NOPE + V_HEAD)), KV_LORA ** -0.5),
        'conv_b_w': nrm(ks[15], (DEPTH, CONV_K, CONV_W), CONV_K ** -0.5),
        'conv_qkv_w': nrm(ks[16], (DEPTH, CONV_K, 2 * GDN_QK_W + GDN_V_W), CONV_K ** -0.5),
        'a_log': jnp.log(a_vals),
        'dt_bias': dt + jnp.log(-jnp.expm1(-dt)),
        'gdn_norm_g': 1.0 + nrm(ks[19], (DEPTH, GDN_DV), 0.02),
        'w_pa': nrm(ks[20], (DEPTH, MLA_W, D), MLA_W ** -0.5),
        'w_pb': nrm(ks[21], (DEPTH, CONV_W, D), CONV_W ** -0.5),
        'w_pc': nrm(ks[22], (DEPTH, GDN_V_W, D), GDN_V_W ** -0.5),
        'w_o': nrm(ks[23], (DEPTH, D, D), D ** -0.5),
        'final_norm_g': 1.0 + nrm(ks[24], (D,), 0.02),
    }


def reference(x_prompt, x_sample, c, cache_ckv, cache_kpe, state_gdn, c_ctx, norm_g, w_ada, b_ada,
              w_in, q_norm_g, kv_norm_g, w_uq, w_ukv, conv_b_w, conv_qkv_w, a_log, dt_bias,
              gdn_norm_g, w_pa, w_pb, w_pc, w_o, final_norm_g):
    cond_ctx = jnp.broadcast_to(c_ctx[None, :], (x_prompt.shape[0], D_MODEL))
    cos, sin = axial_rope_tables(x_sample.shape[1])
    xp, xs = x_prompt, x_sample
    ckv_out, kpe_out, st_out = [], [], []
    for l in range(DEPTH):
        p = {'norm_g': norm_g[l], 'w_ada': w_ada[l], 'b_ada': b_ada[l], 'w_in': w_in[l],
             'q_norm_g': q_norm_g[l], 'kv_norm_g': kv_norm_g[l], 'w_uq': w_uq[l], 'w_ukv': w_ukv[l],
             'conv_b_w': conv_b_w[l], 'conv_qkv_w': conv_qkv_w[l], 'a_log': a_log[l],
             'dt_bias': dt_bias[l], 'gdn_norm_g': gdn_norm_g[l], 'w_pa': w_pa[l], 'w_pb': w_pb[l],
             'w_pc': w_pc[l], 'w_o': w_o[l]}
        xp, ckv_n, kpe, h_fin = context_layer(xp, cond_ctx, p)
        ckv_out.append(ckv_n)
        kpe_out.append(kpe)
        st_out.append(h_fin)
        xs = latent_layer(xs, c, cache_ckv[:, l], cache_kpe[:, l], state_gdn[:, l], cos, sin, p)
    y_prompt = rms_norm(xp, final_norm_g)
    y_sample = rms_norm(xs, final_norm_g)
    new_cache_ckv = jnp.stack(ckv_out, axis=1)
    new_cache_kpe = jnp.stack(kpe_out, axis=1)
    new_state_gdn = jnp.stack(st_out, axis=1)
    return (y_prompt, y_sample, new_cache_ckv, new_cache_kpe, new_state_gdn)
```

```python
import functools
import math

import jax
import jax.numpy as jnp
from jax import lax
from jax.experimental import pallas as pl
from jax.experimental.pallas import tpu as pltpu

F32 = jnp.float32
BF16 = jnp.bfloat16

D_MODEL = 1024
DEPTH = 2
GRID_W = 64
EPS = 1e-6
HEADS = 8
Q_LORA = 384
KV_LORA = 256
QK_NOPE = 64
QK_ROPE = 32
V_HEAD = 64
MLA_W = HEADS * V_HEAD
ROPE_THETA = 10000.0
CONV_W = 512
GDN_DK = 64
GDN_DV = 64
GDN_W = HEADS * GDN_DK
GDN_CHUNK = 64
LANES = 128
HEAD_PAD = 128
QK_W = HEADS * HEAD_PAD
SM_SCALE = (QK_NOPE + QK_ROPE) ** -0.5
COND_ROWS = 16
VMEM_LIMIT = 56 * 1024 * 1024

_O_CQ, _O_CKV, _O_KPE, _O_GA = 0, 384, 640, 672
_O_B, _O_C, _O_X, _O_GB = 1184, 1696, 2208, 2720
_O_Q, _O_K, _O_V, _O_Z = 3232, 3744, 4256, 4768
_O_AB, _O_MG, _O_END = 5280, 5312, 8384
W_MLA = Q_LORA + KV_LORA + LANES
W_QKV = 3 * GDN_W
W_CONV = 4 * CONV_W
W_GZ = MLA_W + GDN_W
W_MG = 3 * D_MODEL
GROUP_WIDTHS = (W_MLA, W_QKV, W_CONV, W_GZ, W_MG)
W_IN_PACKED = sum(GROUP_WIDTHS)
GATE_LANE = 64


def _sigmoid(x):
    return 1.0 / (1.0 + jnp.exp(-x))


def _silu(x):
    return x * _sigmoid(x)


def _rms(x):
    return x * lax.rsqrt(jnp.mean(x * x, axis=-1, keepdims=True) + EPS)


def _dot(a, b):
    return jnp.dot(a, b, preferred_element_type=F32)


def _dot_nt(a, b):
    return lax.dot_general(a, b, (((1,), (1,)), ((), ())), preferred_element_type=F32)


def _split3(x):
    hi = x.astype(BF16)
    r = x - hi.astype(F32)
    mid = r.astype(BF16)
    lo = (r - mid.astype(F32)).astype(BF16)
    return hi, mid, lo


def _head_sums(x2, ones_bd):
    hi = x2.astype(BF16)
    lo = (x2 - hi.astype(F32)).astype(BF16)
    return _dot(hi, ones_bd) + _dot(lo, ones_bd)


def _cparams(sem):
    return pltpu.CompilerParams(dimension_semantics=sem, vmem_limit_bytes=VMEM_LIMIT)


def _const_spec(shape):
    nd = len(shape)
    return pl.BlockSpec(shape, lambda *_: (0,) * nd)


def _ada_kernel(c_ref, w_ref, b_ref, o_ref):
    sc = _silu(c_ref[...])
    o_ref[...] = jnp.dot(sc, w_ref[...], preferred_element_type=F32, precision=lax.Precision.HIGHEST) + b_ref[...]


def _ada_mod(cond, w_ada, b_ada):
    depth = w_ada.shape[0]
    return pl.pallas_call(
        _ada_kernel,
        out_shape=jax.ShapeDtypeStruct((depth, COND_ROWS, 3 * D_MODEL), F32),
        grid=(depth, 3),
        in_specs=[
            pl.BlockSpec((COND_ROWS, D_MODEL), lambda l, j: (0, 0)),
            pl.BlockSpec((None, D_MODEL, D_MODEL), lambda l, j: (l, 0, j)),
            pl.BlockSpec((None, 1, D_MODEL), lambda l, j: (l, 0, j)),
        ],
        out_specs=pl.BlockSpec((None, COND_ROWS, D_MODEL), lambda l, j: (l, 0, j)),
        compiler_params=_cparams(("arbitrary", "arbitrary")),
        name="ada_mod",
    )(cond, w_ada, b_ada.reshape(depth, 1, 3 * D_MODEL))


IN_TM = 256
IN_CHUNK = 768


def _inproj_kernel(x_ref, mod_ref, g_ref, w_ref, *rest):
    outs, h_scr = rest[:-1], rest[-1]
    x = x_ref[...]
    shift = mod_ref[:, 0:D_MODEL]
    scale = mod_ref[:, D_MODEL:2 * D_MODEL]
    h = (_rms(x) * g_ref[...]) * (1.0 + scale) + shift
    h_scr[...] = h.astype(BF16)
    off = 0
    for o_ref in outs:
        n = o_ref.shape[1]
        for c0 in range(0, n, IN_CHUNK):
            c1 = min(c0 + IN_CHUNK, n)
            o_ref[:, c0:c1] = _dot(h_scr[...], w_ref[:, off + c0:off + c1])
        off += n


def _inproj(x2d, mod_l, norm_g, w_in_p, seq, cond_row0, cond_per_seq):
    t = x2d.shape[0]
    tps = seq // IN_TM
    mod3 = mod_l.reshape(COND_ROWS, 1, 3 * D_MODEL)
    return pl.pallas_call(
        _inproj_kernel,
        out_shape=[jax.ShapeDtypeStruct((t, n), F32) for n in GROUP_WIDTHS],
        grid=(t // IN_TM,),
        in_specs=[
            pl.BlockSpec((IN_TM, D_MODEL), lambda i: (i, 0)),
            pl.BlockSpec((None, 1, 3 * D_MODEL), lambda i: (cond_row0 + (i // tps) * cond_per_seq, 0, 0)),
            _const_spec((1, D_MODEL)),
            pl.BlockSpec((D_MODEL, W_IN_PACKED), lambda i: (0, 0), pipeline_mode=pl.Buffered(1)),
        ],
        out_specs=[pl.BlockSpec((IN_TM, n), lambda i: (i, 0)) for n in GROUP_WIDTHS],
        scratch_shapes=[pltpu.VMEM((IN_TM, D_MODEL), BF16)],
        compiler_params=_cparams(("parallel",)),
        name="inproj",
    )(x2d, mod3, norm_g.reshape(1, D_MODEL), w_in_p)


PREP_TM = 256


def _shift_rows(x, prev_row, next_row):
    n = x.shape[0]
    row = lax.broadcasted_iota(jnp.int32, (n, 1), 0)
    down = jnp.where(row == 0, prev_row, pltpu.roll(x, 1, 0))
    up = jnp.where(row == n - 1, next_row, pltpu.roll(x, n - 1, 0))
    return down, up


def _prep_kernel(*refs, use_rope, tiles_per_seq, emit_ckvn):
    it = iter(refs)
    mla_ref, qkv_ref, qkv_prev_ref, qkv_next_ref = next(it), next(it), next(it), next(it)
    if use_rope:
        m1_ref, m2_ref, m3_ref = next(it), next(it), next(it)
    gq_ref, gkv_ref, wa_ref = next(it), next(it), next(it)
    if use_rope:
        wb_ref = next(it)
    wke_ref, wv_ref, cw_ref, alog_ref, dtb_ref, ones_ref = (next(it) for _ in range(6))
    qf_ref, kf_ref, vf_ref = next(it), next(it), next(it)
    if emit_ckvn:
        ckvn_ref = next(it)
    qg_ref, kg_ref, vg_ref, gb_ref = next(it), next(it), next(it), next(it)

    i = pl.program_id(0)
    first = (i % tiles_per_seq) == 0
    last = (i % tiles_per_seq) == tiles_per_seq - 1

    cq = mla_ref[:, 0:Q_LORA]
    ckv = mla_ref[:, Q_LORA:Q_LORA + KV_LORA]
    small = mla_ref[:, Q_LORA + KV_LORA:W_MLA]
    qn = (_rms(cq) * gq_ref[...]).astype(BF16)
    qa = _dot(qn, wa_ref[...])
    if use_rope:
        qb = _dot(qn, wb_ref[...])
        m1 = m1_ref[...]
        m2 = m2_ref[...]
        for h in range(HEADS):
            sl = slice(h * HEAD_PAD, (h + 1) * HEAD_PAD)
            qf_ref[:, sl] = (qa[:, sl] * m1 + qb[:, sl] * m2).astype(BF16)
        kpe_t = small * m3_ref[...]
    else:
        qf_ref[...] = (qa * SM_SCALE).astype(BF16)
        kpe_t = small
    ckvn = _rms(ckv) * gkv_ref[...]
    if emit_ckvn:
        ckvn_ref[...] = ckvn
    ckvn_b = ckvn.astype(BF16)
    kin = jnp.concatenate([ckvn_b, kpe_t.astype(BF16)], axis=1)
    kf_ref[...] = _dot(kin, wke_ref[...]).astype(BF16)
    vf_ref[...] = _dot(ckvn_b, wv_ref[...]).astype(BF16)

    z1 = small + dtb_ref[...]
    sp = jnp.maximum(z1, 0.0) + jnp.log1p(jnp.exp(-jnp.abs(z1)))
    g = -jnp.exp(alog_ref[...]) * sp
    lane = lax.broadcasted_iota(jnp.int32, (1, LANES), 1)
    gb_ref[...] = jnp.where(lane < GATE_LANE + 2 * HEADS, g, _sigmoid(small))

    ones_bd = ones_ref[...]
    for part, o_ref in enumerate((qg_ref, kg_ref, vg_ref)):
        sl = slice(part * GDN_W, (part + 1) * GDN_W)
        x = qkv_ref[:, sl]
        prev_row = jnp.where(first, 0.0, qkv_prev_ref[7:8, sl])
        next_row = jnp.where(last, 0.0, qkv_next_ref[0:1, sl])
        down, up = _shift_rows(x, prev_row, next_row)
        y = down * cw_ref[0:1, sl] + x * cw_ref[1:2, sl] + up * cw_ref[2:3, sl]
        y = _silu(y)
        if part == 0:
            y = y * lax.rsqrt(_head_sums(y * y, ones_bd) + EPS) * (GDN_DK ** -0.5)
        elif part == 1:
            y = y * lax.rsqrt(_head_sums(y * y, ones_bd) + EPS)
        o_ref[...] = y


def _halo_specs(tm, width, n_rows, col_block=0):
    r = tm // 8
    nb = n_rows // 8
    prev = pl.BlockSpec((8, width), lambda i: (jnp.maximum(i * r - 1, 0), col_block))
    nxt = pl.BlockSpec((8, width), lambda i: (jnp.minimum((i + 1) * r, nb - 1), col_block))
    return prev, nxt


def _prep(u_mla, u_qkv, lw, seq, rope_tabs, emit_ckvn):
    t = u_mla.shape[0]
    tm = PREP_TM
    use_rope = rope_tabs is not None
    tps = seq // tm
    row = lambda w: pl.BlockSpec((tm, w), lambda i: (i, 0))
    prev_spec, next_spec = _halo_specs(tm, W_QKV, t)
    in_specs = [row(W_MLA), row(W_QKV), prev_spec, next_spec]
    args = [u_mla, u_qkv, u_qkv, u_qkv]
    if use_rope:
        in_specs += [pl.BlockSpec((tm, LANES), lambda i: (i % tps, 0))] * 3
        args += list(rope_tabs)
    in_specs += [_const_spec((1, Q_LORA)), _const_spec((1, KV_LORA)), _const_spec((Q_LORA, QK_W))]
    args += [lw["q_norm_g"], lw["kv_norm_g"], lw["w_qa"]]
    if use_rope:
        in_specs.append(_const_spec((Q_LORA, QK_W)))
        args.append(lw["w_qb"])
    in_specs += [_const_spec((KV_LORA + LANES, QK_W)), _const_spec((KV_LORA, MLA_W)), _const_spec((3, W_QKV)),
                 _const_spec((1, LANES)), _const_spec((1, LANES)), _const_spec((GDN_W, GDN_W))]
    args += [lw["w_ke_rope"] if use_rope else lw["w_ke"], lw["w_v"], lw["conv_qkv_w"], lw["alog_lane"], lw["dtb_lane"],
             lw["ones_bd"]]
    out_shape = [jax.ShapeDtypeStruct((t, QK_W), BF16), jax.ShapeDtypeStruct((t, QK_W), BF16),
                 jax.ShapeDtypeStruct((t, MLA_W), BF16)]
    out_specs = [row(QK_W), row(QK_W), row(MLA_W)]
    if emit_ckvn:
        out_shape.append(jax.ShapeDtypeStruct((t, KV_LORA), F32))
        out_specs.append(row(KV_LORA))
    out_shape += [jax.ShapeDtypeStruct((t, GDN_W), F32)] * 3 + [jax.ShapeDtypeStruct((t, LANES), F32)]
    out_specs += [row(GDN_W)] * 3 + [row(LANES)]
    outs = pl.pallas_call(
        functools.partial(_prep_kernel, use_rope=use_rope, tiles_per_seq=tps, emit_ckvn=emit_ckvn),
        out_shape=out_shape,
        grid=(t // tm,),
        in_specs=in_specs,
        out_specs=out_specs,
        compiler_params=_cparams(("parallel",)),
        name="prep",
    )(*args)
    return outs


def _ctxkv_kernel(ckv_ref, kpe_ref, wke_ref, wv_ref, kf_ref, vf_ref):
    ckv_b = ckv_ref[...].astype(BF16)
    kin = jnp.concatenate([ckv_b, kpe_ref[...].astype(BF16)], axis=1)
    kf_ref[...] = _dot(kin, wke_ref[...]).astype(BF16)
    vf_ref[...] = _dot(ckv_b, wv_ref[...]).astype(BF16)


def _ctx_kv(ckv2d, kpe_lanes, lw):
    t = ckv2d.shape[0]
    tm = 256
    row = lambda w: pl.BlockSpec((tm, w), lambda i: (i, 0))
    return pl.pallas_call(
        _ctxkv_kernel,
        out_shape=[jax.ShapeDtypeStruct((t, QK_W), BF16), jax.ShapeDtypeStruct((t, MLA_W), BF16)],
        grid=(t // tm,),
        in_specs=[row(KV_LORA), row(LANES), _const_spec((KV_LORA + LANES, QK_W)), _const_spec((KV_LORA, MLA_W))],
        out_specs=[row(QK_W), row(MLA_W)],
        compiler_params=_cparams(("parallel",)),
        name="ctx_kv",
    )(ckv2d, kpe_lanes, lw["w_ke"], lw["w_v"])


def _attn_kernel(*refs, n_parts):
    q_ref = refs[0]
    k_refs = refs[1:1 + n_parts]
    v_refs = refs[1 + n_parts:1 + 2 * n_parts]
    o_ref = refs[1 + 2 * n_parts]
    tq = q_ref.shape[0]
    lane = lax.broadcasted_iota(jnp.int32, (1, 2 * V_HEAD), 1)
    out = None
    for j in range(2):
        sl = slice(j * HEAD_PAD, (j + 1) * HEAD_PAD)
        q = q_ref[:, sl]
        s = [_dot_nt(q, k_ref[:, sl]) for k_ref in k_refs]
        m = s[0].max(axis=-1, keepdims=True)
        for sp in s[1:]:
            m = jnp.maximum(m, sp.max(axis=-1, keepdims=True))
        l = jnp.zeros((tq, 1), F32)
        acc = jnp.zeros((tq, 2 * V_HEAD), F32)
        for sp, v_ref in zip(s, v_refs):
            p = jnp.exp(sp - m)
            l = l + p.sum(axis=-1, keepdims=True)
            acc = acc + _dot(p.astype(BF16), v_ref[...])
        oj = acc / l
        out = oj if out is None else jnp.where(lane < V_HEAD, out, oj)
    o_ref[...] = out


def _attention(qf, kv_parts, n_batch, seq_q, tq):
    n_parts = len(kv_parts)
    nq = seq_q // tq
    in_specs = [pl.BlockSpec((tq, 2 * HEAD_PAD), lambda b, hp, qi: (b * nq + qi, hp))]
    in_specs += [pl.BlockSpec((sk, 2 * HEAD_PAD), lambda b, hp, qi: (b, hp)) for _, _, sk in kv_parts]
    in_specs += [pl.BlockSpec((sk, 2 * V_HEAD), lambda b, hp, qi: (b, hp)) for _, _, sk in kv_parts]
    return pl.pallas_call(
        functools.partial(_attn_kernel, n_parts=n_parts),
        out_shape=jax.ShapeDtypeStruct((n_batch * seq_q, MLA_W), F32),
        grid=(n_batch, HEADS // 2, nq),
        in_specs=in_specs,
        out_specs=pl.BlockSpec((tq, 2 * V_HEAD), lambda b, hp, qi: (b * nq + qi, hp)),
        compiler_params=_cparams(("parallel", "parallel", "arbitrary")),
        name="attention",
    )(qf, *[k for k, _, _ in kv_parts], *[v for _, v, _ in kv_parts])


def _unit_tri_inverse(a, eye, blk16, blk32):
    d1 = jnp.where(blk16, a, 0.0).astype(BF16)
    d2 = _dot(d1, d1)
    d2b = d2.astype(BF16)
    d4 = _dot(d2b, d2b)
    d4b = d4.astype(BF16)
    d8 = _dot(d4b, d4b)
    t = eye - jnp.where(blk16, a, 0.0)
    t = t + _dot(t.astype(BF16), d2b)
    t = t + _dot(t.astype(BF16), d4b)
    t = t + _dot(t.astype(BF16), d8.astype(BF16))
    for off in (jnp.where(blk32 & ~blk16, a, 0.0), jnp.where(blk32, 0.0, a)):
        tb = t.astype(BF16)
        t = t - _dot(_dot(tb, off.astype(BF16)).astype(BF16), tb)
    return t


def _gdn_kernel(*refs, zero_init, n_chunks):
    it = iter(refs)
    dir_in = [tuple(next(it) for _ in range(4)) for _ in range(2)]
    h0_ref = None if zero_init else next(it)
    o_refs = (next(it), next(it))
    hfin_ref = next(it)
    s_scr = next(it)
    c = pl.program_id(1)
    cs = GDN_CHUNK

    @pl.when(c == 0)
    def _():
        if zero_init:
            s_scr[...] = jnp.zeros_like(s_scr)
        else:
            s_scr[...] = h0_ref[...]

    ri = lax.broadcasted_iota(jnp.int32, (cs, cs), 0)
    ci = lax.broadcasted_iota(jnp.int32, (cs, cs), 1)
    eye = (ri == ci).astype(F32)
    blk16 = (ri // 16) == (ci // 16)
    blk32 = (ri // 32) == (ci // 32)
    for d in range(2):
        q_ref, k_ref, v_ref, g_ref = dir_in[d]
        incl = (ri >= ci) if d == 0 else (ri <= ci)
        strict = (ri > ci) if d == 0 else (ri < ci)
        gates = g_ref[...]
        csum = incl.astype(BF16)
        ghi, gmid, glo = _split3(gates)
        gc_all = _dot(csum, ghi) + _dot(csum, gmid) + _dot(csum, glo)
        gc_all_t = gc_all.T
        g_last_all = gc_all[cs - 1:cs, :] if d == 0 else gc_all[0:1, :]
        outs = []
        for h in range(HEADS):
            lg = GATE_LANE + HEADS * d + h
            lb = GATE_LANE + 2 * HEADS + HEADS * d + h
            gc = gc_all[:, lg:lg + 1]
            gr = gc_all_t[lg:lg + 1, :]
            beta = gates[:, lb:lb + 1]
            g_last = g_last_all[:, lg:lg + 1]
            sl = slice(h * GDN_DK, (h + 1) * GDN_DK)
            qh = q_ref[:, sl]
            kh = k_ref[:, sl]
            vh = v_ref[:, sl]
            khb = kh.astype(BF16)
            decay = jnp.exp(jnp.where(incl, gc - gr, -jnp.inf))
            egc = jnp.exp(gc)
            a_mat = jnp.where(strict, beta * _dot_nt(khb, khb) * decay, 0.0)
            attn = jnp.where(incl, _dot_nt(qh.astype(BF16), khb) * decay, 0.0)
            rhs = jnp.concatenate([vh * beta, kh * (beta * egc)], axis=1)
            x = _dot(_unit_tri_inverse(a_mat, eye, blk16, blk32).astype(BF16), rhs.astype(BF16))
            u = x[:, :GDN_DV]
            w = x[:, GDN_DV:]
            s = s_scr[d, h]
            sb = s.astype(BF16)
            v_new = u - _dot(w.astype(BF16), sb)
            v_new_b = v_new.astype(BF16)
            outs.append(_dot((qh * egc).astype(BF16), sb) + _dot(attn.astype(BF16), v_new_b))
            k_dec_t = (kh * jnp.exp(g_last - gc)).T.astype(BF16)
            s_scr[d, h] = s * jnp.exp(g_last) + _dot(k_dec_t, v_new_b)
        o_refs[d][...] = jnp.concatenate(outs, axis=1)

    @pl.when(c == n_chunks - 1)
    def _():
        hfin_ref[...] = s_scr[...]


def _gdn(qg, kg, vg, gb, h0, n_batch, seq):
    nc = seq // GDN_CHUNK
    zero_init = h0 is None
    fwd = lambda w: pl.BlockSpec((GDN_CHUNK, w), lambda b, c: (b * nc + c, 0))
    bwd = lambda w: pl.BlockSpec((GDN_CHUNK, w), lambda b, c: (b * nc + nc - 1 - c, 0))
    in_specs = [fwd(GDN_W), fwd(GDN_W), fwd(GDN_W), fwd(LANES), bwd(GDN_W), bwd(GDN_W), bwd(GDN_W), bwd(LANES)]
    args = [qg, kg, vg, gb, qg, kg, vg, gb]
    st_spec = pl.BlockSpec((None, 2, HEADS, GDN_DK, GDN_DV), lambda b, c: (b, 0, 0, 0, 0))
    if not zero_init:
        in_specs.append(st_spec)
        args.append(h0)
    t = n_batch * seq
    return pl.pallas_call(
        functools.partial(_gdn_kernel, zero_init=zero_init, n_chunks=nc),
        out_shape=[jax.ShapeDtypeStruct((t, GDN_W), F32), jax.ShapeDtypeStruct((t, GDN_W), F32),
                   jax.ShapeDtypeStruct((n_batch, 2, HEADS, GDN_DK, GDN_DV), F32)],
        grid=(n_batch, nc),
        in_specs=in_specs,
        out_specs=[fwd(GDN_W), bwd(GDN_W), st_spec],
        scratch_shapes=[pltpu.VMEM((2, HEADS, GDN_DK, GDN_DV), F32)],
        compiler_params=_cparams(("parallel", "arbitrary")),
        name="gdn",
    )(*args)


MERGE_TM = 256


def _merge_kernel(x_ref, mod_ref, oa_ref, gz_ref, conv_ref, cx_prev_ref, cx_next_ref, of_ref, ob_ref, mg_ref,
                  wpa_ref, wpb_ref, wpc_ref, wo_ref, cw_ref, gn_ref, ones_ref, fin_ref, o_ref, *,
                  tiles_per_seq, final_norm):
    i = pl.program_id(0)
    first = (i % tiles_per_seq) == 0
    last = (i % tiles_per_seq) == tiles_per_seq - 1
    o_a = oa_ref[...] * _silu(gz_ref[:, 0:MLA_W])
    pa = _dot(o_a.astype(BF16), wpa_ref[...])
    cx = conv_ref[:, 0:CONV_W] * conv_ref[:, CONV_W:2 * CONV_W]
    prev_row = jnp.where(first, 0.0, cx_prev_ref[7:8, 0:CONV_W] * cx_prev_ref[7:8, CONV_W:2 * CONV_W])
    next_row = jnp.where(last, 0.0, cx_next_ref[0:1, 0:CONV_W] * cx_next_ref[0:1, CONV_W:2 * CONV_W])
    down, up = _shift_rows(cx, prev_row, next_row)
    conv = down * cw_ref[0:1, :] + cx * cw_ref[1:2, :] + up * cw_ref[2:3, :]
    o_b = conv_ref[:, 2 * CONV_W:3 * CONV_W] * conv * _silu(conv_ref[:, 3 * CONV_W:4 * CONV_W])
    pb = _dot(o_b.astype(BF16), wpb_ref[...])
    og = of_ref[...] + ob_ref[...]
    ms = _head_sums(og * og, ones_ref[...]) * (1.0 / GDN_DV)
    o_c = og * lax.rsqrt(ms + EPS) * gn_ref[...] * _silu(gz_ref[:, MLA_W:W_GZ])
    pc = _dot(o_c.astype(BF16), wpc_ref[...])
    m = (_sigmoid(mg_ref[:, 0:D_MODEL]) * pa + _sigmoid(mg_ref[:, D_MODEL:2 * D_MODEL]) * pb
         + _sigmoid(mg_ref[:, 2 * D_MODEL:3 * D_MODEL]) * pc)
    r = _dot(m.astype(BF16), wo_ref[...])
    y = x_ref[...] + mod_ref[:, 2 * D_MODEL:3 * D_MODEL] * r
    if final_norm:
        y = _rms(y) * fin_ref[...]
    o_ref[...] = y


def _merge(x2d, mod_l, o_a, u_gz, u_conv, o_f, o_b, u_mg, lw, final_g, seq, cond_row0, cond_per_seq, final_norm):
    t = x2d.shape[0]
    tm = MERGE_TM
    tps = seq // tm
    row = lambda w: pl.BlockSpec((tm, w), lambda i: (i, 0))
    prev_spec, next_spec = _halo_specs(tm, 2 * CONV_W, t)
    mod3 = mod_l.reshape(COND_ROWS, 1, 3 * D_MODEL)
    return pl.pallas_call(
        functools.partial(_merge_kernel, tiles_per_seq=tps, final_norm=final_norm),
        out_shape=jax.ShapeDtypeStruct((t, D_MODEL), F32),
        grid=(t // tm,),
        in_specs=[
            row(D_MODEL),
            pl.BlockSpec((None, 1, 3 * D_MODEL), lambda i: (cond_row0 + (i // tps) * cond_per_seq, 0, 0)),
            row(MLA_W), row(W_GZ), row(W_CONV), prev_spec, next_spec, row(GDN_W), row(GDN_W), row(W_MG),
            _const_spec((MLA_W, D_MODEL)), _const_spec((CONV_W, D_MODEL)), _const_spec((GDN_W, D_MODEL)),
            _const_spec((D_MODEL, D_MODEL)), _const_spec((3, CONV_W)), _const_spec((1, GDN_W)),
            _const_spec((GDN_W, GDN_W)), _const_spec((1, D_MODEL)),
        ],
        out_specs=row(D_MODEL),
        compiler_params=_cparams(("parallel",)),
        name="merge",
    )(x2d, mod3, o_a, u_gz, u_conv, u_conv, u_conv, o_f, o_b, u_mg, lw["w_pa"], lw["w_pb"], lw["w_pc"], lw["w_o"],
      lw["conv_b_w"], lw["gdn_norm_lane"], lw["ones_bd"], final_g.reshape(1, D_MODEL))


_EVEN_ODD = list(range(0, QK_ROPE, 2)) + list(range(1, QK_ROPE, 2))
_SWAPPED = list(range(1, QK_ROPE, 2)) + list(range(0, QK_ROPE, 2))


def _pack_layer_weights(l, w_in, q_norm_g, kv_norm_g, w_uq, w_ukv, conv_b_w, conv_qkv_w, a_log, dt_bias, gdn_norm_g,
                        w_pa, w_pb, w_pc, w_o):
    wi = w_in[l]
    eo = jnp.array(_EVEN_ODD)
    sw = jnp.array(_SWAPPED)
    w_kpe = wi[:, _O_KPE:_O_GA]
    small = jnp.concatenate([w_kpe[:, eo], w_kpe[:, sw], wi[:, _O_AB:_O_MG], w_kpe], axis=1)
    w_in_p = jnp.concatenate([
        wi[:, _O_CQ:_O_KPE], small,
        wi[:, _O_Q:_O_Z],
        wi[:, _O_C:_O_GB], wi[:, _O_B:_O_C], wi[:, _O_GB:_O_Q],
        wi[:, _O_GA:_O_B], wi[:, _O_Z:_O_AB],
        wi[:, _O_MG:_O_END]], axis=1).astype(BF16)
    wq = w_uq[l].reshape(Q_LORA, HEADS, QK_NOPE + QK_ROPE)
    nope, rope = wq[:, :, :QK_NOPE], wq[:, :, QK_NOPE:]
    z32 = jnp.zeros((Q_LORA, HEADS, HEAD_PAD - QK_NOPE - QK_ROPE), F32)
    z64 = jnp.zeros((Q_LORA, HEADS, QK_NOPE), F32)
    w_qa = jnp.concatenate([nope, rope[:, :, eo], z32], axis=-1).reshape(Q_LORA, QK_W).astype(BF16)
    w_qb = jnp.concatenate([z64, rope[:, :, sw], z32], axis=-1).reshape(Q_LORA, QK_W).astype(BF16)
    wkv = w_ukv[l].reshape(KV_LORA, HEADS, QK_NOPE + V_HEAD)
    w_k = jnp.concatenate([wkv[:, :, :QK_NOPE], jnp.zeros((KV_LORA, HEADS, HEAD_PAD - QK_NOPE), F32)],
                          axis=-1).reshape(KV_LORA, QK_W)
    w_v = wkv[:, :, QK_NOPE:].reshape(KV_LORA, MLA_W).astype(BF16)
    src = jnp.arange(LANES)[:, None]
    dst = jnp.arange(QK_W)[None, :] % HEAD_PAD - QK_NOPE
    in_rope = (dst >= 0) & (dst < QK_ROPE)
    e_plain = (in_rope & (src == dst)).astype(F32)
    e_rope = (in_rope & ((src == dst) | (src == dst + QK_ROPE))).astype(F32)
    lane = jnp.arange(LANES)
    gate_idx = jnp.clip(lane - GATE_LANE, 0, 2 * HEADS - 1)
    is_g = (lane >= GATE_LANE) & (lane < GATE_LANE + 2 * HEADS)
    alog_lane = jnp.where(is_g, a_log[l].reshape(-1)[gate_idx], 0.0).reshape(1, LANES)
    dtb_lane = jnp.where(is_g, dt_bias[l].reshape(-1)[gate_idx], 0.0).reshape(1, LANES)
    hid = jnp.arange(GDN_W) // GDN_DK
    return {
        "w_in_p": w_in_p,
        "q_norm_g": q_norm_g[l].reshape(1, Q_LORA), "kv_norm_g": kv_norm_g[l].reshape(1, KV_LORA),
        "w_qa": w_qa, "w_qb": w_qb,
        "w_ke": jnp.concatenate([w_k, e_plain], axis=0).astype(BF16),
        "w_ke_rope": jnp.concatenate([w_k, e_rope], axis=0).astype(BF16),
        "w_v": w_v,
        "conv_qkv_w": conv_qkv_w[l], "conv_b_w": conv_b_w[l],
        "alog_lane": alog_lane, "dtb_lane": dtb_lane,
        "ones_bd": (hid[:, None] == hid[None, :]).astype(BF16),
        "gdn_norm_lane": jnp.tile(gdn_norm_g[l], HEADS).reshape(1, GDN_W),
        "w_pa": w_pa[l].astype(BF16), "w_pb": w_pb[l].astype(BF16), "w_pc": w_pc[l].astype(BF16),
        "w_o": w_o[l].astype(BF16),
    }


def _rope_tables(n_tokens):
    t = jnp.arange(n_tokens)
    row = (t // GRID_W).astype(F32)
    col = (t % GRID_W).astype(F32)
    n_freq = QK_ROPE // 4
    inv_freq = ROPE_THETA ** (-jnp.arange(n_freq, dtype=F32) / n_freq)
    ang = jnp.concatenate([row[:, None] * inv_freq, col[:, None] * inv_freq], axis=-1)
    cos, sin = jnp.cos(ang), jnp.sin(ang)
    cc = jnp.concatenate([cos, cos], axis=-1)
    ss = jnp.concatenate([-sin, sin], axis=-1)
    one = jnp.ones((n_tokens, QK_NOPE), F32)
    z32 = jnp.zeros((n_tokens, QK_ROPE), F32)
    z64 = jnp.zeros((n_tokens, QK_NOPE), F32)
    m1 = jnp.concatenate([one, cc, z32], axis=-1) * SM_SCALE
    m2 = jnp.concatenate([z64, ss, z32], axis=-1) * SM_SCALE
    m3 = jnp.concatenate([cc, ss, z64], axis=-1)
    return m1, m2, m3


def _layer(x2d, mod_l, lw, final_g, n_batch, seq, cond_row0, cond_per_seq, final_norm, rope_tabs=None,
           ctx_kv=None, h0=None, tq=256):
    is_ctx = ctx_kv is None
    u_mla, u_qkv, u_conv, u_gz, u_mg = _inproj(x2d, mod_l, lw["norm_g"], lw["w_in_p"], seq, cond_row0, cond_per_seq)
    outs = _prep(u_mla, u_qkv, lw, seq, rope_tabs, emit_ckvn=is_ctx)
    if is_ctx:
        qf, kf, vf, ckvn, qg, kg, vg, gb = outs
        kv_parts = [(kf, vf, seq)]
    else:
        qf, kf, vf, qg, kg, vg, gb = outs
        ckvn = None
        kv_parts = [ctx_kv, (kf, vf, seq)]
    o_a = _attention(qf, kv_parts, n_batch, seq, tq)
    o_f, o_b, h_fin = _gdn(qg, kg, vg, gb, h0, n_batch, seq)
    y = _merge(x2d, mod_l, o_a, u_gz, u_conv, o_f, o_b, u_mg, lw, final_g, seq, cond_row0, cond_per_seq, final_norm)
    return y, ckvn, u_mla[:, W_MLA - QK_ROPE:], h_fin


def kernel(x_prompt, x_sample, c, cache_ckv, cache_kpe, state_gdn, c_ctx, norm_g, w_ada, b_ada, w_in, q_norm_g,
           kv_norm_g, w_uq, w_ukv, conv_b_w, conv_qkv_w, a_log, dt_bias, gdn_norm_g, w_pa, w_pb, w_pc, w_o,
           final_norm_g):
    n_ctx, seq_ctx, _ = x_prompt.shape
    n_lat, seq_lat, _ = x_sample.shape
    past = cache_ckv.shape[2]
    depth = w_in.shape[0]
    assert n_lat + 1 <= COND_ROWS

    cond = jnp.zeros((COND_ROWS, D_MODEL), F32).at[:n_lat].set(c).at[n_lat].set(c_ctx)
    mod = _ada_mod(cond, w_ada, b_ada)
    rope_tabs = _rope_tables(seq_lat)
    eo = jnp.array(_EVEN_ODD)

    xp = x_prompt.reshape(n_ctx * seq_ctx, D_MODEL)
    xs = x_sample.reshape(n_lat * seq_lat, D_MODEL)
    ckv_out, kpe_out, st_out = [], [], []
    for l in range(depth):
        lw = _pack_layer_weights(l, w_in, q_norm_g, kv_norm_g, w_uq, w_ukv, conv_b_w, conv_qkv_w, a_log, dt_bias,
                                 gdn_norm_g, w_pa, w_pb, w_pc, w_o)
        lw["norm_g"] = norm_g[l]
        last = l == depth - 1
        xp, ckvn, kpe, h_fin = _layer(xp, mod[l], lw, final_norm_g, n_ctx, seq_ctx, n_lat, 0, last)
        ckv_out.append(ckvn.reshape(n_ctx, seq_ctx, KV_LORA))
        kpe_out.append(kpe.reshape(n_ctx, seq_ctx, QK_ROPE))
        st_out.append(h_fin)
        kpe_lanes = jnp.pad(cache_kpe[:, l][..., eo].reshape(n_lat * past, QK_ROPE), ((0, 0), (0, LANES - QK_ROPE)))
        kc, vc = _ctx_kv(cache_ckv[:, l].reshape(n_lat * past, KV_LORA), kpe_lanes, lw)
        xs, _, _, _ = _layer(xs, mod[l], lw, final_norm_g, n_lat, seq_lat, 0, 1, last, rope_tabs=rope_tabs,
                             ctx_kv=(kc, vc, past), h0=state_gdn[:, l], tq=512)
    return (xp.reshape(n_ctx, seq_ctx, D_MODEL), xs.reshape(n_lat, seq_lat, D_MODEL),
            jnp.stack(ckv_out, axis=1), jnp.stack(kpe_out, axis=1), jnp.stack(st_out, axis=1))
```

```python
import functools
import math

import jax
import jax.numpy as jnp
from jax import lax
from jax.experimental import pallas as pl
from jax.experimental.pallas import tpu as pltpu

F32 = jnp.float32
BF16 = jnp.bfloat16

D_MODEL = 1024
DEPTH = 2
GRID_W = 64
EPS = 1e-6
HEADS = 8
Q_LORA = 384
KV_LORA = 256
QK_NOPE = 64
QK_ROPE = 32
V_HEAD = 64
MLA_W = HEADS * V_HEAD
ROPE_THETA = 10000.0
CONV_W = 512
GDN_DK = 64
GDN_DV = 64
GDN_W = HEADS * GDN_DK
GDN_CHUNK = 64
LANES = 128
HEAD_PAD = 128
QK_W = HEADS * HEAD_PAD
SM_SCALE = (QK_NOPE + QK_ROPE) ** -0.5
COND_ROWS = 16
VMEM_LIMIT = 56 * 1024 * 1024

_O_CQ, _O_CKV, _O_KPE, _O_GA = 0, 384, 640, 672
_O_B, _O_C, _O_X, _O_GB = 1184, 1696, 2208, 2720
_O_Q, _O_K, _O_V, _O_Z = 3232, 3744, 4256, 4768
_O_AB, _O_MG, _O_END = 5280, 5312, 8384
W_MLA = Q_LORA + KV_LORA + LANES
W_QKV = 3 * GDN_W
W_CONV = 4 * CONV_W
W_GZ = MLA_W + GDN_W
W_MG = 3 * D_MODEL
GROUP_WIDTHS = (W_MLA, W_QKV, W_CONV, W_GZ, W_MG)
W_IN_PACKED = sum(GROUP_WIDTHS)
GATE_LANE = 64


def _sigmoid(x):
    return 1.0 / (1.0 + jnp.exp(-x))


def _silu(x):
    return x * _sigmoid(x)


def _rms(x):
    return x * lax.rsqrt(jnp.mean(x * x, axis=-1, keepdims=True) + EPS)


def _dot(a, b):
    return jnp.dot(a, b, preferred_element_type=F32)


def _dot_nt(a, b):
    return lax.dot_general(a, b, (((1,), (1,)), ((), ())), preferred_element_type=F32)


def _split3(x):
    hi = x.astype(BF16)
    r = x - hi.astype(F32)
    mid = r.astype(BF16)
    lo = (r - mid.astype(F32)).astype(BF16)
    return hi, mid, lo


def _head_sums(x2, ones_bd):
    hi = x2.astype(BF16)
    lo = (x2 - hi.astype(F32)).astype(BF16)
    return _dot(hi, ones_bd) + _dot(lo, ones_bd)


def _cparams(sem):
    return pltpu.CompilerParams(dimension_semantics=sem, vmem_limit_bytes=VMEM_LIMIT)


def _const_spec(shape):
    nd = len(shape)
    return pl.BlockSpec(shape, lambda *_: (0,) * nd)


def _ada_kernel(c_ref, w_ref, b_ref, o_ref):
    sc = _silu(c_ref[...])
    o_ref[...] = jnp.dot(sc, w_ref[...], preferred_element_type=F32, precision=lax.Precision.HIGHEST) + b_ref[...]


def _ada_mod(cond, w_ada, b_ada):
    depth = w_ada.shape[0]
    return pl.pallas_call(
        _ada_kernel,
        out_shape=jax.ShapeDtypeStruct((depth, COND_ROWS, 3 * D_MODEL), F32),
        grid=(depth, 3),
        in_specs=[
            pl.BlockSpec((COND_ROWS, D_MODEL), lambda l, j: (0, 0)),
            pl.BlockSpec((None, D_MODEL, D_MODEL), lambda l, j: (l, 0, j)),
            pl.BlockSpec((None, 1, D_MODEL), lambda l, j: (l, 0, j)),
        ],
        out_specs=pl.BlockSpec((None, COND_ROWS, D_MODEL), lambda l, j: (l, 0, j)),
        compiler_params=_cparams(("arbitrary", "arbitrary")),
        name="ada_mod",
    )(cond, w_ada, b_ada.reshape(depth, 1, 3 * D_MODEL))


IN_TM = 256
IN_CHUNK = 768


def _inproj_kernel(x_ref, mod_ref, g_ref, w_ref, *rest):
    outs, h_scr = rest[:-1], rest[-1]
    x = x_ref[...]
    shift = mod_ref[:, 0:D_MODEL]
    scale = mod_ref[:, D_MODEL:2 * D_MODEL]
    h = (_rms(x) * g_ref[...]) * (1.0 + scale) + shift
    h_scr[...] = h.astype(BF16)
    off = 0
    for o_ref in outs:
        n = o_ref.shape[1]
        for c0 in range(0, n, IN_CHUNK):
            c1 = min(c0 + IN_CHUNK, n)
            o_ref[:, c0:c1] = _dot(h_scr[...], w_ref[:, off + c0:off + c1])
        off += n


def _inproj(x2d, mod_l, norm_g, w_in_p, seq, cond_row0, cond_per_seq):
    t = x2d.shape[0]
    tps = seq // IN_TM
    mod3 = mod_l.reshape(COND_ROWS, 1, 3 * D_MODEL)
    return pl.pallas_call(
        _inproj_kernel,
        out_shape=[jax.ShapeDtypeStruct((t, n), F32) for n in GROUP_WIDTHS],
        grid=(t // IN_TM,),
        in_specs=[
            pl.BlockSpec((IN_TM, D_MODEL), lambda i: (i, 0)),
            pl.BlockSpec((None, 1, 3 * D_MODEL), lambda i: (cond_row0 + (i // tps) * cond_per_seq, 0, 0)),
            _const_spec((1, D_MODEL)),
            pl.BlockSpec((D_MODEL, W_IN_PACKED), lambda i: (0, 0), pipeline_mode=pl.Buffered(1)),
        ],
        out_specs=[pl.BlockSpec((IN_TM, n), lambda i: (i, 0)) for n in GROUP_WIDTHS],
        scratch_shapes=[pltpu.VMEM((IN_TM, D_MODEL), BF16)],
        compiler_params=_cparams(("parallel",)),
        name="inproj",
    )(x2d, mod3, norm_g.reshape(1, D_MODEL), w_in_p)


PREP_TM = 256


def _shift_rows(x, prev_row, next_row):
    n = x.shape[0]
    row = lax.broadcasted_iota(jnp.int32, (n, 1), 0)
    down = jnp.where(row == 0, prev_row, pltpu.roll(x, 1, 0))
    up = jnp.where(row == n - 1, next_row, pltpu.roll(x, n - 1, 0))
    return down, up


def _prep_kernel(*refs, use_rope, tiles_per_seq, emit_ckvn):
    it = iter(refs)
    mla_ref, qkv_ref, qkv_prev_ref, qkv_next_ref = next(it), next(it), next(it), next(it)
    if use_rope:
        m1_ref, m2_ref, m3_ref = next(it), next(it), next(it)
    gq_ref, gkv_ref, wa_ref = next(it), next(it), next(it)
    if use_rope:
        wb_ref = next(it)
    wke_ref, wv_ref, cw_ref, alog_ref, dtb_ref, ones_ref = (next(it) for _ in range(6))
    qf_ref, kf_ref, vf_ref = next(it), next(it), next(it)
    if emit_ckvn:
        ckvn_ref = next(it)
    qg_ref, kg_ref, vg_ref, gb_ref = next(it), next(it), next(it), next(it)

    i = pl.program_id(0)
    first = (i % tiles_per_seq) == 0
    last = (i % tiles_per_seq) == tiles_per_seq - 1

    cq = mla_ref[:, 0:Q_LORA]
    ckv = mla_ref[:, Q_LORA:Q_LORA + KV_LORA]
    small = mla_ref[:, Q_LORA + KV_LORA:W_MLA]
    qn = (_rms(cq) * gq_ref[...]).astype(BF16)
    qa = _dot(qn, wa_ref[...])
    if use_rope:
        qb = _dot(qn, wb_ref[...])
        m1 = m1_ref[...]
        m2 = m2_ref[...]
        for h in range(HEADS):
            sl = slice(h * HEAD_PAD, (h + 1) * HEAD_PAD)
            qf_ref[:, sl] = (qa[:, sl] * m1 + qb[:, sl] * m2).astype(BF16)
        kpe_t = small * m3_ref[...]
    else:
        qf_ref[...] = (qa * SM_SCALE).astype(BF16)
        kpe_t = small
    ckvn = _rms(ckv) * gkv_ref[...]
    if emit_ckvn:
        ckvn_ref[...] = ckvn
    ckvn_b = ckvn.astype(BF16)
    kin = jnp.concatenate([ckvn_b, kpe_t.astype(BF16)], axis=1)
    kf_ref[...] = _dot(kin, wke_ref[...]).astype(BF16)
    vf_ref[...] = _dot(ckvn_b, wv_ref[...]).astype(BF16)

    z1 = small + dtb_ref[...]
    sp = jnp.maximum(z1, 0.0) + jnp.log1p(jnp.exp(-jnp.abs(z1)))
    g = -jnp.exp(alog_ref[...]) * sp
    lane = lax.broadcasted_iota(jnp.int32, (1, LANES), 1)
    gb_ref[...] = jnp.where(lane < GATE_LANE + 2 * HEADS, g, _sigmoid(small))

    ones_bd = ones_ref[...]
    for part, o_ref in enumerate((qg_ref, kg_ref, vg_ref)):
        sl = slice(part * GDN_W, (part + 1) * GDN_W)
        x = qkv_ref[:, sl]
        prev_row = jnp.where(first, 0.0, qkv_prev_ref[7:8, sl])
        next_row = jnp.where(last, 0.0, qkv_next_ref[0:1, sl])
        down, up = _shift_rows(x, prev_row, next_row)
        y = down * cw_ref[0:1, sl] + x * cw_ref[1:2, sl] + up * cw_ref[2:3, sl]
        y = _silu(y)
        if part == 0:
            y = y * lax.rsqrt(_head_sums(y * y, ones_bd) + EPS) * (GDN_DK ** -0.5)
        elif part == 1:
            y = y * lax.rsqrt(_head_sums(y * y, ones_bd) + EPS)
        o_ref[...] = y


def _halo_specs(tm, width, n_rows, col_block=0):
    r = tm // 8
    nb = n_rows // 8
    prev = pl.BlockSpec((8, width), lambda i: (jnp.maximum(i * r - 1, 0), col_block))
    nxt = pl.BlockSpec((8, width), lambda i: (jnp.minimum((i + 1) * r, nb - 1), col_block))
    return prev, nxt


def _prep(u_mla, u_qkv, lw, seq, rope_tabs, emit_ckvn):
    t = u_mla.shape[0]
    tm = PREP_TM
    use_rope = rope_tabs is not None
    tps = seq // tm
    row = lambda w: pl.BlockSpec((tm, w), lambda i: (i, 0))
    prev_spec, next_spec = _halo_specs(tm, W_QKV, t)
    in_specs = [row(W_MLA), row(W_QKV), prev_spec, next_spec]
    args = [u_mla, u_qkv, u_qkv, u_qkv]
    if use_rope:
        in_specs += [pl.BlockSpec((tm, LANES), lambda i: (i % tps, 0))] * 3
        args += list(rope_tabs)
    in_specs += [_const_spec((1, Q_LORA)), _const_spec((1, KV_LORA)), _const_spec((Q_LORA, QK_W))]
    args += [lw["q_norm_g"], lw["kv_norm_g"], lw["w_qa"]]
    if use_rope:
        in_specs.append(_const_spec((Q_LORA, QK_W)))
        args.append(lw["w_qb"])
    in_specs += [_const_spec((KV_LORA + LANES, QK_W)), _const_spec((KV_LORA, MLA_W)), _const_spec((3, W_QKV)),
                 _const_spec((1, LANES)), _const_spec((1, LANES)), _const_spec((GDN_W, GDN_W))]
    args += [lw["w_ke_rope"] if use_rope else lw["w_ke"], lw["w_v"], lw["conv_qkv_w"], lw["alog_lane"], lw["dtb_lane"],
             lw["ones_bd"]]
    out_shape = [jax.ShapeDtypeStruct((t, QK_W), BF16), jax.ShapeDtypeStruct((t, QK_W), BF16),
                 jax.ShapeDtypeStruct((t, MLA_W), BF16)]
    out_specs = [row(QK_W), row(QK_W), row(MLA_W)]
    if emit_ckvn:
        out_shape.append(jax.ShapeDtypeStruct((t, KV_LORA), F32))
        out_specs.append(row(KV_LORA))
    out_shape += [jax.ShapeDtypeStruct((t, GDN_W), F32)] * 3 + [jax.ShapeDtypeStruct((t, LANES), F32)]
    out_specs += [row(GDN_W)] * 3 + [row(LANES)]
    outs = pl.pallas_call(
        functools.partial(_prep_kernel, use_rope=use_rope, tiles_per_seq=tps, emit_ckvn=emit_ckvn),
        out_shape=out_shape,
        grid=(t // tm,),
        in_specs=in_specs,
        out_specs=out_specs,
        compiler_params=_cparams(("parallel",)),
        name="prep",
    )(*args)
    return outs


def _ctxkv_kernel(ckv_ref, kpe_ref, wke_ref, wv_ref, kf_ref, vf_ref):
    ckv_b = ckv_ref[...].astype(BF16)
    kin = jnp.concatenate([ckv_b, kpe_ref[...].astype(BF16)], axis=1)
    kf_ref[...] = _dot(kin, wke_ref[...]).astype(BF16)
    vf_ref[...] = _dot(ckv_b, wv_ref[...]).astype(BF16)


def _ctx_kv(ckv2d, kpe_lanes, lw):
    t = ckv2d.shape[0]
    tm = 256
    row = lambda w: pl.BlockSpec((tm, w), lambda i: (i, 0))
    return pl.pallas_call(
        _ctxkv_kernel,
        out_shape=[jax.ShapeDtypeStruct((t, QK_W), BF16), jax.ShapeDtypeStruct((t, MLA_W), BF16)],
        grid=(t // tm,),
        in_specs=[row(KV_LORA), row(LANES), _const_spec((KV_LORA + LANES, QK_W)), _const_spec((KV_LORA, MLA_W))],
        out_specs=[row(QK_W), row(MLA_W)],
        compiler_params=_cparams(("parallel",)),
        name="ctx_kv",
    )(ckv2d, kpe_lanes, lw["w_ke"], lw["w_v"])


def _attn_kernel(*refs, n_parts):
    q_ref = refs[0]
    k_refs = refs[1:1 + n_parts]
    v_refs = refs[1 + n_parts:1 + 2 * n_parts]
    o_ref = refs[1 + 2 * n_parts]
    tq = q_ref.shape[0]
    lane = lax.broadcasted_iota(jnp.int32, (1, 2 * V_HEAD), 1)
    out = None
    for j in range(2):
        sl = slice(j * HEAD_PAD, (j + 1) * HEAD_PAD)
        q = q_ref[:, sl]
        s = [_dot_nt(q, k_ref[:, sl]) for k_ref in k_refs]
        m = s[0].max(axis=-1, keepdims=True)
        for sp in s[1:]:
            m = jnp.maximum(m, sp.max(axis=-1, keepdims=True))
        l = jnp.zeros((tq, 1), F32)
        acc = jnp.zeros((tq, 2 * V_HEAD), F32)
        for sp, v_ref in zip(s, v_refs):
            p = jnp.exp(sp - m)
            l = l + p.sum(axis=-1, keepdims=True)
            acc = acc + _dot(p.astype(BF16), v_ref[...])
        oj = acc / l
        out = oj if out is None else jnp.where(lane < V_HEAD, out, oj)
    o_ref[...] = out


def _attention(qf, kv_parts, n_batch, seq_q, tq):
    n_parts = len(kv_parts)
    nq = seq_q // tq
    in_specs = [pl.BlockSpec((tq, 2 * HEAD_PAD), lambda b, hp, qi: (b * nq + qi, hp))]
    in_specs += [pl.BlockSpec((sk, 2 * HEAD_PAD), lambda b, hp, qi: (b, hp)) for _, _, sk in kv_parts]
    in_specs += [pl.BlockSpec((sk, 2 * V_HEAD), lambda b, hp, qi: (b, hp)) for _, _, sk in kv_parts]
    return pl.pallas_call(
        functools.partial(_attn_kernel, n_parts=n_parts),
        out_shape=jax.ShapeDtypeStruct((n_batch * seq_q, MLA_W), F32),
        grid=(n_batch, HEADS // 2, nq),
        in_specs=in_specs,
        out_specs=pl.BlockSpec((tq, 2 * V_HEAD), lambda b, hp, qi: (b * nq + qi, hp)),
        compiler_params=_cparams(("parallel", "parallel", "arbitrary")),
        name="attention",
    )(qf, *[k for k, _, _ in kv_parts], *[v for _, v, _ in kv_parts])


PAIRS = HEADS // 2
PAIR_N = 2 * GDN_CHUNK


def _unit_tri_inverse(a_list, eye, blk16, blk32):
    diag = [jnp.where(blk16, a, 0.0) for a in a_list]
    powers = [[x.astype(BF16) for x in diag]]
    for _ in range(3):
        powers.append([_dot(x, x).astype(BF16) for x in powers[-1]])
    t = [eye - x for x in diag]
    for pw in powers[1:]:
        t = [x + _dot(x.astype(BF16), y) for x, y in zip(t, pw)]
    for off in ([jnp.where(blk32 & ~blk16, a, 0.0).astype(BF16) for a in a_list],
                [jnp.where(blk32, 0.0, a).astype(BF16) for a in a_list]):
        tb = [x.astype(BF16) for x in t]
        m = [_dot(x, y).astype(BF16) for x, y in zip(tb, off)]
        t = [x - _dot(y, z) for x, y, z in zip(t, m, tb)]
    return t


def _stack_pair(x, lane_lo):
    return jnp.concatenate([jnp.where(lane_lo, x, 0.0), jnp.where(lane_lo, 0.0, x)], axis=0)


def _pair_col(m, l0):
    return jnp.concatenate([m[:, l0:l0 + 1], m[:, l0 + 1:l0 + 2]], axis=0)


def _gdn_kernel(*refs, zero_init, n_chunks):
    it = iter(refs)
    dir_in = [tuple(next(it) for _ in range(4)) for _ in range(2)]
    h0_ref = None if zero_init else next(it)
    o_refs = (next(it), next(it))
    hfin_ref = next(it)
    s_scr = next(it)
    c = pl.program_id(1)
    cs, n = GDN_CHUNK, PAIR_N

    @pl.when(c == 0)
    def _():
        if zero_init:
            s_scr[...] = jnp.zeros_like(s_scr)
        else:
            z = jnp.zeros((GDN_DK, GDN_DV), F32)
            for d in range(2):
                for p in range(PAIRS):
                    top = jnp.concatenate([h0_ref[d, 2 * p], z], axis=1)
                    bot = jnp.concatenate([z, h0_ref[d, 2 * p + 1]], axis=1)
                    s_scr[d, p] = jnp.concatenate([top, bot], axis=0)

    ri = lax.broadcasted_iota(jnp.int32, (n, n), 0)
    ci = lax.broadcasted_iota(jnp.int32, (n, n), 1)
    same_head = (ri // cs) == (ci // cs)
    eye = (ri == ci).astype(F32)
    blk16 = (ri // 16) == (ci // 16)
    blk32 = (ri // 32) == (ci // 32)
    lane_lo = lax.broadcasted_iota(jnp.int32, (1, n), 1) < cs
    ti = lax.broadcasted_iota(jnp.int32, (cs, cs), 0)
    tj = lax.broadcasted_iota(jnp.int32, (cs, cs), 1)

    chains = [(d, p) for d in range(2) for p in range(PAIRS)]
    incl_d = [same_head & (ri >= ci), same_head & (ri <= ci)]
    strict_d = [same_head & (ri > ci), same_head & (ri < ci)]
    gates_d, gc_d, gct_d, glast_d = [], [], [], []
    for d in range(2):
        gates = dir_in[d][3][...]
        csum = ((ti >= tj) if d == 0 else (ti <= tj)).astype(BF16)
        ghi, gmid, glo = _split3(gates)
        gc_all = _dot(csum, ghi) + _dot(csum, gmid) + _dot(csum, glo)
        gates_d.append(gates)
        gc_d.append(gc_all)
        gct_d.append(gc_all.T)
        glast_d.append(gc_all[cs - 1:cs, :] if d == 0 else gc_all[0:1, :])

    kst, qst, vst, gcol, bcol, glast, decay, egc = [], [], [], [], [], [], [], []
    for d, p in chains:
        q_ref, k_ref, v_ref, _ = dir_in[d]
        sl = slice(p * n, (p + 1) * n)
        lg = GATE_LANE + HEADS * d + 2 * p
        kst.append(_stack_pair(k_ref[:, sl], lane_lo))
        qst.append(_stack_pair(q_ref[:, sl], lane_lo))
        vst.append(_stack_pair(v_ref[:, sl], lane_lo))
        gc = _pair_col(gc_d[d], lg)
        gr = jnp.concatenate([gct_d[d][lg:lg + 1, :], gct_d[d][lg + 1:lg + 2, :]], axis=1)
        gcol.append(gc)
        bcol.append(_pair_col(gates_d[d], lg + 2 * HEADS))
        glast.append(jnp.concatenate([jnp.broadcast_to(glast_d[d][:, lg:lg + 1], (cs, 1)),
                                      jnp.broadcast_to(glast_d[d][:, lg + 1:lg + 2], (cs, 1))], axis=0))
        decay.append(jnp.exp(jnp.where(incl_d[d], gc - gr, -jnp.inf)))
        egc.append(jnp.exp(gc))
    nch = len(chains)
    kstb = [x.astype(BF16) for x in kst]
    gp = [_dot_nt(jnp.concatenate([kstb[i], qst[i].astype(BF16)], axis=0), kstb[i]) for i in range(nch)]
    a_mat = [jnp.where(strict_d[chains[i][0]], bcol[i] * gp[i][:n] * decay[i], 0.0) for i in range(nch)]
    attn = [jnp.where(incl_d[chains[i][0]], gp[i][n:] * decay[i], 0.0).astype(BF16) for i in range(nch)]
    rhs = [jnp.concatenate([vst[i] * bcol[i], kst[i] * (bcol[i] * egc[i])], axis=1).astype(BF16) for i in range(nch)]
    t_inv = _unit_tri_inverse(a_mat, eye, blk16, blk32)
    x = [_dot(t_inv[i].astype(BF16), rhs[i]) for i in range(nch)]
    s_old = [s_scr[d, p] for d, p in chains]
    sb = [s.astype(BF16) for s in s_old]
    ws = [_dot(jnp.concatenate([x[i][:, n:], qst[i] * egc[i]], axis=0).astype(BF16), sb[i]) for i in range(nch)]
    v_new = [(x[i][:, :n] - ws[i][:n]).astype(BF16) for i in range(nch)]
    o_st = [ws[i][n:] + _dot(attn[i], v_new[i]) for i in range(nch)]
    k_dec_t = [(kst[i] * jnp.exp(glast[i] - gcol[i])).T.astype(BF16) for i in range(nch)]
    for i, (d, p) in enumerate(chains):
        s_scr[d, p] = s_old[i] * jnp.exp(glast[i]) + _dot(k_dec_t[i], v_new[i])
    for d in range(2):
        o_refs[d][...] = jnp.concatenate(
            [o_st[d * PAIRS + p][:cs] + o_st[d * PAIRS + p][cs:] for p in range(PAIRS)], axis=1)

    @pl.when(c == n_chunks - 1)
    def _():
        for d in range(2):
            for p in range(PAIRS):
                s = s_scr[d, p]
                hfin_ref[d, 2 * p] = s[:GDN_DK, :GDN_DV]
                hfin_ref[d, 2 * p + 1] = s[GDN_DK:, GDN_DV:]


def _gdn(qg, kg, vg, gb, h0, n_batch, seq):
    nc = seq // GDN_CHUNK
    zero_init = h0 is None
    fwd = lambda w: pl.BlockSpec((GDN_CHUNK, w), lambda b, c: (b * nc + c, 0))
    bwd = lambda w: pl.BlockSpec((GDN_CHUNK, w), lambda b, c: (b * nc + nc - 1 - c, 0))
    in_specs = [fwd(GDN_W), fwd(GDN_W), fwd(GDN_W), fwd(LANES), bwd(GDN_W), bwd(GDN_W), bwd(GDN_W), bwd(LANES)]
    args = [qg, kg, vg, gb, qg, kg, vg, gb]
    st_spec = pl.BlockSpec((None, 2, HEADS, GDN_DK, GDN_DV), lambda b, c: (b, 0, 0, 0, 0))
    if not zero_init:
        in_specs.append(st_spec)
        args.append(h0)
    t = n_batch * seq
    return pl.pallas_call(
        functools.partial(_gdn_kernel, zero_init=zero_init, n_chunks=nc),
        out_shape=[jax.ShapeDtypeStruct((t, GDN_W), F32), jax.ShapeDtypeStruct((t, GDN_W), F32),
                   jax.ShapeDtypeStruct((n_batch, 2, HEADS, GDN_DK, GDN_DV), F32)],
        grid=(n_batch, nc),
        in_specs=in_specs,
        out_specs=[fwd(GDN_W), bwd(GDN_W), st_spec],
        scratch_shapes=[pltpu.VMEM((2, PAIRS, PAIR_N, PAIR_N), F32)],
        compiler_params=_cparams(("parallel", "arbitrary")),
        name="gdn",
    )(*args)


MERGE_TM = 256


def _merge_kernel(x_ref, mod_ref, oa_ref, gz_ref, conv_ref, cx_prev_ref, cx_next_ref, of_ref, ob_ref, mg_ref,
                  wpa_ref, wpb_ref, wpc_ref, wo_ref, cw_ref, gn_ref, ones_ref, fin_ref, o_ref, *,
                  tiles_per_seq, final_norm):
    i = pl.program_id(0)
    first = (i % tiles_per_seq) == 0
    last = (i % tiles_per_seq) == tiles_per_seq - 1
    o_a = oa_ref[...] * _silu(gz_ref[:, 0:MLA_W])
    pa = _dot(o_a.astype(BF16), wpa_ref[...])
    cx = conv_ref[:, 0:CONV_W] * conv_ref[:, CONV_W:2 * CONV_W]
    prev_row = jnp.where(first, 0.0, cx_prev_ref[7:8, 0:CONV_W] * cx_prev_ref[7:8, CONV_W:2 * CONV_W])
    next_row = jnp.where(last, 0.0, cx_next_ref[0:1, 0:CONV_W] * cx_next_ref[0:1, CONV_W:2 * CONV_W])
    down, up = _shift_rows(cx, prev_row, next_row)
    conv = down * cw_ref[0:1, :] + cx * cw_ref[1:2, :] + up * cw_ref[2:3, :]
    o_b = conv_ref[:, 2 * CONV_W:3 * CONV_W] * conv * _silu(conv_ref[:, 3 * CONV_W:4 * CONV_W])
    pb = _dot(o_b.astype(BF16), wpb_ref[...])
    og = of_ref[...] + ob_ref[...]
    ms = _head_sums(og * og, ones_ref[...]) * (1.0 / GDN_DV)
    o_c = og * lax.rsqrt(ms + EPS) * gn_ref[...] * _silu(gz_ref[:, MLA_W:W_GZ])
    pc = _dot(o_c.astype(BF16), wpc_ref[...])
    m = (_sigmoid(mg_ref[:, 0:D_MODEL]) * pa + _sigmoid(mg_ref[:, D_MODEL:2 * D_MODEL]) * pb
         + _sigmoid(mg_ref[:, 2 * D_MODEL:3 * D_MODEL]) * pc)
    r = _dot(m.astype(BF16), wo_ref[...])
    y = x_ref[...] + mod_ref[:, 2 * D_MODEL:3 * D_MODEL] * r
    if final_norm:
        y = _rms(y) * fin_ref[...]
    o_ref[...] = y


def _merge(x2d, mod_l, o_a, u_gz, u_conv, o_f, o_b, u_mg, lw, final_g, seq, cond_row0, cond_per_seq, final_norm):
    t = x2d.shape[0]
    tm = MERGE_TM
    tps = seq // tm
    row = lambda w: pl.BlockSpec((tm, w), lambda i: (i, 0))
    prev_spec, next_spec = _halo_specs(tm, 2 * CONV_W, t)
    mod3 = mod_l.reshape(COND_ROWS, 1, 3 * D_MODEL)
    return pl.pallas_call(
        functools.partial(_merge_kernel, tiles_per_seq=tps, final_norm=final_norm),
        out_shape=jax.ShapeDtypeStruct((t, D_MODEL), F32),
        grid=(t // tm,),
        in_specs=[
            row(D_MODEL),
            pl.BlockSpec((None, 1, 3 * D_MODEL), lambda i: (cond_row0 + (i // tps) * cond_per_seq, 0, 0)),
            row(MLA_W), row(W_GZ), row(W_CONV), prev_spec, next_spec, row(GDN_W), row(GDN_W), row(W_MG),
            _const_spec((MLA_W, D_MODEL)), _const_spec((CONV_W, D_MODEL)), _const_spec((GDN_W, D_MODEL)),
            _const_spec((D_MODEL, D_MODEL)), _const_spec((3, CONV_W)), _const_spec((1, GDN_W)),
            _const_spec((GDN_W, GDN_W)), _const_spec((1, D_MODEL)),
        ],
        out_specs=row(D_MODEL),
        compiler_params=_cparams(("parallel",)),
        name="merge",
    )(x2d, mod3, o_a, u_gz, u_conv, u_conv, u_conv, o_f, o_b, u_mg, lw["w_pa"], lw["w_pb"], lw["w_pc"], lw["w_o"],
      lw["conv_b_w"], lw["gdn_norm_lane"], lw["ones_bd"], final_g.reshape(1, D_MODEL))


_EVEN_ODD = list(range(0, QK_ROPE, 2)) + list(range(1, QK_ROPE, 2))
_SWAPPED = list(range(1, QK_ROPE, 2)) + list(range(0, QK_ROPE, 2))


def _pack_layer_weights(l, w_in, q_norm_g, kv_norm_g, w_uq, w_ukv, conv_b_w, conv_qkv_w, a_log, dt_bias, gdn_norm_g,
                        w_pa, w_pb, w_pc, w_o):
    wi = w_in[l]
    eo = jnp.array(_EVEN_ODD)
    sw = jnp.array(_SWAPPED)
    w_kpe = wi[:, _O_KPE:_O_GA]
    small = jnp.concatenate([w_kpe[:, eo], w_kpe[:, sw], wi[:, _O_AB:_O_MG], w_kpe], axis=1)
    w_in_p = jnp.concatenate([
        wi[:, _O_CQ:_O_KPE], small,
        wi[:, _O_Q:_O_Z],
        wi[:, _O_C:_O_GB], wi[:, _O_B:_O_C], wi[:, _O_GB:_O_Q],
        wi[:, _O_GA:_O_B], wi[:, _O_Z:_O_AB],
        wi[:, _O_MG:_O_END]], axis=1).astype(BF16)
    wq = w_uq[l].reshape(Q_LORA, HEADS, QK_NOPE + QK_ROPE)
    nope, rope = wq[:, :, :QK_NOPE], wq[:, :, QK_NOPE:]
    z32 = jnp.zeros((Q_LORA, HEADS, HEAD_PAD - QK_NOPE - QK_ROPE), F32)
    z64 = jnp.zeros((Q_LORA, HEADS, QK_NOPE), F32)
    w_qa = jnp.concatenate([nope, rope[:, :, eo], z32], axis=-1).reshape(Q_LORA, QK_W).astype(BF16)
    w_qb = jnp.concatenate([z64, rope[:, :, sw], z32], axis=-1).reshape(Q_LORA, QK_W).astype(BF16)
    wkv = w_ukv[l].reshape(KV_LORA, HEADS, QK_NOPE + V_HEAD)
    w_k = jnp.concatenate([wkv[:, :, :QK_NOPE], jnp.zeros((KV_LORA, HEADS, HEAD_PAD - QK_NOPE), F32)],
                          axis=-1).reshape(KV_LORA, QK_W)
    w_v = wkv[:, :, QK_NOPE:].reshape(KV_LORA, MLA_W).astype(BF16)
    src = jnp.arange(LANES)[:, None]
    dst = jnp.arange(QK_W)[None, :] % HEAD_PAD - QK_NOPE
    in_rope = (dst >= 0) & (dst < QK_ROPE)
    e_plain = (in_rope & (src == dst)).astype(F32)
    e_rope = (in_rope & ((src == dst) | (src == dst + QK_ROPE))).astype(F32)
    lane = jnp.arange(LANES)
    gate_idx = jnp.clip(lane - GATE_LANE, 0, 2 * HEADS - 1)
    is_g = (lane >= GATE_LANE) & (lane < GATE_LANE + 2 * HEADS)
    alog_lane = jnp.where(is_g, a_log[l].reshape(-1)[gate_idx], 0.0).reshape(1, LANES)
    dtb_lane = jnp.where(is_g, dt_bias[l].reshape(-1)[gate_idx], 0.0).reshape(1, LANES)
    hid = jnp.arange(GDN_W) // GDN_DK
    return {
        "w_in_p": w_in_p,
        "q_norm_g": q_norm_g[l].reshape(1, Q_LORA), "kv_norm_g": kv_norm_g[l].reshape(1, KV_LORA),
        "w_qa": w_qa, "w_qb": w_qb,
        "w_ke": jnp.concatenate([w_k, e_plain], axis=0).astype(BF16),
        "w_ke_rope": jnp.concatenate([w_k, e_rope], axis=0).astype(BF16),
        "w_v": w_v,
        "conv_qkv_w": conv_qkv_w[l], "conv_b_w": conv_b_w[l],
        "alog_lane": alog_lane, "dtb_lane": dtb_lane,
        "ones_bd": (hid[:, None] == hid[None, :]).astype(BF16),
        "gdn_norm_lane": jnp.tile(gdn_norm_g[l], HEADS).reshape(1, GDN_W),
        "w_pa": w_pa[l].astype(BF16), "w_pb": w_pb[l].astype(BF16), "w_pc": w_pc[l].astype(BF16),
        "w_o": w_o[l].astype(BF16),
    }


def _rope_tables(n_tokens):
    t = jnp.arange(n_tokens)
    row = (t // GRID_W).astype(F32)
    col = (t % GRID_W).astype(F32)
    n_freq = QK_ROPE // 4
    inv_freq = ROPE_THETA ** (-jnp.arange(n_freq, dtype=F32) / n_freq)
    ang = jnp.concatenate([row[:, None] * inv_freq, col[:, None] * inv_freq], axis=-1)
    cos, sin = jnp.cos(ang), jnp.sin(ang)
    cc = jnp.concatenate([cos, cos], axis=-1)
    ss = jnp.concatenate([-sin, sin], axis=-1)
    one = jnp.ones((n_tokens, QK_NOPE), F32)
    z32 = jnp.zeros((n_tokens, QK_ROPE), F32)
    z64 = jnp.zeros((n_tokens, QK_NOPE), F32)
    m1 = jnp.concatenate([one, cc, z32], axis=-1) * SM_SCALE
    m2 = jnp.concatenate([z64, ss, z32], axis=-1) * SM_SCALE
    m3 = jnp.concatenate([cc, ss, z64], axis=-1)
    return m1, m2, m3


def _layer(x2d, mod_l, lw, final_g, n_batch, seq, cond_row0, cond_per_seq, final_norm, rope_tabs=None,
           ctx_kv=None, h0=None, tq=256):
    is_ctx = ctx_kv is None
    u_mla, u_qkv, u_conv, u_gz, u_mg = _inproj(x2d, mod_l, lw["norm_g"], lw["w_in_p"], seq, cond_row0, cond_per_seq)
    outs = _prep(u_mla, u_qkv, lw, seq, rope_tabs, emit_ckvn=is_ctx)
    if is_ctx:
        qf, kf, vf, ckvn, qg, kg, vg, gb = outs
        kv_parts = [(kf, vf, seq)]
    else:
        qf, kf, vf, qg, kg, vg, gb = outs
        ckvn = None
        kv_parts = [ctx_kv, (kf, vf, seq)]
    o_a = _attention(qf, kv_parts, n_batch, seq, tq)
    o_f, o_b, h_fin = _gdn(qg, kg, vg, gb, h0, n_batch, seq)
    y = _merge(x2d, mod_l, o_a, u_gz, u_conv, o_f, o_b, u_mg, lw, final_g, seq, cond_row0, cond_per_seq, final_norm)
    return y, ckvn, u_mla[:, W_MLA - QK_ROPE:], h_fin


def kernel(x_prompt, x_sample, c, cache_ckv, cache_kpe, state_gdn, c_ctx, norm_g, w_ada, b_ada, w_in, q_norm_g,
           kv_norm_g, w_uq, w_ukv, conv_b_w, conv_qkv_w, a_log, dt_bias, gdn_norm_g, w_pa, w_pb, w_pc, w_o,
           final_norm_g):
    n_ctx, seq_ctx, _ = x_prompt.shape
    n_lat, seq_lat, _ = x_sample.shape
    past = cache_ckv.shape[2]
    depth = w_in.shape[0]
    assert n_lat + 1 <= COND_ROWS

    cond = jnp.zeros((COND_ROWS, D_MODEL), F32).at[:n_lat].set(c).at[n_lat].set(c_ctx)
    mod = _ada_mod(cond, w_ada, b_ada)
    rope_tabs = _rope_tables(seq_lat)
    eo = jnp.array(_EVEN_ODD)

    xp = x_prompt.reshape(n_ctx * seq_ctx, D_MODEL)
    xs = x_sample.reshape(n_lat * seq_lat, D_MODEL)
    ckv_out, kpe_out, st_out = [], [], []
    for l in range(depth):
        lw = _pack_layer_weights(l, w_in, q_norm_g, kv_norm_g, w_uq, w_ukv, conv_b_w, conv_qkv_w, a_log, dt_bias,
                                 gdn_norm_g, w_pa, w_pb, w_pc, w_o)
        lw["norm_g"] = norm_g[l]
        last = l == depth - 1
        xp, ckvn, kpe, h_fin = _layer(xp, mod[l], lw, final_norm_g, n_ctx, seq_ctx, n_lat, 0, last)
        ckv_out.append(ckvn.reshape(n_ctx, seq_ctx, KV_LORA))
        kpe_out.append(kpe.reshape(n_ctx, seq_ctx, QK_ROPE))
        st_out.append(h_fin)
        kpe_lanes = jnp.pad(cache_kpe[:, l][..., eo].reshape(n_lat * past, QK_ROPE), ((0, 0), (0, LANES - QK_ROPE)))
        kc, vc = _ctx_kv(cache_ckv[:, l].reshape(n_lat * past, KV_LORA), kpe_lanes, lw)
        xs, _, _, _ = _layer(xs, mod[l], lw, final_norm_g, n_lat, seq_lat, 0, 1, last, rope_tabs=rope_tabs,
                             ctx_kv=(kc, vc, past), h0=state_gdn[:, l], tq=512)
    return (xp.reshape(n_ctx, seq_ctx, D_MODEL), xs.reshape(n_lat, seq_lat, D_MODEL),
            jnp.stack(ckv_out, axis=1), jnp.stack(kpe_out, axis=1), jnp.stack(st_out, axis=1))
```

```python
import functools
import math

import jax
import jax.numpy as jnp
from jax import lax
from jax.experimental import pallas as pl
from jax.experimental.pallas import tpu as pltpu

F32 = jnp.float32
BF16 = jnp.bfloat16

D_MODEL = 1024
DEPTH = 2
GRID_W = 64
EPS = 1e-6
HEADS = 8
Q_LORA = 384
KV_LORA = 256
QK_NOPE = 64
QK_ROPE = 32
V_HEAD = 64
MLA_W = HEADS * V_HEAD
ROPE_THETA = 10000.0
CONV_W = 512
GDN_DK = 64
GDN_DV = 64
GDN_W = HEADS * GDN_DK
GDN_CHUNK = 64
LANES = 128
HEAD_PAD = 128
QK_W = HEADS * HEAD_PAD
SM_SCALE = (QK_NOPE + QK_ROPE) ** -0.5
COND_ROWS = 16
VMEM_LIMIT = 56 * 1024 * 1024

_O_CQ, _O_CKV, _O_KPE, _O_GA = 0, 384, 640, 672
_O_B, _O_C, _O_X, _O_GB = 1184, 1696, 2208, 2720
_O_Q, _O_K, _O_V, _O_Z = 3232, 3744, 4256, 4768
_O_AB, _O_MG, _O_END = 5280, 5312, 8384
W_MLA = Q_LORA + KV_LORA + LANES
W_QKV = 3 * GDN_W
W_CONV = 4 * CONV_W
W_GZ = MLA_W + GDN_W
W_MG = 3 * D_MODEL
GROUP_WIDTHS = (W_MLA, W_QKV, W_CONV, W_GZ, W_MG)
GROUP_DTYPES = (F32, BF16, BF16, BF16, BF16)
W_IN_PACKED = sum(GROUP_WIDTHS)
GATE_LANE = 64


def _sigmoid(x):
    return 1.0 / (1.0 + jnp.exp(-x))


def _silu(x):
    return x * _sigmoid(x)


def _rms(x):
    return x * lax.rsqrt(jnp.mean(x * x, axis=-1, keepdims=True) + EPS)


def _dot(a, b):
    return jnp.dot(a, b, preferred_element_type=F32)


def _dot_nt(a, b):
    return lax.dot_general(a, b, (((1,), (1,)), ((), ())), preferred_element_type=F32)


def _split3(x):
    hi = x.astype(BF16)
    r = x - hi.astype(F32)
    mid = r.astype(BF16)
    lo = (r - mid.astype(F32)).astype(BF16)
    return hi, mid, lo


def _head_sums(x2, ones_bd):
    hi = x2.astype(BF16)
    lo = (x2 - hi.astype(F32)).astype(BF16)
    return _dot(hi, ones_bd) + _dot(lo, ones_bd)


def _cparams(sem):
    return pltpu.CompilerParams(dimension_semantics=sem, vmem_limit_bytes=VMEM_LIMIT)


def _const_spec(shape):
    nd = len(shape)
    return pl.BlockSpec(shape, lambda *_: (0,) * nd)


def _ada_kernel(c_ref, w_ref, b_ref, o_ref):
    sc = _silu(c_ref[...])
    o_ref[...] = jnp.dot(sc, w_ref[...], preferred_element_type=F32, precision=lax.Precision.HIGHEST) + b_ref[...]


def _ada_mod(cond, w_ada, b_ada):
    depth = w_ada.shape[0]
    return pl.pallas_call(
        _ada_kernel,
        out_shape=jax.ShapeDtypeStruct((depth, COND_ROWS, 3 * D_MODEL), F32),
        grid=(depth, 3),
        in_specs=[
            pl.BlockSpec((COND_ROWS, D_MODEL), lambda l, j: (0, 0)),
            pl.BlockSpec((None, D_MODEL, D_MODEL), lambda l, j: (l, 0, j)),
            pl.BlockSpec((None, 1, D_MODEL), lambda l, j: (l, 0, j)),
        ],
        out_specs=pl.BlockSpec((None, COND_ROWS, D_MODEL), lambda l, j: (l, 0, j)),
        compiler_params=_cparams(("arbitrary", "arbitrary")),
        name="ada_mod",
    )(cond, w_ada, b_ada.reshape(depth, 1, 3 * D_MODEL))


IN_TM = 512
IN_CHUNK = 768


def _inproj_kernel(x_ref, mod_ref, g_ref, w_ref, *rest):
    outs, h_scr = rest[:-1], rest[-1]
    x = x_ref[...]
    shift = mod_ref[:, 0:D_MODEL]
    scale = mod_ref[:, D_MODEL:2 * D_MODEL]
    h = (_rms(x) * g_ref[...]) * (1.0 + scale) + shift
    h_scr[...] = h.astype(BF16)
    off = 0
    for o_ref in outs:
        n = o_ref.shape[1]
        for c0 in range(0, n, IN_CHUNK):
            c1 = min(c0 + IN_CHUNK, n)
            o_ref[:, c0:c1] = _dot(h_scr[...], w_ref[:, off + c0:off + c1]).astype(o_ref.dtype)
        off += n


def _inproj(x2d, mod_l, norm_g, w_in_p, seq, cond_row0, cond_per_seq):
    t = x2d.shape[0]
    assert t % IN_TM == 0 and (cond_per_seq == 0 or seq % IN_TM == 0)
    tps = max(seq // IN_TM, 1)
    mod3 = mod_l.reshape(COND_ROWS, 1, 3 * D_MODEL)
    return pl.pallas_call(
        _inproj_kernel,
        out_shape=[jax.ShapeDtypeStruct((t, n), dt) for n, dt in zip(GROUP_WIDTHS, GROUP_DTYPES)],
        grid=(t // IN_TM,),
        in_specs=[
            pl.BlockSpec((IN_TM, D_MODEL), lambda i: (i, 0)),
            pl.BlockSpec((None, 1, 3 * D_MODEL), lambda i: (cond_row0 + (i // tps) * cond_per_seq, 0, 0)),
            _const_spec((1, D_MODEL)),
            pl.BlockSpec((D_MODEL, W_IN_PACKED), lambda i: (0, 0), pipeline_mode=pl.Buffered(1)),
        ],
        out_specs=[pl.BlockSpec((IN_TM, n), lambda i: (i, 0)) for n in GROUP_WIDTHS],
        scratch_shapes=[pltpu.VMEM((IN_TM, D_MODEL), BF16)],
        compiler_params=_cparams(("parallel",)),
        name="inproj",
    )(x2d, mod3, norm_g.reshape(1, D_MODEL), w_in_p)


PREP_TM = 256


def _shift_rows(x, prev_row, next_row):
    n = x.shape[0]
    row = lax.broadcasted_iota(jnp.int32, (n, 1), 0)
    down = jnp.where(row == 0, prev_row, pltpu.roll(x, 1, 0))
    up = jnp.where(row == n - 1, next_row, pltpu.roll(x, n - 1, 0))
    return down, up


def _prep_kernel(*refs, use_rope, tiles_per_seq, emit_ckvn):
    it = iter(refs)
    mla_ref, qkv_ref, qkv_prev_ref, qkv_next_ref = next(it), next(it), next(it), next(it)
    if use_rope:
        m1_ref, m2_ref, m3_ref = next(it), next(it), next(it)
    gq_ref, gkv_ref, wa_ref = next(it), next(it), next(it)
    if use_rope:
        wb_ref = next(it)
    wke_ref, wv_ref, cw_ref, alog_ref, dtb_ref, ones_ref = (next(it) for _ in range(6))
    qf_ref, kf_ref, vf_ref = next(it), next(it), next(it)
    if emit_ckvn:
        ckvn_ref = next(it)
    qg_ref, kg_ref, vg_ref, gb_ref = next(it), next(it), next(it), next(it)

    i = pl.program_id(0)
    first = (i % tiles_per_seq) == 0
    last = (i % tiles_per_seq) == tiles_per_seq - 1

    cq = mla_ref[:, 0:Q_LORA]
    ckv = mla_ref[:, Q_LORA:Q_LORA + KV_LORA]
    small = mla_ref[:, Q_LORA + KV_LORA:W_MLA]
    qn = (_rms(cq) * gq_ref[...]).astype(BF16)
    qa = _dot(qn, wa_ref[...])
    if use_rope:
        qb = _dot(qn, wb_ref[...])
        m1 = m1_ref[...]
        m2 = m2_ref[...]
        for h in range(HEADS):
            sl = slice(h * HEAD_PAD, (h + 1) * HEAD_PAD)
            qf_ref[:, sl] = (qa[:, sl] * m1 + qb[:, sl] * m2).astype(BF16)
        kpe_t = small * m3_ref[...]
    else:
        qf_ref[...] = (qa * SM_SCALE).astype(BF16)
        kpe_t = small
    ckvn = _rms(ckv) * gkv_ref[...]
    if emit_ckvn:
        ckvn_ref[...] = ckvn
    ckvn_b = ckvn.astype(BF16)
    kin = jnp.concatenate([ckvn_b, kpe_t.astype(BF16)], axis=1)
    kf_ref[...] = _dot(kin, wke_ref[...]).astype(BF16)
    vf_ref[...] = _dot(ckvn_b, wv_ref[...]).astype(BF16)

    z1 = small + dtb_ref[...]
    sp = jnp.maximum(z1, 0.0) + jnp.log1p(jnp.exp(-jnp.abs(z1)))
    g = -jnp.exp(alog_ref[...]) * sp
    lane = lax.broadcasted_iota(jnp.int32, (1, LANES), 1)
    gb_ref[...] = jnp.where(lane < GATE_LANE + 2 * HEADS, g, _sigmoid(small))

    ones_bd = ones_ref[...]
    for part, o_ref in enumerate((qg_ref, kg_ref, vg_ref)):
        sl = slice(part * GDN_W, (part + 1) * GDN_W)
        x = qkv_ref[:, sl].astype(F32)
        prev_row = jnp.where(first, 0.0, qkv_prev_ref[HALO - 1:HALO, sl].astype(F32))
        next_row = jnp.where(last, 0.0, qkv_next_ref[0:1, sl].astype(F32))
        down, up = _shift_rows(x, prev_row, next_row)
        y = down * cw_ref[0:1, sl] + x * cw_ref[1:2, sl] + up * cw_ref[2:3, sl]
        y = _silu(y)
        if part == 0:
            y = y * lax.rsqrt(_head_sums(y * y, ones_bd) + EPS) * (GDN_DK ** -0.5)
        elif part == 1:
            y = y * lax.rsqrt(_head_sums(y * y, ones_bd) + EPS)
        o_ref[...] = y


HALO = 16


def _halo_specs(tm, width, n_rows, col_block=0):
    r = tm // HALO
    nb = n_rows // HALO
    prev = pl.BlockSpec((HALO, width), lambda i: (jnp.maximum(i * r - 1, 0), col_block))
    nxt = pl.BlockSpec((HALO, width), lambda i: (jnp.minimum((i + 1) * r, nb - 1), col_block))
    return prev, nxt


def _prep(u_mla, u_qkv, lw, seq, rope_tabs, emit_ckvn):
    t = u_mla.shape[0]
    tm = PREP_TM
    use_rope = rope_tabs is not None
    tps = seq // tm
    row = lambda w: pl.BlockSpec((tm, w), lambda i: (i, 0))
    prev_spec, next_spec = _halo_specs(tm, W_QKV, t)
    in_specs = [row(W_MLA), row(W_QKV), prev_spec, next_spec]
    args = [u_mla, u_qkv, u_qkv, u_qkv]
    if use_rope:
        in_specs += [pl.BlockSpec((tm, LANES), lambda i: (i % tps, 0))] * 3
        args += list(rope_tabs)
    in_specs += [_const_spec((1, Q_LORA)), _const_spec((1, KV_LORA)), _const_spec((Q_LORA, QK_W))]
    args += [lw["q_norm_g"], lw["kv_norm_g"], lw["w_qa"]]
    if use_rope:
        in_specs.append(_const_spec((Q_LORA, QK_W)))
        args.append(lw["w_qb"])
    in_specs += [_const_spec((KV_LORA + LANES, QK_W)), _const_spec((KV_LORA, MLA_W)), _const_spec((3, W_QKV)),
                 _const_spec((1, LANES)), _const_spec((1, LANES)), _const_spec((GDN_W, GDN_W))]
    args += [lw["w_ke_rope"] if use_rope else lw["w_ke"], lw["w_v"], lw["conv_qkv_w"], lw["alog_lane"], lw["dtb_lane"],
             lw["ones_bd"]]
    out_shape = [jax.ShapeDtypeStruct((t, QK_W), BF16), jax.ShapeDtypeStruct((t, QK_W), BF16),
                 jax.ShapeDtypeStruct((t, MLA_W), BF16)]
    out_specs = [row(QK_W), row(QK_W), row(MLA_W)]
    if emit_ckvn:
        out_shape.append(jax.ShapeDtypeStruct((t, KV_LORA), F32))
        out_specs.append(row(KV_LORA))
    out_shape += [jax.ShapeDtypeStruct((t, GDN_W), F32)] * 3 + [jax.ShapeDtypeStruct((t, LANES), F32)]
    out_specs += [row(GDN_W)] * 3 + [row(LANES)]
    outs = pl.pallas_call(
        functools.partial(_prep_kernel, use_rope=use_rope, tiles_per_seq=tps, emit_ckvn=emit_ckvn),
        out_shape=out_shape,
        grid=(t // tm,),
        in_specs=in_specs,
        out_specs=out_specs,
        compiler_params=_cparams(("parallel",)),
        name="prep",
    )(*args)
    return outs


def _ctxkv_kernel(ckv_ref, kpe_ref, wke_ref, wv_ref, kf_ref, vf_ref):
    ckv_b = ckv_ref[...].astype(BF16)
    kin = jnp.concatenate([ckv_b, kpe_ref[...].astype(BF16)], axis=1)
    kf_ref[...] = _dot(kin, wke_ref[...]).astype(BF16)
    vf_ref[...] = _dot(ckv_b, wv_ref[...]).astype(BF16)


def _ctx_kv(ckv2d, kpe_lanes, lw):
    t = ckv2d.shape[0]
    tm = 256
    row = lambda w: pl.BlockSpec((tm, w), lambda i: (i, 0))
    return pl.pallas_call(
        _ctxkv_kernel,
        out_shape=[jax.ShapeDtypeStruct((t, QK_W), BF16), jax.ShapeDtypeStruct((t, MLA_W), BF16)],
        grid=(t // tm,),
        in_specs=[row(KV_LORA), row(LANES), _const_spec((KV_LORA + LANES, QK_W)), _const_spec((KV_LORA, MLA_W))],
        out_specs=[row(QK_W), row(MLA_W)],
        compiler_params=_cparams(("parallel",)),
        name="ctx_kv",
    )(ckv2d, kpe_lanes, lw["w_ke"], lw["w_v"])


def _attn_kernel(*refs, n_parts):
    q_ref = refs[0]
    k_refs = refs[1:1 + n_parts]
    v_refs = refs[1 + n_parts:1 + 2 * n_parts]
    o_ref = refs[1 + 2 * n_parts]
    lane = lax.broadcasted_iota(jnp.int32, (1, 2 * V_HEAD), 1)
    own = [lane < V_HEAD, lane >= V_HEAD]
    heads = range(2)
    s = [[_dot_nt(q_ref[:, j * HEAD_PAD:(j + 1) * HEAD_PAD], k_ref[:, j * HEAD_PAD:(j + 1) * HEAD_PAD])
          for k_ref in k_refs] for j in heads]
    m = []
    for j in heads:
        mj = s[j][0].max(axis=-1, keepdims=True)
        for sp in s[j][1:]:
            mj = jnp.maximum(mj, sp.max(axis=-1, keepdims=True))
        m.append(mj)
    p = [[jnp.exp(sp - m[j]).astype(BF16) for sp in s[j]] for j in heads]
    acc = []
    for j in heads:
        a = None
        for pj, v_ref in zip(p[j], v_refs):
            t = _dot(pj, jnp.where(own[j], v_ref[...], 1.0).astype(BF16))
            a = t if a is None else a + t
        acc.append(a)
    l0 = acc[0][:, V_HEAD:V_HEAD + 1]
    l1 = acc[1][:, 0:1]
    o_ref[...] = jnp.where(own[0], acc[0] / l0, acc[1] / l1).astype(o_ref.dtype)


def _attention(qf, kv_parts, n_batch, seq_q, tq):
    n_parts = len(kv_parts)
    nq = seq_q // tq
    in_specs = [pl.BlockSpec((tq, 2 * HEAD_PAD), lambda b, hp, qi: (b * nq + qi, hp))]
    in_specs += [pl.BlockSpec((sk, 2 * HEAD_PAD), lambda b, hp, qi: (b, hp)) for _, _, sk in kv_parts]
    in_specs += [pl.BlockSpec((sk, 2 * V_HEAD), lambda b, hp, qi: (b, hp)) for _, _, sk in kv_parts]
    return pl.pallas_call(
        functools.partial(_attn_kernel, n_parts=n_parts),
        out_shape=jax.ShapeDtypeStruct((n_batch * seq_q, MLA_W), BF16),
        grid=(n_batch, HEADS // 2, nq),
        in_specs=in_specs,
        out_specs=pl.BlockSpec((tq, 2 * V_HEAD), lambda b, hp, qi: (b * nq + qi, hp)),
        compiler_params=_cparams(("parallel", "parallel", "arbitrary")),
        name="attention",
    )(qf, *[k for k, _, _ in kv_parts], *[v for _, v, _ in kv_parts])


PAIRS = HEADS // 2
PAIR_N = 2 * GDN_CHUNK


def _unit_tri_inverse(a_list, eye, blk16, blk32):
    diag = [jnp.where(blk16, a, 0.0) for a in a_list]
    powers = [[x.astype(BF16) for x in diag]]
    for _ in range(3):
        powers.append([_dot(x, x).astype(BF16) for x in powers[-1]])
    t = [eye - x for x in diag]
    for pw in powers[1:]:
        t = [x + _dot(x.astype(BF16), y) for x, y in zip(t, pw)]
    for off in ([jnp.where(blk32 & ~blk16, a, 0.0).astype(BF16) for a in a_list],
                [jnp.where(blk32, 0.0, a).astype(BF16) for a in a_list]):
        tb = [x.astype(BF16) for x in t]
        m = [_dot(x, y).astype(BF16) for x, y in zip(tb, off)]
        t = [x - _dot(y, z) for x, y, z in zip(t, m, tb)]
    return t


def _stack_pair(x, lane_lo):
    return jnp.concatenate([jnp.where(lane_lo, x, 0.0), jnp.where(lane_lo, 0.0, x)], axis=0)


def _pair_col(m, l0):
    return jnp.concatenate([m[:, l0:l0 + 1], m[:, l0 + 1:l0 + 2]], axis=0)


def _gdn_kernel(*refs, zero_init, n_chunks, n_elems):
    it = iter(refs)
    dir_in = [tuple(next(it) for _ in range(4)) for _ in range(2)]
    h0_ref = None if zero_init else next(it)
    o_refs = (next(it), next(it))
    hfin_ref = next(it)
    s_scr = next(it)
    c = pl.program_id(1)
    cs, n = GDN_CHUNK, PAIR_N

    @pl.when(c == 0)
    def _():
        if zero_init:
            s_scr[...] = jnp.zeros_like(s_scr)
        else:
            z = jnp.zeros((GDN_DK, GDN_DV), F32)
            for e in range(n_elems):
                for d in range(2):
                    for p in range(PAIRS):
                        top = jnp.concatenate([h0_ref[e, d, 2 * p], z], axis=1)
                        bot = jnp.concatenate([z, h0_ref[e, d, 2 * p + 1]], axis=1)
                        s_scr[e, d, p] = jnp.concatenate([top, bot], axis=0)

    ri = lax.broadcasted_iota(jnp.int32, (n, n), 0)
    ci = lax.broadcasted_iota(jnp.int32, (n, n), 1)
    same_head = (ri // cs) == (ci // cs)
    eye = (ri == ci).astype(F32)
    blk16 = (ri // 16) == (ci // 16)
    blk32 = (ri // 32) == (ci // 32)
    lane_lo = lax.broadcasted_iota(jnp.int32, (1, n), 1) < cs
    ti = lax.broadcasted_iota(jnp.int32, (cs, cs), 0)
    tj = lax.broadcasted_iota(jnp.int32, (cs, cs), 1)

    chains = [(e, d, p) for e in range(n_elems) for d in range(2) for p in range(PAIRS)]
    incl_d = [same_head & (ri >= ci), same_head & (ri <= ci)]
    strict_d = [same_head & (ri > ci), same_head & (ri < ci)]
    gates_d, gc_d, gct_d, glast_d = {}, {}, {}, {}
    for e in range(n_elems):
        for d in range(2):
            gates = dir_in[d][3][e]
            csum = ((ti >= tj) if d == 0 else (ti <= tj)).astype(BF16)
            ghi, gmid, glo = _split3(gates)
            gc_all = _dot(csum, ghi) + _dot(csum, gmid) + _dot(csum, glo)
            gates_d[e, d] = gates
            gc_d[e, d] = gc_all
            gct_d[e, d] = gc_all.T
            glast_d[e, d] = gc_all[cs - 1:cs, :] if d == 0 else gc_all[0:1, :]

    kst, qst, vst, gcol, bcol, glast, decay, egc = [], [], [], [], [], [], [], []
    for e, d, p in chains:
        q_ref, k_ref, v_ref, _ = dir_in[d]
        sl = slice(p * n, (p + 1) * n)
        lg = GATE_LANE + HEADS * d + 2 * p
        kst.append(_stack_pair(k_ref[e, :, sl], lane_lo))
        qst.append(_stack_pair(q_ref[e, :, sl], lane_lo))
        vst.append(_stack_pair(v_ref[e, :, sl], lane_lo))
        gc = _pair_col(gc_d[e, d], lg)
        gr = jnp.concatenate([gct_d[e, d][lg:lg + 1, :], gct_d[e, d][lg + 1:lg + 2, :]], axis=1)
        gcol.append(gc)
        bcol.append(_pair_col(gates_d[e, d], lg + 2 * HEADS))
        glast.append(jnp.concatenate([jnp.broadcast_to(glast_d[e, d][:, lg:lg + 1], (cs, 1)),
                                      jnp.broadcast_to(glast_d[e, d][:, lg + 1:lg + 2], (cs, 1))], axis=0))
        decay.append(jnp.exp(jnp.where(incl_d[d], gc - gr, -jnp.inf)))
        egc.append(jnp.exp(gc))
    nch = len(chains)
    kstb = [x.astype(BF16) for x in kst]
    gp = [_dot_nt(jnp.concatenate([kstb[i], qst[i].astype(BF16)], axis=0), kstb[i]) for i in range(nch)]
    a_mat = [jnp.where(strict_d[chains[i][1]], bcol[i] * gp[i][:n] * decay[i], 0.0) for i in range(nch)]
    attn = [jnp.where(incl_d[chains[i][1]], gp[i][n:] * decay[i], 0.0).astype(BF16) for i in range(nch)]
    rhs = [jnp.concatenate([vst[i] * bcol[i], kst[i] * (bcol[i] * egc[i])], axis=1).astype(BF16) for i in range(nch)]
    t_inv = _unit_tri_inverse(a_mat, eye, blk16, blk32)
    x = [_dot(t_inv[i].astype(BF16), rhs[i]) for i in range(nch)]
    s_old = [s_scr[e, d, p] for e, d, p in chains]
    sb = [s.astype(BF16) for s in s_old]
    ws = [_dot(jnp.concatenate([x[i][:, n:], qst[i] * egc[i]], axis=0).astype(BF16), sb[i]) for i in range(nch)]
    v_new = [(x[i][:, :n] - ws[i][:n]).astype(BF16) for i in range(nch)]
    o_st = [ws[i][n:] + _dot(attn[i], v_new[i]) for i in range(nch)]
    k_dec_t = [(kst[i] * jnp.exp(glast[i] - gcol[i])).T.astype(BF16) for i in range(nch)]
    for i, (e, d, p) in enumerate(chains):
        s_scr[e, d, p] = s_old[i] * jnp.exp(glast[i]) + _dot(k_dec_t[i], v_new[i])
    for e in range(n_elems):
        for d in range(2):
            o_pairs = [o_st[chains.index((e, d, p))] for p in range(PAIRS)]
            o_refs[d][e] = jnp.concatenate([o[:cs] + o[cs:] for o in o_pairs], axis=1)

    @pl.when(c == n_chunks - 1)
    def _():
        for e in range(n_elems):
            for d in range(2):
                for p in range(PAIRS):
                    s = s_scr[e, d, p]
                    hfin_ref[e, d, 2 * p] = s[:GDN_DK, :GDN_DV]
                    hfin_ref[e, d, 2 * p + 1] = s[GDN_DK:, GDN_DV:]


GDN_ELEMS = 2


def _gdn(qg, kg, vg, gb, h0, n_batch, seq):
    nc = seq // GDN_CHUNK
    ne = GDN_ELEMS
    assert n_batch % ne == 0
    zero_init = h0 is None
    fwd = lambda w: pl.BlockSpec((ne, GDN_CHUNK, w), lambda b, c: (b, c, 0))
    bwd = lambda w: pl.BlockSpec((ne, GDN_CHUNK, w), lambda b, c: (b, nc - 1 - c, 0))
    in_specs = [fwd(GDN_W), fwd(GDN_W), fwd(GDN_W), fwd(LANES), bwd(GDN_W), bwd(GDN_W), bwd(GDN_W), bwd(LANES)]
    seq3 = lambda a: a.reshape(n_batch, seq, a.shape[-1])
    args = [seq3(qg), seq3(kg), seq3(vg), seq3(gb)] * 2
    st_spec = pl.BlockSpec((ne, 2, HEADS, GDN_DK, GDN_DV), lambda b, c: (b, 0, 0, 0, 0))
    if not zero_init:
        in_specs.append(st_spec)
        args.append(h0)
    t = n_batch * seq
    o_f, o_b, h_fin = pl.pallas_call(
        functools.partial(_gdn_kernel, zero_init=zero_init, n_chunks=nc, n_elems=ne),
        out_shape=[jax.ShapeDtypeStruct((n_batch, seq, GDN_W), F32), jax.ShapeDtypeStruct((n_batch, seq, GDN_W), F32),
                   jax.ShapeDtypeStruct((n_batch, 2, HEADS, GDN_DK, GDN_DV), F32)],
        grid=(n_batch // ne, nc),
        in_specs=in_specs,
        out_specs=[fwd(GDN_W), bwd(GDN_W), st_spec],
        scratch_shapes=[pltpu.VMEM((ne, 2, PAIRS, PAIR_N, PAIR_N), F32)],
        compiler_params=_cparams(("parallel", "arbitrary")),
        name="gdn",
    )(*args)
    return o_f.reshape(t, GDN_W), o_b.reshape(t, GDN_W), h_fin


MERGE_TM = 256


def _merge_kernel(x_ref, mod_ref, oa_ref, gz_ref, conv_ref, cx_prev_ref, cx_next_ref, of_ref, ob_ref, mg_ref,
                  wpa_ref, wpb_ref, wpc_ref, wo_ref, cw_ref, gn_ref, ones_ref, fin_ref, o_ref, *,
                  tiles_per_seq, final_norm):
    i = pl.program_id(0)
    first = (i % tiles_per_seq) == 0
    last = (i % tiles_per_seq) == tiles_per_seq - 1
    o_a = oa_ref[...].astype(F32) * _silu(gz_ref[:, 0:MLA_W].astype(F32))
    pa = _dot(o_a.astype(BF16), wpa_ref[...])
    f32 = lambda ref, rows, c0: ref[rows, c0:c0 + CONV_W].astype(F32)
    every = slice(None)
    cx = f32(conv_ref, every, 0) * f32(conv_ref, every, CONV_W)
    pr, nr = slice(HALO - 1, HALO), slice(0, 1)
    prev_row = jnp.where(first, 0.0, f32(cx_prev_ref, pr, 0) * f32(cx_prev_ref, pr, CONV_W))
    next_row = jnp.where(last, 0.0, f32(cx_next_ref, nr, 0) * f32(cx_next_ref, nr, CONV_W))
    down, up = _shift_rows(cx, prev_row, next_row)
    conv = down * cw_ref[0:1, :] + cx * cw_ref[1:2, :] + up * cw_ref[2:3, :]
    o_b = f32(conv_ref, every, 2 * CONV_W) * conv * _silu(f32(conv_ref, every, 3 * CONV_W))
    pb = _dot(o_b.astype(BF16), wpb_ref[...])
    og = of_ref[...] + ob_ref[...]
    ms = _head_sums(og * og, ones_ref[...]) * (1.0 / GDN_DV)
    o_c = og * lax.rsqrt(ms + EPS) * gn_ref[...] * _silu(gz_ref[:, MLA_W:W_GZ].astype(F32))
    pc = _dot(o_c.astype(BF16), wpc_ref[...])
    m = (_sigmoid(mg_ref[:, 0:D_MODEL].astype(F32)) * pa + _sigmoid(mg_ref[:, D_MODEL:2 * D_MODEL].astype(F32)) * pb
         + _sigmoid(mg_ref[:, 2 * D_MODEL:3 * D_MODEL].astype(F32)) * pc)
    r = _dot(m.astype(BF16), wo_ref[...])
    y = x_ref[...] + mod_ref[:, 2 * D_MODEL:3 * D_MODEL] * r
    if final_norm:
        y = _rms(y) * fin_ref[...]
    o_ref[...] = y


def _merge(x2d, mod_l, o_a, u_gz, u_conv, o_f, o_b, u_mg, lw, final_g, seq, cond_row0, cond_per_seq, final_norm):
    t = x2d.shape[0]
    tm = MERGE_TM
    tps = seq // tm
    row = lambda w: pl.BlockSpec((tm, w), lambda i: (i, 0))
    prev_spec, next_spec = _halo_specs(tm, 2 * CONV_W, t)
    mod3 = mod_l.reshape(COND_ROWS, 1, 3 * D_MODEL)
    return pl.pallas_call(
        functools.partial(_merge_kernel, tiles_per_seq=tps, final_norm=final_norm),
        out_shape=jax.ShapeDtypeStruct((t, D_MODEL), F32),
        grid=(t // tm,),
        in_specs=[
            row(D_MODEL),
            pl.BlockSpec((None, 1, 3 * D_MODEL), lambda i: (cond_row0 + (i // tps) * cond_per_seq, 0, 0)),
            row(MLA_W), row(W_GZ), row(W_CONV), prev_spec, next_spec, row(GDN_W), row(GDN_W), row(W_MG),
            _const_spec((MLA_W, D_MODEL)), _const_spec((CONV_W, D_MODEL)), _const_spec((GDN_W, D_MODEL)),
            _const_spec((D_MODEL, D_MODEL)), _const_spec((3, CONV_W)), _const_spec((1, GDN_W)),
            _const_spec((GDN_W, GDN_W)), _const_spec((1, D_MODEL)),
        ],
        out_specs=row(D_MODEL),
        compiler_params=_cparams(("parallel",)),
        name="merge",
    )(x2d, mod3, o_a, u_gz, u_conv, u_conv, u_conv, o_f, o_b, u_mg, lw["w_pa"], lw["w_pb"], lw["w_pc"], lw["w_o"],
      lw["conv_b_w"], lw["gdn_norm_lane"], lw["ones_bd"], final_g.reshape(1, D_MODEL))


_EVEN_ODD = list(range(0, QK_ROPE, 2)) + list(range(1, QK_ROPE, 2))
_SWAPPED = list(range(1, QK_ROPE, 2)) + list(range(0, QK_ROPE, 2))


def _pack_layer_weights(l, w_in, q_norm_g, kv_norm_g, w_uq, w_ukv, conv_b_w, conv_qkv_w, a_log, dt_bias, gdn_norm_g,
                        w_pa, w_pb, w_pc, w_o):
    wi = w_in[l]
    eo = jnp.array(_EVEN_ODD)
    sw = jnp.array(_SWAPPED)
    w_kpe = wi[:, _O_KPE:_O_GA]
    small = jnp.concatenate([w_kpe[:, eo], w_kpe[:, sw], wi[:, _O_AB:_O_MG], w_kpe], axis=1)
    w_in_p = jnp.concatenate([
        wi[:, _O_CQ:_O_KPE], small,
        wi[:, _O_Q:_O_Z],
        wi[:, _O_C:_O_GB], wi[:, _O_B:_O_C], wi[:, _O_GB:_O_Q],
        wi[:, _O_GA:_O_B], wi[:, _O_Z:_O_AB],
        wi[:, _O_MG:_O_END]], axis=1).astype(BF16)
    wq = w_uq[l].reshape(Q_LORA, HEADS, QK_NOPE + QK_ROPE)
    nope, rope = wq[:, :, :QK_NOPE], wq[:, :, QK_NOPE:]
    z32 = jnp.zeros((Q_LORA, HEADS, HEAD_PAD - QK_NOPE - QK_ROPE), F32)
    z64 = jnp.zeros((Q_LORA, HEADS, QK_NOPE), F32)
    w_qa = jnp.concatenate([nope, rope[:, :, eo], z32], axis=-1).reshape(Q_LORA, QK_W).astype(BF16)
    w_qb = jnp.concatenate([z64, rope[:, :, sw], z32], axis=-1).reshape(Q_LORA, QK_W).astype(BF16)
    wkv = w_ukv[l].reshape(KV_LORA, HEADS, QK_NOPE + V_HEAD)
    w_k = jnp.concatenate([wkv[:, :, :QK_NOPE], jnp.zeros((KV_LORA, HEADS, HEAD_PAD - QK_NOPE), F32)],
                          axis=-1).reshape(KV_LORA, QK_W)
    w_v = wkv[:, :, QK_NOPE:].reshape(KV_LORA, MLA_W).astype(BF16)
    src = jnp.arange(LANES)[:, None]
    dst = jnp.arange(QK_W)[None, :] % HEAD_PAD - QK_NOPE
    in_rope = (dst >= 0) & (dst < QK_ROPE)
    e_plain = (in_rope & (src == dst)).astype(F32)
    e_rope = (in_rope & ((src == dst) | (src == dst + QK_ROPE))).astype(F32)
    lane = jnp.arange(LANES)
    gate_idx = jnp.clip(lane - GATE_LANE, 0, 2 * HEADS - 1)
    is_g = (lane >= GATE_LANE) & (lane < GATE_LANE + 2 * HEADS)
    alog_lane = jnp.where(is_g, a_log[l].reshape(-1)[gate_idx], 0.0).reshape(1, LANES)
    dtb_lane = jnp.where(is_g, dt_bias[l].reshape(-1)[gate_idx], 0.0).reshape(1, LANES)
    hid = jnp.arange(GDN_W) // GDN_DK
    return {
        "w_in_p": w_in_p,
        "q_norm_g": q_norm_g[l].reshape(1, Q_LORA), "kv_norm_g": kv_norm_g[l].reshape(1, KV_LORA),
        "w_qa": w_qa, "w_qb": w_qb,
        "w_ke": jnp.concatenate([w_k, e_plain], axis=0).astype(BF16),
        "w_ke_rope": jnp.concatenate([w_k, e_rope], axis=0).astype(BF16),
        "w_v": w_v,
        "conv_qkv_w": conv_qkv_w[l], "conv_b_w": conv_b_w[l],
        "alog_lane": alog_lane, "dtb_lane": dtb_lane,
        "ones_bd": (hid[:, None] == hid[None, :]).astype(BF16),
        "gdn_norm_lane": jnp.tile(gdn_norm_g[l], HEADS).reshape(1, GDN_W),
        "w_pa": w_pa[l].astype(BF16), "w_pb": w_pb[l].astype(BF16), "w_pc": w_pc[l].astype(BF16),
        "w_o": w_o[l].astype(BF16),
    }


def _rope_tables(n_tokens):
    t = jnp.arange(n_tokens)
    row = (t // GRID_W).astype(F32)
    col = (t % GRID_W).astype(F32)
    n_freq = QK_ROPE // 4
    inv_freq = ROPE_THETA ** (-jnp.arange(n_freq, dtype=F32) / n_freq)
    ang = jnp.concatenate([row[:, None] * inv_freq, col[:, None] * inv_freq], axis=-1)
    cos, sin = jnp.cos(ang), jnp.sin(ang)
    cc = jnp.concatenate([cos, cos], axis=-1)
    ss = jnp.concatenate([-sin, sin], axis=-1)
    one = jnp.ones((n_tokens, QK_NOPE), F32)
    z32 = jnp.zeros((n_tokens, QK_ROPE), F32)
    z64 = jnp.zeros((n_tokens, QK_NOPE), F32)
    m1 = jnp.concatenate([one, cc, z32], axis=-1) * SM_SCALE
    m2 = jnp.concatenate([z64, ss, z32], axis=-1) * SM_SCALE
    m3 = jnp.concatenate([cc, ss, z64], axis=-1)
    return m1, m2, m3


def _layer(x2d, mod_l, lw, final_g, n_batch, seq, cond_row0, cond_per_seq, final_norm, rope_tabs=None,
           ctx_kv=None, h0=None, tq=256):
    is_ctx = ctx_kv is None
    u_mla, u_qkv, u_conv, u_gz, u_mg = _inproj(x2d, mod_l, lw["norm_g"], lw["w_in_p"], seq, cond_row0, cond_per_seq)
    outs = _prep(u_mla, u_qkv, lw, seq, rope_tabs, emit_ckvn=is_ctx)
    if is_ctx:
        qf, kf, vf, ckvn, qg, kg, vg, gb = outs
        kv_parts = [(kf, vf, seq)]
    else:
        qf, kf, vf, qg, kg, vg, gb = outs
        ckvn = None
        kv_parts = [ctx_kv, (kf, vf, seq)]
    o_a = _attention(qf, kv_parts, n_batch, seq, tq)
    o_f, o_b, h_fin = _gdn(qg, kg, vg, gb, h0, n_batch, seq)
    y = _merge(x2d, mod_l, o_a, u_gz, u_conv, o_f, o_b, u_mg, lw, final_g, seq, cond_row0, cond_per_seq, final_norm)
    return y, ckvn, u_mla[:, W_MLA - QK_ROPE:], h_fin


def kernel(x_prompt, x_sample, c, cache_ckv, cache_kpe, state_gdn, c_ctx, norm_g, w_ada, b_ada, w_in, q_norm_g,
           kv_norm_g, w_uq, w_ukv, conv_b_w, conv_qkv_w, a_log, dt_bias, gdn_norm_g, w_pa, w_pb, w_pc, w_o,
           final_norm_g):
    n_ctx, seq_ctx, _ = x_prompt.shape
    n_lat, seq_lat, _ = x_sample.shape
    past = cache_ckv.shape[2]
    depth = w_in.shape[0]
    assert n_lat + 1 <= COND_ROWS

    cond = jnp.zeros((COND_ROWS, D_MODEL), F32).at[:n_lat].set(c).at[n_lat].set(c_ctx)
    mod = _ada_mod(cond, w_ada, b_ada)
    rope_tabs = _rope_tables(seq_lat)
    eo = jnp.array(_EVEN_ODD)

    xp = x_prompt.reshape(n_ctx * seq_ctx, D_MODEL)
    xs = x_sample.reshape(n_lat * seq_lat, D_MODEL)
    ckv_out, kpe_out, st_out = [], [], []
    for l in range(depth):
        lw = _pack_layer_weights(l, w_in, q_norm_g, kv_norm_g, w_uq, w_ukv, conv_b_w, conv_qkv_w, a_log, dt_bias,
                                 gdn_norm_g, w_pa, w_pb, w_pc, w_o)
        lw["norm_g"] = norm_g[l]
        last = l == depth - 1
        xp, ckvn, kpe, h_fin = _layer(xp, mod[l], lw, final_norm_g, n_ctx, seq_ctx, n_lat, 0, last)
        ckv_out.append(ckvn.reshape(n_ctx, seq_ctx, KV_LORA))
        kpe_out.append(kpe.reshape(n_ctx, seq_ctx, QK_ROPE))
        st_out.append(h_fin)
        kpe_lanes = jnp.pad(cache_kpe[:, l][..., eo].reshape(n_lat * past, QK_ROPE), ((0, 0), (0, LANES - QK_ROPE)))
        kc, vc = _ctx_kv(cache_ckv[:, l].reshape(n_lat * past, KV_LORA), kpe_lanes, lw)
        xs, _, _, _ = _layer(xs, mod[l], lw, final_norm_g, n_lat, seq_lat, 0, 1, last, rope_tabs=rope_tabs,
                             ctx_kv=(kc, vc, past), h0=state_gdn[:, l], tq=512)
    return (xp.reshape(n_ctx, seq_ctx, D_MODEL), xs.reshape(n_lat, seq_lat, D_MODEL),
            jnp.stack(ckv_out, axis=1), jnp.stack(kpe_out, axis=1), jnp.stack(st_out, axis=1))
```

```python
import functools
import math

import jax
import jax.numpy as jnp
import numpy as np
from jax import lax
from jax.experimental import pallas as pl
from jax.experimental.pallas import tpu as pltpu

F32 = jnp.float32
BF16 = jnp.bfloat16

D_MODEL = 1024
DEPTH = 2
GRID_W = 64
EPS = 1e-6
HEADS = 8
Q_LORA = 384
KV_LORA = 256
QK_NOPE = 64
QK_ROPE = 32
V_HEAD = 64
MLA_W = HEADS * V_HEAD
ROPE_THETA = 10000.0
CONV_W = 512
GDN_DK = 64
GDN_DV = 64
GDN_W = HEADS * GDN_DK
GDN_CHUNK = 64
LANES = 128
HEAD_PAD = 128
QK_W = HEADS * HEAD_PAD
SM_SCALE = (QK_NOPE + QK_ROPE) ** -0.5
COND_ROWS = 16
VMEM_LIMIT = 56 * 1024 * 1024

_O_CQ, _O_CKV, _O_KPE, _O_GA = 0, 384, 640, 672
_O_B, _O_C, _O_X, _O_GB = 1184, 1696, 2208, 2720
_O_Q, _O_K, _O_V, _O_Z = 3232, 3744, 4256, 4768
_O_AB, _O_MG, _O_END = 5280, 5312, 8384
W_MLA = Q_LORA + KV_LORA + LANES
W_QKV = 3 * GDN_W
W_CONV = 4 * CONV_W
W_GZ = MLA_W + GDN_W
W_MG = 3 * D_MODEL
GROUP_WIDTHS = (W_MLA, W_QKV, W_CONV, W_GZ, W_MG)
GROUP_DTYPES = (F32, BF16, BF16, BF16, BF16)
W_IN_PACKED = sum(GROUP_WIDTHS)
GATE_LANE = 64


def _sigmoid(x):
    return 1.0 / (1.0 + jnp.exp(-x))


def _silu(x):
    return x * _sigmoid(x)


def _rms(x):
    return x * lax.rsqrt(jnp.mean(x * x, axis=-1, keepdims=True) + EPS)


def _dot(a, b):
    return jnp.dot(a, b, preferred_element_type=F32)


def _dot_nt(a, b):
    return lax.dot_general(a, b, (((1,), (1,)), ((), ())), preferred_element_type=F32)


def _split3(x):
    hi = x.astype(BF16)
    r = x - hi.astype(F32)
    mid = r.astype(BF16)
    lo = (r - mid.astype(F32)).astype(BF16)
    return hi, mid, lo


def _head_sums(x2, ones_bd):
    hi = x2.astype(BF16)
    lo = (x2 - hi.astype(F32)).astype(BF16)
    return _dot(hi, ones_bd) + _dot(lo, ones_bd)


def _cparams(sem):
    return pltpu.CompilerParams(dimension_semantics=sem, vmem_limit_bytes=VMEM_LIMIT)


def _const_spec(shape):
    nd = len(shape)
    return pl.BlockSpec(shape, lambda *_: (0,) * nd)


def _mod_spec(l, cond_row):
    return pl.BlockSpec((None, None, 1, 3 * D_MODEL), lambda i: (l, cond_row(i), 0, 0))


def _layer_spec(arr, l, **kw):
    nd = arr.ndim - 1
    return pl.BlockSpec((None,) + arr.shape[1:], lambda *_: (l,) + (0,) * nd, **kw)


def _ada_kernel(c_ref, w_ref, b_ref, o_ref):
    sc = _silu(c_ref[...])
    o_ref[...] = jnp.dot(sc, w_ref[...], preferred_element_type=F32, precision=lax.Precision.HIGHEST) + b_ref[...]


def _ada_mod(cond, w_ada, b_ada):
    depth = w_ada.shape[0]
    return pl.pallas_call(
        _ada_kernel,
        out_shape=jax.ShapeDtypeStruct((depth, COND_ROWS, 3 * D_MODEL), F32),
        grid=(depth, 3),
        in_specs=[
            pl.BlockSpec((COND_ROWS, D_MODEL), lambda l, j: (0, 0)),
            pl.BlockSpec((None, D_MODEL, D_MODEL), lambda l, j: (l, 0, j)),
            pl.BlockSpec((None, 1, D_MODEL), lambda l, j: (l, 0, j)),
        ],
        out_specs=pl.BlockSpec((None, COND_ROWS, D_MODEL), lambda l, j: (l, 0, j)),
        compiler_params=_cparams(("arbitrary", "arbitrary")),
        name="ada_mod",
    )(cond, w_ada, b_ada.reshape(depth, 1, 3 * D_MODEL))


IN_TM = 512
IN_CHUNK = 768


def _inproj_kernel(x_ref, mod_ref, g_ref, w_ref, *rest):
    outs, h_scr = rest[:-1], rest[-1]
    x = x_ref[...]
    shift = mod_ref[:, 0:D_MODEL]
    scale = mod_ref[:, D_MODEL:2 * D_MODEL]
    h = (_rms(x) * g_ref[...]) * (1.0 + scale) + shift
    h_scr[...] = h.astype(BF16)
    off = 0
    for o_ref in outs:
        n = o_ref.shape[1]
        for c0 in range(0, n, IN_CHUNK):
            c1 = min(c0 + IN_CHUNK, n)
            o_ref[:, c0:c1] = _dot(h_scr[...], w_ref[:, off + c0:off + c1]).astype(o_ref.dtype)
        off += n


def _inproj(x2d, mod, lw, l, seq, cond_row0, cond_per_seq):
    t = x2d.shape[0]
    assert t % IN_TM == 0 and (cond_per_seq == 0 or seq % IN_TM == 0)
    tps = max(seq // IN_TM, 1)
    return pl.pallas_call(
        _inproj_kernel,
        out_shape=[jax.ShapeDtypeStruct((t, n), dt) for n, dt in zip(GROUP_WIDTHS, GROUP_DTYPES)],
        grid=(t // IN_TM,),
        in_specs=[
            pl.BlockSpec((IN_TM, D_MODEL), lambda i: (i, 0)),
            _mod_spec(l, lambda i: cond_row0 + (i // tps) * cond_per_seq),
            _layer_spec(lw["norm_g"], l),
            _layer_spec(lw["w_in_p"], l, pipeline_mode=pl.Buffered(1)),
        ],
        out_specs=[pl.BlockSpec((IN_TM, n), lambda i: (i, 0)) for n in GROUP_WIDTHS],
        scratch_shapes=[pltpu.VMEM((IN_TM, D_MODEL), BF16)],
        compiler_params=_cparams(("parallel",)),
        name="inproj",
    )(x2d, mod, lw["norm_g"], lw["w_in_p"])


PREP_TM = 512


def _seq_edges(tm, seq, i):
    row = lax.broadcasted_iota(jnp.int32, (tm, 1), 0)
    if seq >= tm:
        tps = seq // tm
        return (row == 0) & ((i % tps) == 0), (row == tm - 1) & ((i % tps) == tps - 1)
    pos = lax.rem(row, seq)
    return pos == 0, pos == seq - 1


def _shift_rows(x, prev_row, next_row, edges):
    n = x.shape[0]
    row = lax.broadcasted_iota(jnp.int32, (n, 1), 0)
    starts, ends = edges
    down = jnp.where(starts, 0.0, jnp.where(row == 0, prev_row, pltpu.roll(x, 1, 0)))
    up = jnp.where(ends, 0.0, jnp.where(row == n - 1, next_row, pltpu.roll(x, n - 1, 0)))
    return down, up


def _prep_kernel(*refs, use_rope, seq, emit_ckvn):
    it = iter(refs)
    mla_ref, qkv_ref, qkv_prev_ref, qkv_next_ref = next(it), next(it), next(it), next(it)
    if use_rope:
        m1_ref, m2_ref, m3_ref = next(it), next(it), next(it)
    gq_ref, gkv_ref, wa_ref = next(it), next(it), next(it)
    if use_rope:
        wb_ref = next(it)
    wke_ref, wv_ref, cw_ref, alog_ref, dtb_ref, ones_ref = (next(it) for _ in range(6))
    qf_ref, kf_ref, vf_ref = next(it), next(it), next(it)
    if emit_ckvn:
        ckvn_ref = next(it)
    qg_ref, kg_ref, vg_ref, gb_ref = next(it), next(it), next(it), next(it)

    edges = _seq_edges(mla_ref.shape[0], seq, pl.program_id(0))

    cq = mla_ref[:, 0:Q_LORA]
    ckv = mla_ref[:, Q_LORA:Q_LORA + KV_LORA]
    small = mla_ref[:, Q_LORA + KV_LORA:W_MLA]
    qn = (_rms(cq) * gq_ref[...]).astype(BF16)
    qa = _dot(qn, wa_ref[...])
    if use_rope:
        qb = _dot(qn, wb_ref[...])
        m1 = m1_ref[...]
        m2 = m2_ref[...]
        for h in range(HEADS):
            sl = slice(h * HEAD_PAD, (h + 1) * HEAD_PAD)
            qf_ref[:, sl] = (qa[:, sl] * m1 + qb[:, sl] * m2).astype(BF16)
        kpe_t = small * m3_ref[...]
    else:
        qf_ref[...] = (qa * SM_SCALE).astype(BF16)
        kpe_t = small
    ckvn = _rms(ckv) * gkv_ref[...]
    if emit_ckvn:
        ckvn_ref[...] = ckvn
    ckvn_b = ckvn.astype(BF16)
    kin = jnp.concatenate([ckvn_b, kpe_t.astype(BF16)], axis=1)
    kf_ref[...] = _dot(kin, wke_ref[...]).astype(BF16)
    vf_ref[...] = _dot(ckvn_b, wv_ref[...]).astype(BF16)

    z1 = small + dtb_ref[...]
    sp = jnp.maximum(z1, 0.0) + jnp.log1p(jnp.exp(-jnp.abs(z1)))
    g = -jnp.exp(alog_ref[...]) * sp
    lane = lax.broadcasted_iota(jnp.int32, (1, LANES), 1)
    gb_ref[...] = jnp.where(lane < GATE_LANE + 2 * HEADS, g, _sigmoid(small))

    ones_bd = ones_ref[...]
    for part, o_ref in enumerate((qg_ref, kg_ref, vg_ref)):
        sl = slice(part * GDN_W, (part + 1) * GDN_W)
        x = qkv_ref[:, sl].astype(F32)
        prev_row = qkv_prev_ref[HALO - 1:HALO, sl].astype(F32)
        next_row = qkv_next_ref[0:1, sl].astype(F32)
        down, up = _shift_rows(x, prev_row, next_row, edges)
        y = down * cw_ref[0:1, sl] + x * cw_ref[1:2, sl] + up * cw_ref[2:3, sl]
        y = _silu(y)
        if part == 0:
            y = y * lax.rsqrt(_head_sums(y * y, ones_bd) + EPS) * (GDN_DK ** -0.5)
        elif part == 1:
            y = y * lax.rsqrt(_head_sums(y * y, ones_bd) + EPS)
        o_ref[...] = y


HALO = 16


def _halo_specs(tm, width, n_rows, col_block=0):
    r = tm // HALO
    nb = n_rows // HALO
    prev = pl.BlockSpec((HALO, width), lambda i: (jnp.maximum(i * r - 1, 0), col_block))
    nxt = pl.BlockSpec((HALO, width), lambda i: (jnp.minimum((i + 1) * r, nb - 1), col_block))
    return prev, nxt


def _prep(u_mla, u_qkv, lw, l, seq, rope_tabs, emit_ckvn):
    t = u_mla.shape[0]
    tm = PREP_TM
    use_rope = rope_tabs is not None
    assert t % tm == 0 and (seq % tm == 0 or tm % seq == 0) and (not use_rope or seq % tm == 0)
    tps = max(seq // tm, 1)
    row = lambda w: pl.BlockSpec((tm, w), lambda i: (i, 0))
    prev_spec, next_spec = _halo_specs(tm, W_QKV, t)
    in_specs = [row(W_MLA), row(W_QKV), prev_spec, next_spec]
    args = [u_mla, u_qkv, u_qkv, u_qkv]
    if use_rope:
        in_specs += [pl.BlockSpec((tm, LANES), lambda i: (i % tps, 0))] * 3
        args += list(rope_tabs)
    names = ["q_norm_g", "kv_norm_g", "w_qa"] + (["w_qb"] if use_rope else [])
    names += ["w_ke_rope" if use_rope else "w_ke", "w_v", "conv_qkv_w", "alog_lane", "dtb_lane"]
    in_specs += [_layer_spec(lw[k], l) for k in names] + [_const_spec((GDN_W, GDN_W))]
    args += [lw[k] for k in names] + [lw["ones_bd"]]
    out_shape = [jax.ShapeDtypeStruct((t, QK_W), BF16), jax.ShapeDtypeStruct((t, QK_W), BF16),
                 jax.ShapeDtypeStruct((t, MLA_W), BF16)]
    out_specs = [row(QK_W), row(QK_W), row(MLA_W)]
    if emit_ckvn:
        out_shape.append(jax.ShapeDtypeStruct((t, KV_LORA), F32))
        out_specs.append(row(KV_LORA))
    out_shape += [jax.ShapeDtypeStruct((t, GDN_W), F32)] * 3 + [jax.ShapeDtypeStruct((t, LANES), F32)]
    out_specs += [row(GDN_W)] * 3 + [row(LANES)]
    outs = pl.pallas_call(
        functools.partial(_prep_kernel, use_rope=use_rope, seq=seq, emit_ckvn=emit_ckvn),
        out_shape=out_shape,
        grid=(t // tm,),
        in_specs=in_specs,
        out_specs=out_specs,
        compiler_params=_cparams(("parallel",)),
        name="prep",
    )(*args)
    return outs


def _ctxkv_kernel(ckv_ref, kpe_ref, wke_ref, wv_ref, kf_ref, vf_ref):
    ckv_b = ckv_ref[...].astype(BF16)
    kin = jnp.concatenate([ckv_b, kpe_ref[...].astype(BF16)], axis=1)
    kf_ref[...] = _dot(kin, wke_ref[...]).astype(BF16)
    vf_ref[...] = _dot(ckv_b, wv_ref[...]).astype(BF16)


def _ctx_kv(ckv2d, kpe_lanes, lw, l):
    t = ckv2d.shape[0]
    tm = 256
    row = lambda w: pl.BlockSpec((tm, w), lambda i: (i, 0))
    return pl.pallas_call(
        _ctxkv_kernel,
        out_shape=[jax.ShapeDtypeStruct((t, QK_W), BF16), jax.ShapeDtypeStruct((t, MLA_W), BF16)],
        grid=(t // tm,),
        in_specs=[row(KV_LORA), row(LANES), _layer_spec(lw["w_ke"], l), _layer_spec(lw["w_v"], l)],
        out_specs=[row(QK_W), row(MLA_W)],
        compiler_params=_cparams(("parallel",)),
        name="ctx_kv",
    )(ckv2d, kpe_lanes, lw["w_ke"], lw["w_v"])


def _attn_kernel(*refs, n_parts, n_heads):
    q_ref = refs[0]
    k_refs = refs[1:1 + n_parts]
    v_refs = refs[1 + n_parts:1 + 2 * n_parts]
    o_ref = refs[1 + 2 * n_parts]
    lane = lax.broadcasted_iota(jnp.int32, (1, 2 * V_HEAD), 1)
    own = [lane < V_HEAD, lane >= V_HEAD]
    heads = range(n_heads)
    hl = [slice(j * HEAD_PAD, (j + 1) * HEAD_PAD) for j in heads]
    pl_ = [slice((j // 2) * 2 * V_HEAD, (j // 2 + 1) * 2 * V_HEAD) for j in heads]
    s = [[_dot_nt(q_ref[:, hl[j]], k_ref[:, hl[j]]) for k_ref in k_refs] for j in heads]
    m = []
    for j in heads:
        mj = s[j][0].max(axis=-1, keepdims=True)
        for sp in s[j][1:]:
            mj = jnp.maximum(mj, sp.max(axis=-1, keepdims=True))
        m.append(mj)
    p = [[jnp.exp(sp - m[j]).astype(BF16) for sp in s[j]] for j in heads]
    acc = []
    for j in heads:
        a = None
        for pj, v_ref in zip(p[j], v_refs):
            t = _dot(pj, jnp.where(own[j % 2], v_ref[:, pl_[j]], 1.0).astype(BF16))
            a = t if a is None else a + t
        acc.append(a)
    for j in range(0, n_heads, 2):
        l0 = acc[j][:, V_HEAD:V_HEAD + 1]
        l1 = acc[j + 1][:, 0:1]
        o_ref[:, pl_[j]] = jnp.where(own[0], acc[j] / l0, acc[j + 1] / l1).astype(o_ref.dtype)


def _attention(qf, kv_parts, n_batch, seq_q, tq, n_heads):
    n_parts = len(kv_parts)
    nq = seq_q // tq
    in_specs = [pl.BlockSpec((tq, n_heads * HEAD_PAD), lambda b, hp, qi: (b * nq + qi, hp))]
    in_specs += [pl.BlockSpec((sk, n_heads * HEAD_PAD), lambda b, hp, qi: (b, hp)) for _, _, sk in kv_parts]
    in_specs += [pl.BlockSpec((sk, n_heads * V_HEAD), lambda b, hp, qi: (b, hp)) for _, _, sk in kv_parts]
    return pl.pallas_call(
        functools.partial(_attn_kernel, n_parts=n_parts, n_heads=n_heads),
        out_shape=jax.ShapeDtypeStruct((n_batch * seq_q, MLA_W), BF16),
        grid=(n_batch, HEADS // n_heads, nq),
        in_specs=in_specs,
        out_specs=pl.BlockSpec((tq, n_heads * V_HEAD), lambda b, hp, qi: (b * nq + qi, hp)),
        compiler_params=_cparams(("parallel", "parallel", "arbitrary")),
        name="attention",
    )(qf, *[k for k, _, _ in kv_parts], *[v for _, v, _ in kv_parts])


PAIRS = HEADS // 2
PAIR_N = 2 * GDN_CHUNK


def _unit_tri_inverse(a_list, eye, blk16, blk32):
    diag = [jnp.where(blk16, a, 0.0) for a in a_list]
    powers = [[x.astype(BF16) for x in diag]]
    for _ in range(3):
        powers.append([_dot(x, x).astype(BF16) for x in powers[-1]])
    t = [eye - x for x in diag]
    for pw in powers[1:]:
        t = [x + _dot(x.astype(BF16), y) for x, y in zip(t, pw)]
    for off in ([jnp.where(blk32 & ~blk16, a, 0.0).astype(BF16) for a in a_list],
                [jnp.where(blk32, 0.0, a).astype(BF16) for a in a_list]):
        tb = [x.astype(BF16) for x in t]
        m = [_dot(x, y).astype(BF16) for x, y in zip(tb, off)]
        t = [x - _dot(y, z) for x, y, z in zip(t, m, tb)]
    return t


def _stack_pair(x, lane_lo):
    return jnp.concatenate([jnp.where(lane_lo, x, 0.0), jnp.where(lane_lo, 0.0, x)], axis=0)


def _pair_col(m, l0):
    return jnp.concatenate([m[:, l0:l0 + 1], m[:, l0 + 1:l0 + 2]], axis=0)


def _gdn_kernel(*refs, zero_init, n_chunks, n_elems):
    it = iter(refs)
    dir_in = [tuple(next(it) for _ in range(4)) for _ in range(2)]
    h0_ref = None if zero_init else next(it)
    o_refs = (next(it), next(it))
    hfin_ref = next(it)
    s_scr = next(it)
    c = pl.program_id(1)
    cs, n = GDN_CHUNK, PAIR_N

    @pl.when(c == 0)
    def _():
        if zero_init:
            s_scr[...] = jnp.zeros_like(s_scr)
        else:
            z = jnp.zeros((GDN_DK, GDN_DV), F32)
            for e in range(n_elems):
                for d in range(2):
                    for p in range(PAIRS):
                        top = jnp.concatenate([h0_ref[e, d, 2 * p], z], axis=1)
                        bot = jnp.concatenate([z, h0_ref[e, d, 2 * p + 1]], axis=1)
                        s_scr[e, d, p] = jnp.concatenate([top, bot], axis=0)

    ri = lax.broadcasted_iota(jnp.int32, (n, n), 0)
    ci = lax.broadcasted_iota(jnp.int32, (n, n), 1)
    same_head = (ri // cs) == (ci // cs)
    eye = (ri == ci).astype(F32)
    blk16 = (ri // 16) == (ci // 16)
    blk32 = (ri // 32) == (ci // 32)
    lane_lo = lax.broadcasted_iota(jnp.int32, (1, n), 1) < cs
    ti = lax.broadcasted_iota(jnp.int32, (cs, cs), 0)
    tj = lax.broadcasted_iota(jnp.int32, (cs, cs), 1)

    chains = [(e, d, p) for e in range(n_elems) for d in range(2) for p in range(PAIRS)]
    incl_d = [same_head & (ri >= ci), same_head & (ri <= ci)]
    strict_d = [same_head & (ri > ci), same_head & (ri < ci)]
    gates_d, gc_d, gct_d, glast_d = {}, {}, {}, {}
    for e in range(n_elems):
        for d in range(2):
            gates = dir_in[d][3][e]
            csum = ((ti >= tj) if d == 0 else (ti <= tj)).astype(BF16)
            ghi, gmid, glo = _split3(gates)
            gc_all = _dot(csum, ghi) + _dot(csum, gmid) + _dot(csum, glo)
            gates_d[e, d] = gates
            gc_d[e, d] = gc_all
            gct_d[e, d] = gc_all.T
            glast_d[e, d] = gc_all[cs - 1:cs, :] if d == 0 else gc_all[0:1, :]

    kst, qst, vst, gcol, bcol, glast, decay, egc = [], [], [], [], [], [], [], []
    for e, d, p in chains:
        q_ref, k_ref, v_ref, _ = dir_in[d]
        sl = slice(p * n, (p + 1) * n)
        lg = GATE_LANE + HEADS * d + 2 * p
        kst.append(_stack_pair(k_ref[e, :, sl], lane_lo))
        qst.append(_stack_pair(q_ref[e, :, sl], lane_lo))
        vst.append(_stack_pair(v_ref[e, :, sl], lane_lo))
        gc = _pair_col(gc_d[e, d], lg)
        gr = jnp.concatenate([gct_d[e, d][lg:lg + 1, :], gct_d[e, d][lg + 1:lg + 2, :]], axis=1)
        gcol.append(gc)
        bcol.append(_pair_col(gates_d[e, d], lg + 2 * HEADS))
        glast.append(jnp.concatenate([jnp.broadcast_to(glast_d[e, d][:, lg:lg + 1], (cs, 1)),
                                      jnp.broadcast_to(glast_d[e, d][:, lg + 1:lg + 2], (cs, 1))], axis=0))
        decay.append(jnp.exp(jnp.where(incl_d[d], gc - gr, -jnp.inf)))
        egc.append(jnp.exp(gc))
    nch = len(chains)
    kstb = [x.astype(BF16) for x in kst]
    gp = [_dot_nt(jnp.concatenate([kstb[i], qst[i].astype(BF16)], axis=0), kstb[i]) for i in range(nch)]
    a_mat = [jnp.where(strict_d[chains[i][1]], bcol[i] * gp[i][:n] * decay[i], 0.0) for i in range(nch)]
    attn = [jnp.where(incl_d[chains[i][1]], gp[i][n:] * decay[i], 0.0).astype(BF16) for i in range(nch)]
    rhs = [jnp.concatenate([vst[i] * bcol[i], kst[i] * (bcol[i] * egc[i])], axis=1).astype(BF16) for i in range(nch)]
    t_inv = _unit_tri_inverse(a_mat, eye, blk16, blk32)
    x = [_dot(t_inv[i].astype(BF16), rhs[i]) for i in range(nch)]
    s_old = [s_scr[e, d, p] for e, d, p in chains]
    sb = [s.astype(BF16) for s in s_old]
    ws = [_dot(jnp.concatenate([x[i][:, n:], qst[i] * egc[i]], axis=0).astype(BF16), sb[i]) for i in range(nch)]
    v_new = [(x[i][:, :n] - ws[i][:n]).astype(BF16) for i in range(nch)]
    o_st = [ws[i][n:] + _dot(attn[i], v_new[i]) for i in range(nch)]
    k_dec_t = [(kst[i] * jnp.exp(glast[i] - gcol[i])).T.astype(BF16) for i in range(nch)]
    for i, (e, d, p) in enumerate(chains):
        s_scr[e, d, p] = s_old[i] * jnp.exp(glast[i]) + _dot(k_dec_t[i], v_new[i])
    for e in range(n_elems):
        for d in range(2):
            o_pairs = [o_st[chains.index((e, d, p))] for p in range(PAIRS)]
            o_refs[d][e] = jnp.concatenate([o[:cs] + o[cs:] for o in o_pairs], axis=1)

    @pl.when(c == n_chunks - 1)
    def _():
        for e in range(n_elems):
            for d in range(2):
                for p in range(PAIRS):
                    s = s_scr[e, d, p]
                    hfin_ref[e, d, 2 * p] = s[:GDN_DK, :GDN_DV]
                    hfin_ref[e, d, 2 * p + 1] = s[GDN_DK:, GDN_DV:]


GDN_ELEMS = 2


def _gdn(qg, kg, vg, gb, h0, n_batch, seq):
    nc = seq // GDN_CHUNK
    ne = GDN_ELEMS
    assert n_batch % ne == 0
    zero_init = h0 is None
    fwd = lambda w: pl.BlockSpec((ne, GDN_CHUNK, w), lambda b, c: (b, c, 0))
    bwd = lambda w: pl.BlockSpec((ne, GDN_CHUNK, w), lambda b, c: (b, nc - 1 - c, 0))
    in_specs = [fwd(GDN_W), fwd(GDN_W), fwd(GDN_W), fwd(LANES), bwd(GDN_W), bwd(GDN_W), bwd(GDN_W), bwd(LANES)]
    seq3 = lambda a: a.reshape(n_batch, seq, a.shape[-1])
    args = [seq3(qg), seq3(kg), seq3(vg), seq3(gb)] * 2
    st_spec = pl.BlockSpec((ne, 2, HEADS, GDN_DK, GDN_DV), lambda b, c: (b, 0, 0, 0, 0))
    if not zero_init:
        in_specs.append(st_spec)
        args.append(h0)
    t = n_batch * seq
    o_f, o_b, h_fin = pl.pallas_call(
        functools.partial(_gdn_kernel, zero_init=zero_init, n_chunks=nc, n_elems=ne),
        out_shape=[jax.ShapeDtypeStruct((n_batch, seq, GDN_W), F32), jax.ShapeDtypeStruct((n_batch, seq, GDN_W), F32),
                   jax.ShapeDtypeStruct((n_batch, 2, HEADS, GDN_DK, GDN_DV), F32)],
        grid=(n_batch // ne, nc),
        in_specs=in_specs,
        out_specs=[fwd(GDN_W), bwd(GDN_W), st_spec],
        scratch_shapes=[pltpu.VMEM((ne, 2, PAIRS, PAIR_N, PAIR_N), F32)],
        compiler_params=_cparams(("parallel", "arbitrary")),
        name="gdn",
    )(*args)
    return o_f.reshape(t, GDN_W), o_b.reshape(t, GDN_W), h_fin


MERGE_TM = 512


def _merge_kernel(x_ref, mod_ref, oa_ref, gz_ref, conv_ref, cx_prev_ref, cx_next_ref, of_ref, ob_ref, mg_ref,
                  wpa_ref, wpb_ref, wpc_ref, wo_ref, cw_ref, gn_ref, ones_ref, fin_ref, o_ref, *,
                  seq, final_norm):
    edges = _seq_edges(x_ref.shape[0], seq, pl.program_id(0))
    o_a = oa_ref[...].astype(F32) * _silu(gz_ref[:, 0:MLA_W].astype(F32))
    pa = _dot(o_a.astype(BF16), wpa_ref[...])
    f32 = lambda ref, rows, c0: ref[rows, c0:c0 + CONV_W].astype(F32)
    every = slice(None)
    cx = f32(conv_ref, every, 0) * f32(conv_ref, every, CONV_W)
    pr, nr = slice(HALO - 1, HALO), slice(0, 1)
    prev_row = f32(cx_prev_ref, pr, 0) * f32(cx_prev_ref, pr, CONV_W)
    next_row = f32(cx_next_ref, nr, 0) * f32(cx_next_ref, nr, CONV_W)
    down, up = _shift_rows(cx, prev_row, next_row, edges)
    conv = down * cw_ref[0:1, :] + cx * cw_ref[1:2, :] + up * cw_ref[2:3, :]
    o_b = f32(conv_ref, every, 2 * CONV_W) * conv * _silu(f32(conv_ref, every, 3 * CONV_W))
    pb = _dot(o_b.astype(BF16), wpb_ref[...])
    og = of_ref[...] + ob_ref[...]
    ms = _head_sums(og * og, ones_ref[...]) * (1.0 / GDN_DV)
    o_c = og * lax.rsqrt(ms + EPS) * gn_ref[...] * _silu(gz_ref[:, MLA_W:W_GZ].astype(F32))
    pc = _dot(o_c.astype(BF16), wpc_ref[...])
    m = (_sigmoid(mg_ref[:, 0:D_MODEL].astype(F32)) * pa + _sigmoid(mg_ref[:, D_MODEL:2 * D_MODEL].astype(F32)) * pb
         + _sigmoid(mg_ref[:, 2 * D_MODEL:3 * D_MODEL].astype(F32)) * pc)
    r = _dot(m.astype(BF16), wo_ref[...])
    y = x_ref[...] + mod_ref[:, 2 * D_MODEL:3 * D_MODEL] * r
    if final_norm:
        y = _rms(y) * fin_ref[...]
    o_ref[...] = y


def _merge(x2d, mod, o_a, u_gz, u_conv, o_f, o_b, u_mg, lw, l, final_g, seq, cond_row0, cond_per_seq, final_norm):
    t = x2d.shape[0]
    tm = MERGE_TM
    assert t % tm == 0 and (seq % tm == 0 or tm % seq == 0) and (cond_per_seq == 0 or seq % tm == 0)
    tps = max(seq // tm, 1)
    row = lambda w: pl.BlockSpec((tm, w), lambda i: (i, 0))
    prev_spec, next_spec = _halo_specs(tm, 2 * CONV_W, t)
    names = ["w_pa", "w_pb", "w_pc", "w_o", "conv_b_w", "gdn_norm_lane"]
    return pl.pallas_call(
        functools.partial(_merge_kernel, seq=seq, final_norm=final_norm),
        out_shape=jax.ShapeDtypeStruct((t, D_MODEL), F32),
        grid=(t // tm,),
        in_specs=[
            row(D_MODEL),
            _mod_spec(l, lambda i: cond_row0 + (i // tps) * cond_per_seq),
            row(MLA_W), row(W_GZ), row(W_CONV), prev_spec, next_spec, row(GDN_W), row(GDN_W), row(W_MG),
        ] + [_layer_spec(lw[k], l) for k in names] + [_const_spec((GDN_W, GDN_W)), _const_spec((1, D_MODEL))],
        out_specs=row(D_MODEL),
        compiler_params=_cparams(("parallel",)),
        name="merge",
    )(x2d, mod, o_a, u_gz, u_conv, u_conv, u_conv, o_f, o_b, u_mg, *[lw[k] for k in names], lw["ones_bd"],
      final_g.reshape(1, D_MODEL))


_EVEN_ODD = list(range(0, QK_ROPE, 2)) + list(range(1, QK_ROPE, 2))
_SWAPPED = list(range(1, QK_ROPE, 2)) + list(range(0, QK_ROPE, 2))


def _pack_weights(w_in, q_norm_g, kv_norm_g, w_uq, w_ukv, conv_b_w, conv_qkv_w, a_log, dt_bias, gdn_norm_g,
                  w_pa, w_pb, w_pc, w_o, norm_g):
    depth = w_in.shape[0]
    eo = np.array(_EVEN_ODD)
    sw = np.array(_SWAPPED)
    wi = w_in.astype(BF16)
    w_kpe = wi[..., _O_KPE:_O_GA]
    small = jnp.concatenate([w_kpe[..., eo], w_kpe[..., sw], wi[..., _O_AB:_O_MG], w_kpe], axis=-1)
    w_in_p = jnp.concatenate([
        wi[..., _O_CQ:_O_KPE], small,
        wi[..., _O_Q:_O_Z],
        wi[..., _O_C:_O_GB], wi[..., _O_B:_O_C], wi[..., _O_GB:_O_Q],
        wi[..., _O_GA:_O_B], wi[..., _O_Z:_O_AB],
        wi[..., _O_MG:_O_END]], axis=-1)
    wq = w_uq.astype(BF16).reshape(depth, Q_LORA, HEADS, QK_NOPE + QK_ROPE)
    nope, rope = wq[..., :QK_NOPE], wq[..., QK_NOPE:]
    z32 = jnp.zeros((depth, Q_LORA, HEADS, HEAD_PAD - QK_NOPE - QK_ROPE), BF16)
    z64 = jnp.zeros((depth, Q_LORA, HEADS, QK_NOPE), BF16)
    w_qa = jnp.concatenate([nope, rope[..., eo], z32], axis=-1).reshape(depth, Q_LORA, QK_W)
    w_qb = jnp.concatenate([z64, rope[..., sw], z32], axis=-1).reshape(depth, Q_LORA, QK_W)
    wkv = w_ukv.astype(BF16).reshape(depth, KV_LORA, HEADS, QK_NOPE + V_HEAD)
    w_k = jnp.concatenate([wkv[..., :QK_NOPE], jnp.zeros((depth, KV_LORA, HEADS, HEAD_PAD - QK_NOPE), BF16)],
                          axis=-1).reshape(depth, KV_LORA, QK_W)
    w_v = wkv[..., QK_NOPE:].reshape(depth, KV_LORA, MLA_W)
    src = np.arange(LANES)[:, None]
    dst = np.arange(QK_W)[None, :] % HEAD_PAD - QK_NOPE
    in_rope = (dst >= 0) & (dst < QK_ROPE)
    e_plain = jnp.asarray(np.broadcast_to(in_rope & (src == dst), (depth, LANES, QK_W)), BF16)
    e_rope = jnp.asarray(np.broadcast_to(in_rope & ((src == dst) | (src == dst + QK_ROPE)), (depth, LANES, QK_W)), BF16)
    gate_pad = ((0, 0), (0, 0), (GATE_LANE, LANES - GATE_LANE - 2 * HEADS))
    hid = np.arange(GDN_W) // GDN_DK
    return {
        "w_in_p": w_in_p, "norm_g": norm_g.reshape(depth, 1, D_MODEL),
        "q_norm_g": q_norm_g.reshape(depth, 1, Q_LORA), "kv_norm_g": kv_norm_g.reshape(depth, 1, KV_LORA),
        "w_qa": w_qa, "w_qb": w_qb,
        "w_ke": jnp.concatenate([w_k, e_plain], axis=1), "w_ke_rope": jnp.concatenate([w_k, e_rope], axis=1),
        "w_v": w_v,
        "conv_qkv_w": conv_qkv_w, "conv_b_w": conv_b_w,
        "alog_lane": jnp.pad(a_log.reshape(depth, 1, 2 * HEADS), gate_pad),
        "dtb_lane": jnp.pad(dt_bias.reshape(depth, 1, 2 * HEADS), gate_pad),
        "ones_bd": jnp.asarray(hid[:, None] == hid[None, :], BF16),
        "gdn_norm_lane": jnp.tile(gdn_norm_g, (1, HEADS)).reshape(depth, 1, GDN_W),
        "w_pa": w_pa.astype(BF16), "w_pb": w_pb.astype(BF16), "w_pc": w_pc.astype(BF16), "w_o": w_o.astype(BF16),
    }


def _rope_tables(n_tokens):
    t = np.arange(n_tokens)
    row = (t // GRID_W).astype(np.float32)
    col = (t % GRID_W).astype(np.float32)
    n_freq = QK_ROPE // 4
    inv_freq = (np.float32(ROPE_THETA) ** (-np.arange(n_freq, dtype=np.float32) / n_freq)).astype(np.float32)
    ang = np.concatenate([row[:, None] * inv_freq, col[:, None] * inv_freq], axis=-1).astype(np.float32)
    cos, sin = np.cos(ang), np.sin(ang)
    cc = np.concatenate([cos, cos], axis=-1)
    ss = np.concatenate([-sin, sin], axis=-1)
    one = np.ones((n_tokens, QK_NOPE), np.float32)
    z32 = np.zeros((n_tokens, QK_ROPE), np.float32)
    z64 = np.zeros((n_tokens, QK_NOPE), np.float32)
    m1 = np.concatenate([one, cc, z32], axis=-1) * np.float32(SM_SCALE)
    m2 = np.concatenate([z64, ss, z32], axis=-1) * np.float32(SM_SCALE)
    m3 = np.concatenate([cc, ss, z64], axis=-1)
    return tuple(jnp.asarray(m, F32) for m in (m1, m2, m3))


def _layer(x2d, mod, lw, l, final_g, n_batch, seq, cond_row0, cond_per_seq, final_norm, rope_tabs=None,
           ctx_kv=None, h0=None, tq=256, attn_heads=HEADS):
    is_ctx = ctx_kv is None
    u_mla, u_qkv, u_conv, u_gz, u_mg = _inproj(x2d, mod, lw, l, seq, cond_row0, cond_per_seq)
    outs = _prep(u_mla, u_qkv, lw, l, seq, rope_tabs, emit_ckvn=is_ctx)
    if is_ctx:
        qf, kf, vf, ckvn, qg, kg, vg, gb = outs
        kv_parts = [(kf, vf, seq)]
    else:
        qf, kf, vf, qg, kg, vg, gb = outs
        ckvn = None
        kv_parts = [ctx_kv, (kf, vf, seq)]
    o_a = _attention(qf, kv_parts, n_batch, seq, tq, attn_heads)
    o_f, o_b, h_fin = _gdn(qg, kg, vg, gb, h0, n_batch, seq)
    y = _merge(x2d, mod, o_a, u_gz, u_conv, o_f, o_b, u_mg, lw, l, final_g, seq, cond_row0, cond_per_seq, final_norm)
    return y, ckvn, u_mla[:, W_MLA - QK_ROPE:], h_fin


def kernel(x_prompt, x_sample, c, cache_ckv, cache_kpe, state_gdn, c_ctx, norm_g, w_ada, b_ada, w_in, q_norm_g,
           kv_norm_g, w_uq, w_ukv, conv_b_w, conv_qkv_w, a_log, dt_bias, gdn_norm_g, w_pa, w_pb, w_pc, w_o,
           final_norm_g):
    n_ctx, seq_ctx, _ = x_prompt.shape
    n_lat, seq_lat, _ = x_sample.shape
    past = cache_ckv.shape[2]
    depth = w_in.shape[0]
    assert n_lat + 1 <= COND_ROWS

    cond = jnp.concatenate([c, c_ctx[None, :], jnp.zeros((COND_ROWS - n_lat - 1, D_MODEL), F32)], axis=0)
    mod = _ada_mod(cond, w_ada, b_ada).reshape(depth, COND_ROWS, 1, 3 * D_MODEL)
    rope_tabs = _rope_tables(seq_lat)
    lw = _pack_weights(w_in, q_norm_g, kv_norm_g, w_uq, w_ukv, conv_b_w, conv_qkv_w, a_log, dt_bias, gdn_norm_g,
                       w_pa, w_pb, w_pc, w_o, norm_g)
    kpe_ctx = jnp.pad(cache_kpe[..., np.array(_EVEN_ODD)], ((0, 0),) * 3 + ((0, LANES - QK_ROPE),))

    xp = x_prompt.reshape(n_ctx * seq_ctx, D_MODEL)
    xs = x_sample.reshape(n_lat * seq_lat, D_MODEL)
    ckv_out, kpe_out, st_out = [], [], []
    for l in range(depth):
        last = l == depth - 1
        xp, ckvn, kpe, h_fin = _layer(xp, mod, lw, l, final_norm_g, n_ctx, seq_ctx, n_lat, 0, last)
        ckv_out.append(ckvn.reshape(n_ctx, seq_ctx, KV_LORA))
        kpe_out.append(kpe.reshape(n_ctx, seq_ctx, QK_ROPE))
        st_out.append(h_fin)
        kc, vc = _ctx_kv(cache_ckv[:, l].reshape(n_lat * past, KV_LORA), kpe_ctx[:, l].reshape(n_lat * past, LANES),
                         lw, l)
        xs, _, _, _ = _layer(xs, mod, lw, l, final_norm_g, n_lat, seq_lat, 0, 1, last, rope_tabs=rope_tabs,
                             ctx_kv=(kc, vc, past), h0=state_gdn[:, l], tq=512, attn_heads=2)
    return (xp.reshape(n_ctx, seq_ctx, D_MODEL), xs.reshape(n_lat, seq_lat, D_MODEL),
            jnp.stack(ckv_out, axis=1), jnp.stack(kpe_out, axis=1), jnp.stack(st_out, axis=1))
```

```python
import functools
import math

import jax
import jax.numpy as jnp
import numpy as np
from jax import lax
from jax.experimental import pallas as pl
from jax.experimental.pallas import tpu as pltpu

F32 = jnp.float32
BF16 = jnp.bfloat16

D_MODEL = 1024
DEPTH = 2
GRID_W = 64
EPS = 1e-6
HEADS = 8
Q_LORA = 384
KV_LORA = 256
QK_NOPE = 64
QK_ROPE = 32
V_HEAD = 64
MLA_W = HEADS * V_HEAD
ROPE_THETA = 10000.0
CONV_W = 512
GDN_DK = 64
GDN_DV = 64
GDN_W = HEADS * GDN_DK
GDN_CHUNK = 64
LANES = 128
HEAD_PAD = 128
QK_W = HEADS * HEAD_PAD
SM_SCALE = (QK_NOPE + QK_ROPE) ** -0.5
COND_ROWS = 16
VMEM_LIMIT = 56 * 1024 * 1024

_O_CQ, _O_CKV, _O_KPE, _O_GA = 0, 384, 640, 672
_O_B, _O_C, _O_X, _O_GB = 1184, 1696, 2208, 2720
_O_Q, _O_K, _O_V, _O_Z = 3232, 3744, 4256, 4768
_O_AB, _O_MG, _O_END = 5280, 5312, 8384
W_MLA = Q_LORA + KV_LORA + LANES
W_QKV = 3 * GDN_W
W_CONV = 4 * CONV_W
W_GZ = MLA_W + GDN_W
W_MG = 3 * D_MODEL
GROUP_WIDTHS = (W_MLA, W_QKV, W_CONV, W_GZ, W_MG)
GROUP_DTYPES = (F32, BF16, BF16, BF16, BF16)
W_IN_PACKED = sum(GROUP_WIDTHS)
GATE_LANE = 64


def _sigmoid(x):
    return 1.0 / (1.0 + jnp.exp(-x))


def _silu(x):
    return x * _sigmoid(x)


def _rms(x):
    return x * lax.rsqrt(jnp.mean(x * x, axis=-1, keepdims=True) + EPS)


def _dot(a, b):
    return jnp.dot(a, b, preferred_element_type=F32)


def _dot_nt(a, b):
    return lax.dot_general(a, b, (((1,), (1,)), ((), ())), preferred_element_type=F32)


def _split3(x):
    hi = x.astype(BF16)
    r = x - hi.astype(F32)
    mid = r.astype(BF16)
    lo = (r - mid.astype(F32)).astype(BF16)
    return hi, mid, lo


def _head_sums(x2, ones_bd):
    hi = x2.astype(BF16)
    lo = (x2 - hi.astype(F32)).astype(BF16)
    return _dot(hi, ones_bd) + _dot(lo, ones_bd)


def _cparams(sem):
    return pltpu.CompilerParams(dimension_semantics=sem, vmem_limit_bytes=VMEM_LIMIT)


def _const_spec(shape):
    nd = len(shape)
    return pl.BlockSpec(shape, lambda *_: (0,) * nd)


def _mod_spec(l, cond_row):
    return pl.BlockSpec((None, None, 1, 3 * D_MODEL), lambda i: (l, cond_row(i), 0, 0))


def _layer_spec(arr, l, **kw):
    nd = arr.ndim - 1
    return pl.BlockSpec((None,) + arr.shape[1:], lambda *_: (l,) + (0,) * nd, **kw)


def _ada_kernel(c_ref, w_ref, b_ref, o_ref):
    sc = _silu(c_ref[...])
    o_ref[...] = jnp.dot(sc, w_ref[...], preferred_element_type=F32, precision=lax.Precision.HIGHEST) + b_ref[...]


def _ada_mod(cond, w_ada, b_ada):
    depth = w_ada.shape[0]
    return pl.pallas_call(
        _ada_kernel,
        out_shape=jax.ShapeDtypeStruct((depth, COND_ROWS, 3 * D_MODEL), F32),
        grid=(depth, 3),
        in_specs=[
            pl.BlockSpec((COND_ROWS, D_MODEL), lambda l, j: (0, 0)),
            pl.BlockSpec((None, D_MODEL, D_MODEL), lambda l, j: (l, 0, j)),
            pl.BlockSpec((None, 1, D_MODEL), lambda l, j: (l, 0, j)),
        ],
        out_specs=pl.BlockSpec((None, COND_ROWS, D_MODEL), lambda l, j: (l, 0, j)),
        compiler_params=_cparams(("arbitrary", "arbitrary")),
        name="ada_mod",
    )(cond, w_ada, b_ada.reshape(depth, 1, 3 * D_MODEL))


IN_TM = 512
IN_CHUNK = 768


def _inproj_kernel(x_ref, mod_ref, g_ref, w_ref, *rest):
    outs, h_scr = rest[:-1], rest[-1]
    x = x_ref[...]
    shift = mod_ref[:, 0:D_MODEL]
    scale = mod_ref[:, D_MODEL:2 * D_MODEL]
    h = (_rms(x) * g_ref[...]) * (1.0 + scale) + shift
    h_scr[...] = h.astype(BF16)
    off = 0
    for o_ref in outs:
        n = o_ref.shape[1]
        for c0 in range(0, n, IN_CHUNK):
            c1 = min(c0 + IN_CHUNK, n)
            o_ref[:, c0:c1] = _dot(h_scr[...], w_ref[:, off + c0:off + c1]).astype(o_ref.dtype)
        off += n


def _inproj(x2d, mod, lw, l, seq, cond_row0, cond_per_seq):
    t = x2d.shape[0]
    assert t % IN_TM == 0 and (cond_per_seq == 0 or seq % IN_TM == 0)
    tps = max(seq // IN_TM, 1)
    return pl.pallas_call(
        _inproj_kernel,
        out_shape=[jax.ShapeDtypeStruct((t, n), dt) for n, dt in zip(GROUP_WIDTHS, GROUP_DTYPES)],
        grid=(t // IN_TM,),
        in_specs=[
            pl.BlockSpec((IN_TM, D_MODEL), lambda i: (i, 0)),
            _mod_spec(l, lambda i: cond_row0 + (i // tps) * cond_per_seq),
            _layer_spec(lw["norm_g"], l),
            _layer_spec(lw["w_in_p"], l, pipeline_mode=pl.Buffered(1)),
        ],
        out_specs=[pl.BlockSpec((IN_TM, n), lambda i: (i, 0)) for n in GROUP_WIDTHS],
        scratch_shapes=[pltpu.VMEM((IN_TM, D_MODEL), BF16)],
        compiler_params=_cparams(("parallel",)),
        name="inproj",
    )(x2d, mod, lw["norm_g"], lw["w_in_p"])


PREP_TM = 512


def _seq_edges(tm, seq, i):
    row = lax.broadcasted_iota(jnp.int32, (tm, 1), 0)
    if seq >= tm:
        tps = seq // tm
        return (row == 0) & ((i % tps) == 0), (row == tm - 1) & ((i % tps) == tps - 1)
    pos = lax.rem(row, seq)
    return pos == 0, pos == seq - 1


def _shift_rows(x, prev_row, next_row, edges):
    n = x.shape[0]
    row = lax.broadcasted_iota(jnp.int32, (n, 1), 0)
    starts, ends = edges
    down = jnp.where(starts, 0.0, jnp.where(row == 0, prev_row, pltpu.roll(x, 1, 0)))
    up = jnp.where(ends, 0.0, jnp.where(row == n - 1, next_row, pltpu.roll(x, n - 1, 0)))
    return down, up


def _prep_kernel(*refs, use_rope, seq, emit_ckvn):
    it = iter(refs)
    mla_ref, qkv_ref, qkv_prev_ref, qkv_next_ref = next(it), next(it), next(it), next(it)
    if use_rope:
        m1_ref, m2_ref, m3_ref = next(it), next(it), next(it)
    gq_ref, gkv_ref, wa_ref = next(it), next(it), next(it)
    if use_rope:
        wb_ref = next(it)
    wke_ref, wv_ref, cw_ref, alog_ref, dtb_ref, ones_ref = (next(it) for _ in range(6))
    qf_ref, kf_ref, vf_ref = next(it), next(it), next(it)
    if emit_ckvn:
        ckvn_ref = next(it)
    qg_ref, kg_ref, vg_ref, gb_ref = next(it), next(it), next(it), next(it)

    edges = _seq_edges(mla_ref.shape[0], seq, pl.program_id(0))

    cq = mla_ref[:, 0:Q_LORA]
    ckv = mla_ref[:, Q_LORA:Q_LORA + KV_LORA]
    small = mla_ref[:, Q_LORA + KV_LORA:W_MLA]
    qn = (_rms(cq) * gq_ref[...]).astype(BF16)
    qa = _dot(qn, wa_ref[...])
    if use_rope:
        qb = _dot(qn, wb_ref[...])
        m1 = m1_ref[...]
        m2 = m2_ref[...]
        for h in range(HEADS):
            sl = slice(h * HEAD_PAD, (h + 1) * HEAD_PAD)
            qf_ref[:, sl] = (qa[:, sl] * m1 + qb[:, sl] * m2).astype(BF16)
        kpe_t = small * m3_ref[...]
    else:
        qf_ref[...] = (qa * SM_SCALE).astype(BF16)
        kpe_t = small
    ckvn = _rms(ckv) * gkv_ref[...]
    if emit_ckvn:
        ckvn_ref[...] = ckvn
    ckvn_b = ckvn.astype(BF16)
    kin = jnp.concatenate([ckvn_b, kpe_t.astype(BF16)], axis=1)
    kf_ref[...] = _dot(kin, wke_ref[...]).astype(BF16)
    vf_ref[...] = _dot(ckvn_b, wv_ref[...]).astype(BF16)

    z1 = small + dtb_ref[...]
    sp = jnp.maximum(z1, 0.0) + jnp.log1p(jnp.exp(-jnp.abs(z1)))
    g = -jnp.exp(alog_ref[...]) * sp
    lane = lax.broadcasted_iota(jnp.int32, (1, LANES), 1)
    gb_ref[...] = jnp.where(lane < GATE_LANE + 2 * HEADS, g, _sigmoid(small))

    ones_bd = ones_ref[...]
    for part, o_ref in enumerate((qg_ref, kg_ref, vg_ref)):
        sl = slice(part * GDN_W, (part + 1) * GDN_W)
        x = qkv_ref[:, sl].astype(F32)
        prev_row = qkv_prev_ref[HALO - 1:HALO, sl].astype(F32)
        next_row = qkv_next_ref[0:1, sl].astype(F32)
        down, up = _shift_rows(x, prev_row, next_row, edges)
        y = down * cw_ref[0:1, sl] + x * cw_ref[1:2, sl] + up * cw_ref[2:3, sl]
        y = _silu(y)
        if part == 0:
            y = y * lax.rsqrt(_head_sums(y * y, ones_bd) + EPS) * (GDN_DK ** -0.5)
        elif part == 1:
            y = y * lax.rsqrt(_head_sums(y * y, ones_bd) + EPS)
        o_ref[...] = y


HALO = 16


def _halo_specs(tm, width, n_rows, col_block=0):
    r = tm // HALO
    nb = n_rows // HALO
    prev = pl.BlockSpec((HALO, width), lambda i: (jnp.maximum(i * r - 1, 0), col_block))
    nxt = pl.BlockSpec((HALO, width), lambda i: (jnp.minimum((i + 1) * r, nb - 1), col_block))
    return prev, nxt


def _prep(u_mla, u_qkv, lw, l, seq, rope_tabs, emit_ckvn):
    t = u_mla.shape[0]
    tm = PREP_TM
    use_rope = rope_tabs is not None
    assert t % tm == 0 and (seq % tm == 0 or tm % seq == 0) and (not use_rope or seq % tm == 0)
    tps = max(seq // tm, 1)
    row = lambda w: pl.BlockSpec((tm, w), lambda i: (i, 0))
    prev_spec, next_spec = _halo_specs(tm, W_QKV, t)
    in_specs = [row(W_MLA), row(W_QKV), prev_spec, next_spec]
    args = [u_mla, u_qkv, u_qkv, u_qkv]
    if use_rope:
        in_specs += [pl.BlockSpec((tm, LANES), lambda i: (i % tps, 0))] * 3
        args += list(rope_tabs)
    names = ["q_norm_g", "kv_norm_g", "w_qa"] + (["w_qb"] if use_rope else [])
    names += ["w_ke_rope" if use_rope else "w_ke", "w_v", "conv_qkv_w", "alog_lane", "dtb_lane"]
    in_specs += [_layer_spec(lw[k], l) for k in names] + [_const_spec((GDN_W, GDN_W))]
    args += [lw[k] for k in names] + [lw["ones_bd"]]
    out_shape = [jax.ShapeDtypeStruct((t, QK_W), BF16), jax.ShapeDtypeStruct((t, QK_W), BF16),
                 jax.ShapeDtypeStruct((t, MLA_W), BF16)]
    out_specs = [row(QK_W), row(QK_W), row(MLA_W)]
    if emit_ckvn:
        out_shape.append(jax.ShapeDtypeStruct((t, KV_LORA), F32))
        out_specs.append(row(KV_LORA))
    out_shape += [jax.ShapeDtypeStruct((t, GDN_W), F32)] * 3 + [jax.ShapeDtypeStruct((t, LANES), F32)]
    out_specs += [row(GDN_W)] * 3 + [row(LANES)]
    outs = pl.pallas_call(
        functools.partial(_prep_kernel, use_rope=use_rope, seq=seq, emit_ckvn=emit_ckvn),
        out_shape=out_shape,
        grid=(t // tm,),
        in_specs=in_specs,
        out_specs=out_specs,
        compiler_params=_cparams(("parallel",)),
        name="prep",
    )(*args)
    return outs


def _ctxkv_kernel(ckv_ref, kpe_ref, wke_ref, wv_ref, kf_ref, vf_ref):
    ckv_b = ckv_ref[...].astype(BF16)
    kin = jnp.concatenate([ckv_b, kpe_ref[...].astype(BF16)], axis=1)
    kf_ref[...] = _dot(kin, wke_ref[...]).astype(BF16)
    vf_ref[...] = _dot(ckv_b, wv_ref[...]).astype(BF16)


def _ctx_kv(ckv2d, kpe_lanes, lw, l):
    t = ckv2d.shape[0]
    tm = 256
    row = lambda w: pl.BlockSpec((tm, w), lambda i: (i, 0))
    return pl.pallas_call(
        _ctxkv_kernel,
        out_shape=[jax.ShapeDtypeStruct((t, QK_W), BF16), jax.ShapeDtypeStruct((t, MLA_W), BF16)],
        grid=(t // tm,),
        in_specs=[row(KV_LORA), row(LANES), _layer_spec(lw["w_ke"], l), _layer_spec(lw["w_v"], l)],
        out_specs=[row(QK_W), row(MLA_W)],
        compiler_params=_cparams(("parallel",)),
        name="ctx_kv",
    )(ckv2d, kpe_lanes, lw["w_ke"], lw["w_v"])


def _attn_kernel(*refs, n_parts, n_heads):
    q_ref = refs[0]
    k_refs = refs[1:1 + n_parts]
    v_refs = refs[1 + n_parts:1 + 2 * n_parts]
    o_ref = refs[1 + 2 * n_parts]
    lane = lax.broadcasted_iota(jnp.int32, (1, 2 * V_HEAD), 1)
    own = [lane < V_HEAD, lane >= V_HEAD]
    heads = range(n_heads)
    hl = [slice(j * HEAD_PAD, (j + 1) * HEAD_PAD) for j in heads]
    pl_ = [slice((j // 2) * 2 * V_HEAD, (j // 2 + 1) * 2 * V_HEAD) for j in heads]
    s = [[_dot_nt(q_ref[:, hl[j]], k_ref[:, hl[j]]) for k_ref in k_refs] for j in heads]
    m = []
    for j in heads:
        mj = s[j][0].max(axis=-1, keepdims=True)
        for sp in s[j][1:]:
            mj = jnp.maximum(mj, sp.max(axis=-1, keepdims=True))
        m.append(mj)
    p = [[jnp.exp(sp - m[j]).astype(BF16) for sp in s[j]] for j in heads]
    acc = []
    for j in heads:
        a = None
        for pj, v_ref in zip(p[j], v_refs):
            t = _dot(pj, jnp.where(own[j % 2], v_ref[:, pl_[j]], 1.0).astype(BF16))
            a = t if a is None else a + t
        acc.append(a)
    for j in range(0, n_heads, 2):
        l0 = acc[j][:, V_HEAD:V_HEAD + 1]
        l1 = acc[j + 1][:, 0:1]
        o_ref[:, pl_[j]] = jnp.where(own[0], acc[j] / l0, acc[j + 1] / l1).astype(o_ref.dtype)


def _attention(qf, kv_parts, n_batch, seq_q, tq, n_heads):
    n_parts = len(kv_parts)
    nq = seq_q // tq
    in_specs = [pl.BlockSpec((tq, n_heads * HEAD_PAD), lambda b, hp, qi: (b * nq + qi, hp))]
    in_specs += [pl.BlockSpec((sk, n_heads * HEAD_PAD), lambda b, hp, qi: (b, hp)) for _, _, sk in kv_parts]
    in_specs += [pl.BlockSpec((sk, n_heads * V_HEAD), lambda b, hp, qi: (b, hp)) for _, _, sk in kv_parts]
    return pl.pallas_call(
        functools.partial(_attn_kernel, n_parts=n_parts, n_heads=n_heads),
        out_shape=jax.ShapeDtypeStruct((n_batch * seq_q, MLA_W), BF16),
        grid=(n_batch, HEADS // n_heads, nq),
        in_specs=in_specs,
        out_specs=pl.BlockSpec((tq, n_heads * V_HEAD), lambda b, hp, qi: (b * nq + qi, hp)),
        compiler_params=_cparams(("parallel", "parallel", "arbitrary")),
        name="attention",
    )(qf, *[k for k, _, _ in kv_parts], *[v for _, v, _ in kv_parts])


PAIRS = HEADS // 2
PAIR_N = 2 * GDN_CHUNK


def _unit_tri_inverse(a_list, eye, blk16, blk32):
    diag = [jnp.where(blk16, a, 0.0) for a in a_list]
    powers = [[x.astype(BF16) for x in diag]]
    for _ in range(3):
        powers.append([_dot(x, x).astype(BF16) for x in powers[-1]])
    t = [eye - x for x in diag]
    for pw in powers[1:]:
        t = [x + _dot(x.astype(BF16), y) for x, y in zip(t, pw)]
    for off in ([jnp.where(blk32 & ~blk16, a, 0.0).astype(BF16) for a in a_list],
                [jnp.where(blk32, 0.0, a).astype(BF16) for a in a_list]):
        tb = [x.astype(BF16) for x in t]
        m = [_dot(x, y).astype(BF16) for x, y in zip(tb, off)]
        t = [x - _dot(y, z) for x, y, z in zip(t, m, tb)]
    return t


def _stack_pair(x, lane_lo):
    return jnp.concatenate([jnp.where(lane_lo, x, 0.0), jnp.where(lane_lo, 0.0, x)], axis=0)


def _pair_col(m, l0):
    return jnp.concatenate([m[:, l0:l0 + 1], m[:, l0 + 1:l0 + 2]], axis=0)


def _gdn_kernel(*refs, zero_init, n_chunks, n_elems):
    it = iter(refs)
    dir_in = [tuple(next(it) for _ in range(4)) for _ in range(2)]
    h0_ref = None if zero_init else next(it)
    o_refs = (next(it), next(it))
    hfin_ref = next(it)
    s_scr = next(it)
    c = pl.program_id(1)
    cs, n = GDN_CHUNK, PAIR_N

    @pl.when(c == 0)
    def _():
        if zero_init:
            s_scr[...] = jnp.zeros_like(s_scr)
        else:
            z = jnp.zeros((GDN_DK, GDN_DV), F32)
            for e in range(n_elems):
                for d in range(2):
                    for p in range(PAIRS):
                        top = jnp.concatenate([h0_ref[e, d, 2 * p], z], axis=1)
                        bot = jnp.concatenate([z, h0_ref[e, d, 2 * p + 1]], axis=1)
                        s_scr[e, d, p] = jnp.concatenate([top, bot], axis=0)

    ri = lax.broadcasted_iota(jnp.int32, (n, n), 0)
    ci = lax.broadcasted_iota(jnp.int32, (n, n), 1)
    same_head = (ri // cs) == (ci // cs)
    eye = (ri == ci).astype(F32)
    blk16 = (ri // 16) == (ci // 16)
    blk32 = (ri // 32) == (ci // 32)
    lane_lo = lax.broadcasted_iota(jnp.int32, (1, n), 1) < cs
    ti = lax.broadcasted_iota(jnp.int32, (cs, cs), 0)
    tj = lax.broadcasted_iota(jnp.int32, (cs, cs), 1)

    chains = [(e, d, p) for e in range(n_elems) for d in range(2) for p in range(PAIRS)]
    incl_d = [same_head & (ri >= ci), same_head & (ri <= ci)]
    strict_d = [same_head & (ri > ci), same_head & (ri < ci)]
    gates_d, gc_d, gct_d, glast_d = {}, {}, {}, {}
    for e in range(n_elems):
        for d in range(2):
            gates = dir_in[d][3][e]
            csum = ((ti >= tj) if d == 0 else (ti <= tj)).astype(BF16)
            ghi, gmid, glo = _split3(gates)
            gc_all = _dot(csum, ghi) + _dot(csum, gmid) + _dot(csum, glo)
            gates_d[e, d] = gates
            gc_d[e, d] = gc_all
            gct_d[e, d] = gc_all.T
            glast_d[e, d] = gc_all[cs - 1:cs, :] if d == 0 else gc_all[0:1, :]

    kst, qst, vst, gcol, bcol, glast, decay, egc = [], [], [], [], [], [], [], []
    for e, d, p in chains:
        q_ref, k_ref, v_ref, _ = dir_in[d]
        sl = slice(p * n, (p + 1) * n)
        lg = GATE_LANE + HEADS * d + 2 * p
        kst.append(_stack_pair(k_ref[e, :, sl], lane_lo))
        qst.append(_stack_pair(q_ref[e, :, sl], lane_lo))
        vst.append(_stack_pair(v_ref[e, :, sl], lane_lo))
        gc = _pair_col(gc_d[e, d], lg)
        gr = jnp.concatenate([gct_d[e, d][lg:lg + 1, :], gct_d[e, d][lg + 1:lg + 2, :]], axis=1)
        gcol.append(gc)
        bcol.append(_pair_col(gates_d[e, d], lg + 2 * HEADS))
        glast.append(jnp.concatenate([jnp.broadcast_to(glast_d[e, d][:, lg:lg + 1], (cs, 1)),
                                      jnp.broadcast_to(glast_d[e, d][:, lg + 1:lg + 2], (cs, 1))], axis=0))
        decay.append(jnp.exp(jnp.where(incl_d[d], gc - gr, -jnp.inf)))
        egc.append(jnp.exp(gc))
    nch = len(chains)
    kstb = [x.astype(BF16) for x in kst]
    gp = [_dot_nt(jnp.concatenate([kstb[i], qst[i].astype(BF16)], axis=0), kstb[i]) for i in range(nch)]
    a_mat = [jnp.where(strict_d[chains[i][1]], bcol[i] * gp[i][:n] * decay[i], 0.0) for i in range(nch)]
    attn = [jnp.where(incl_d[chains[i][1]], gp[i][n:] * decay[i], 0.0).astype(BF16) for i in range(nch)]
    rhs = [jnp.concatenate([vst[i] * bcol[i], kst[i] * (bcol[i] * egc[i])], axis=1).astype(BF16) for i in range(nch)]
    t_inv = _unit_tri_inverse(a_mat, eye, blk16, blk32)
    x = [_dot(t_inv[i].astype(BF16), rhs[i]) for i in range(nch)]
    s_old = [s_scr[e, d, p] for e, d, p in chains]
    sb = [s.astype(BF16) for s in s_old]
    ws = [_dot(jnp.concatenate([x[i][:, n:], qst[i] * egc[i]], axis=0).astype(BF16), sb[i]) for i in range(nch)]
    v_new = [(x[i][:, :n] - ws[i][:n]).astype(BF16) for i in range(nch)]
    o_st = [ws[i][n:] + _dot(attn[i], v_new[i]) for i in range(nch)]
    k_dec_t = [(kst[i] * jnp.exp(glast[i] - gcol[i])).T.astype(BF16) for i in range(nch)]
    for i, (e, d, p) in enumerate(chains):
        s_scr[e, d, p] = s_old[i] * jnp.exp(glast[i]) + _dot(k_dec_t[i], v_new[i])
    for e in range(n_elems):
        for d in range(2):
            o_pairs = [o_st[chains.index((e, d, p))] for p in range(PAIRS)]
            o_refs[d][e] = jnp.concatenate([o[:cs] + o[cs:] for o in o_pairs], axis=1)

    @pl.when(c == n_chunks - 1)
    def _():
        for e in range(n_elems):
            for d in range(2):
                for p in range(PAIRS):
                    s = s_scr[e, d, p]
                    hfin_ref[e, d, 2 * p] = s[:GDN_DK, :GDN_DV]
                    hfin_ref[e, d, 2 * p + 1] = s[GDN_DK:, GDN_DV:]


GDN_ELEMS = 4


def _gdn(qg, kg, vg, gb, h0, n_batch, seq):
    nc = seq // GDN_CHUNK
    ne = GDN_ELEMS
    assert n_batch % ne == 0
    zero_init = h0 is None
    fwd = lambda w: pl.BlockSpec((ne, GDN_CHUNK, w), lambda b, c: (b, c, 0))
    bwd = lambda w: pl.BlockSpec((ne, GDN_CHUNK, w), lambda b, c: (b, nc - 1 - c, 0))
    in_specs = [fwd(GDN_W), fwd(GDN_W), fwd(GDN_W), fwd(LANES), bwd(GDN_W), bwd(GDN_W), bwd(GDN_W), bwd(LANES)]
    seq3 = lambda a: a.reshape(n_batch, seq, a.shape[-1])
    args = [seq3(qg), seq3(kg), seq3(vg), seq3(gb)] * 2
    st_spec = pl.BlockSpec((ne, 2, HEADS, GDN_DK, GDN_DV), lambda b, c: (b, 0, 0, 0, 0))
    if not zero_init:
        in_specs.append(st_spec)
        args.append(h0)
    t = n_batch * seq
    o_f, o_b, h_fin = pl.pallas_call(
        functools.partial(_gdn_kernel, zero_init=zero_init, n_chunks=nc, n_elems=ne),
        out_shape=[jax.ShapeDtypeStruct((n_batch, seq, GDN_W), F32), jax.ShapeDtypeStruct((n_batch, seq, GDN_W), F32),
                   jax.ShapeDtypeStruct((n_batch, 2, HEADS, GDN_DK, GDN_DV), F32)],
        grid=(n_batch // ne, nc),
        in_specs=in_specs,
        out_specs=[fwd(GDN_W), bwd(GDN_W), st_spec],
        scratch_shapes=[pltpu.VMEM((ne, 2, PAIRS, PAIR_N, PAIR_N), F32)],
        compiler_params=_cparams(("parallel", "arbitrary")),
        name="gdn",
    )(*args)
    return o_f.reshape(t, GDN_W), o_b.reshape(t, GDN_W), h_fin


MERGE_TM = 512


def _merge_kernel(x_ref, mod_ref, oa_ref, gz_ref, conv_ref, cx_prev_ref, cx_next_ref, of_ref, ob_ref, mg_ref,
                  wpa_ref, wpb_ref, wpc_ref, wo_ref, cw_ref, gn_ref, ones_ref, fin_ref, o_ref, *,
                  seq, final_norm):
    edges = _seq_edges(x_ref.shape[0], seq, pl.program_id(0))
    o_a = oa_ref[...].astype(F32) * _silu(gz_ref[:, 0:MLA_W].astype(F32))
    pa = _dot(o_a.astype(BF16), wpa_ref[...])
    f32 = lambda ref, rows, c0: ref[rows, c0:c0 + CONV_W].astype(F32)
    every = slice(None)
    cx = f32(conv_ref, every, 0) * f32(conv_ref, every, CONV_W)
    pr, nr = slice(HALO - 1, HALO), slice(0, 1)
    prev_row = f32(cx_prev_ref, pr, 0) * f32(cx_prev_ref, pr, CONV_W)
    next_row = f32(cx_next_ref, nr, 0) * f32(cx_next_ref, nr, CONV_W)
    down, up = _shift_rows(cx, prev_row, next_row, edges)
    conv = down * cw_ref[0:1, :] + cx * cw_ref[1:2, :] + up * cw_ref[2:3, :]
    o_b = f32(conv_ref, every, 2 * CONV_W) * conv * _silu(f32(conv_ref, every, 3 * CONV_W))
    pb = _dot(o_b.astype(BF16), wpb_ref[...])
    og = of_ref[...] + ob_ref[...]
    ms = _head_sums(og * og, ones_ref[...]) * (1.0 / GDN_DV)
    o_c = og * lax.rsqrt(ms + EPS) * gn_ref[...] * _silu(gz_ref[:, MLA_W:W_GZ].astype(F32))
    pc = _dot(o_c.astype(BF16), wpc_ref[...])
    m = (_sigmoid(mg_ref[:, 0:D_MODEL].astype(F32)) * pa + _sigmoid(mg_ref[:, D_MODEL:2 * D_MODEL].astype(F32)) * pb
         + _sigmoid(mg_ref[:, 2 * D_MODEL:3 * D_MODEL].astype(F32)) * pc)
    r = _dot(m.astype(BF16), wo_ref[...])
    y = x_ref[...] + mod_ref[:, 2 * D_MODEL:3 * D_MODEL] * r
    if final_norm:
        y = _rms(y) * fin_ref[...]
    o_ref[...] = y


def _merge(x2d, mod, o_a, u_gz, u_conv, o_f, o_b, u_mg, lw, l, final_g, seq, cond_row0, cond_per_seq, final_norm):
    t = x2d.shape[0]
    tm = MERGE_TM
    assert t % tm == 0 and (seq % tm == 0 or tm % seq == 0) and (cond_per_seq == 0 or seq % tm == 0)
    tps = max(seq // tm, 1)
    row = lambda w: pl.BlockSpec((tm, w), lambda i: (i, 0))
    prev_spec, next_spec = _halo_specs(tm, 2 * CONV_W, t)
    names = ["w_pa", "w_pb", "w_pc", "w_o", "conv_b_w", "gdn_norm_lane"]
    return pl.pallas_call(
        functools.partial(_merge_kernel, seq=seq, final_norm=final_norm),
        out_shape=jax.ShapeDtypeStruct((t, D_MODEL), F32),
        grid=(t // tm,),
        in_specs=[
            row(D_MODEL),
            _mod_spec(l, lambda i: cond_row0 + (i // tps) * cond_per_seq),
            row(MLA_W), row(W_GZ), row(W_CONV), prev_spec, next_spec, row(GDN_W), row(GDN_W), row(W_MG),
        ] + [_layer_spec(lw[k], l) for k in names] + [_const_spec((GDN_W, GDN_W)), _const_spec((1, D_MODEL))],
        out_specs=row(D_MODEL),
        compiler_params=_cparams(("parallel",)),
        name="merge",
    )(x2d, mod, o_a, u_gz, u_conv, u_conv, u_conv, o_f, o_b, u_mg, *[lw[k] for k in names], lw["ones_bd"],
      final_g.reshape(1, D_MODEL))


_EVEN_ODD = list(range(0, QK_ROPE, 2)) + list(range(1, QK_ROPE, 2))
_SWAPPED = list(range(1, QK_ROPE, 2)) + list(range(0, QK_ROPE, 2))


REPACK_TM = 256
_PACK_PIECES = ((_O_CQ, _O_KPE), None, (_O_AB, _O_MG), (_O_KPE, _O_GA),
                (_O_Q, _O_Z), (_O_C, _O_GB), (_O_B, _O_C), (_O_GB, _O_Q), (_O_GA, _O_B), (_O_Z, _O_AB),
                (_O_MG, _O_END))


def _repack_kernel(w_ref, kpe_ref, o_ref):
    off = 0
    for piece in _PACK_PIECES:
        x = kpe_ref[...] if piece is None else w_ref[:, piece[0]:piece[1]]
        o_ref[:, off:off + x.shape[1]] = x.astype(BF16)
        off += x.shape[1]


def _repack_w_in(w_in):
    depth = w_in.shape[0]
    kpe = w_in[..., _O_KPE:_O_GA]
    kpe4 = jnp.concatenate([kpe[..., 0::2], kpe[..., 1::2], kpe[..., 1::2], kpe[..., 0::2]], axis=-1)
    return pl.pallas_call(
        _repack_kernel,
        out_shape=jax.ShapeDtypeStruct((depth, D_MODEL, W_IN_PACKED), BF16),
        grid=(depth, D_MODEL // REPACK_TM),
        in_specs=[pl.BlockSpec((None, REPACK_TM, _O_END), lambda l, i: (l, i, 0)),
                  pl.BlockSpec((None, REPACK_TM, 2 * QK_ROPE), lambda l, i: (l, i, 0))],
        out_specs=pl.BlockSpec((None, REPACK_TM, W_IN_PACKED), lambda l, i: (l, i, 0)),
        compiler_params=_cparams(("parallel", "parallel")),
        name="repack_w_in",
    )(w_in, kpe4)


def _pack_weights(w_in, q_norm_g, kv_norm_g, w_uq, w_ukv, conv_b_w, conv_qkv_w, a_log, dt_bias, gdn_norm_g,
                  w_pa, w_pb, w_pc, w_o, norm_g):
    depth = w_in.shape[0]
    eo = np.array(_EVEN_ODD)
    sw = np.array(_SWAPPED)
    w_in_p = _repack_w_in(w_in)
    wq = w_uq.astype(BF16).reshape(depth, Q_LORA, HEADS, QK_NOPE + QK_ROPE)
    nope, rope = wq[..., :QK_NOPE], wq[..., QK_NOPE:]
    z32 = jnp.zeros((depth, Q_LORA, HEADS, HEAD_PAD - QK_NOPE - QK_ROPE), BF16)
    z64 = jnp.zeros((depth, Q_LORA, HEADS, QK_NOPE), BF16)
    w_qa = jnp.concatenate([nope, rope[..., eo], z32], axis=-1).reshape(depth, Q_LORA, QK_W)
    w_qb = jnp.concatenate([z64, rope[..., sw], z32], axis=-1).reshape(depth, Q_LORA, QK_W)
    wkv = w_ukv.astype(BF16).reshape(depth, KV_LORA, HEADS, QK_NOPE + V_HEAD)
    w_k = jnp.concatenate([wkv[..., :QK_NOPE], jnp.zeros((depth, KV_LORA, HEADS, HEAD_PAD - QK_NOPE), BF16)],
                          axis=-1).reshape(depth, KV_LORA, QK_W)
    w_v = wkv[..., QK_NOPE:].reshape(depth, KV_LORA, MLA_W)
    src = np.arange(LANES)[:, None]
    dst = np.arange(QK_W)[None, :] % HEAD_PAD - QK_NOPE
    in_rope = (dst >= 0) & (dst < QK_ROPE)
    e_plain = jnp.asarray(np.broadcast_to(in_rope & (src == dst), (depth, LANES, QK_W)), BF16)
    e_rope = jnp.asarray(np.broadcast_to(in_rope & ((src == dst) | (src == dst + QK_ROPE)), (depth, LANES, QK_W)), BF16)
    gate_pad = ((0, 0), (0, 0), (GATE_LANE, LANES - GATE_LANE - 2 * HEADS))
    hid = np.arange(GDN_W) // GDN_DK
    return {
        "w_in_p": w_in_p, "norm_g": norm_g.reshape(depth, 1, D_MODEL),
        "q_norm_g": q_norm_g.reshape(depth, 1, Q_LORA), "kv_norm_g": kv_norm_g.reshape(depth, 1, KV_LORA),
        "w_qa": w_qa, "w_qb": w_qb,
        "w_ke": jnp.concatenate([w_k, e_plain], axis=1), "w_ke_rope": jnp.concatenate([w_k, e_rope], axis=1),
        "w_v": w_v,
        "conv_qkv_w": conv_qkv_w, "conv_b_w": conv_b_w,
        "alog_lane": jnp.pad(a_log.reshape(depth, 1, 2 * HEADS), gate_pad),
        "dtb_lane": jnp.pad(dt_bias.reshape(depth, 1, 2 * HEADS), gate_pad),
        "ones_bd": jnp.asarray(hid[:, None] == hid[None, :], BF16),
        "gdn_norm_lane": jnp.tile(gdn_norm_g, (1, HEADS)).reshape(depth, 1, GDN_W),
        "w_pa": w_pa.astype(BF16), "w_pb": w_pb.astype(BF16), "w_pc": w_pc.astype(BF16), "w_o": w_o.astype(BF16),
    }


def _rope_tables(n_tokens):
    t = np.arange(n_tokens)
    row = (t // GRID_W).astype(np.float32)
    col = (t % GRID_W).astype(np.float32)
    n_freq = QK_ROPE // 4
    inv_freq = (np.float32(ROPE_THETA) ** (-np.arange(n_freq, dtype=np.float32) / n_freq)).astype(np.float32)
    ang = np.concatenate([row[:, None] * inv_freq, col[:, None] * inv_freq], axis=-1).astype(np.float32)
    cos, sin = np.cos(ang), np.sin(ang)
    cc = np.concatenate([cos, cos], axis=-1)
    ss = np.concatenate([-sin, sin], axis=-1)
    one = np.ones((n_tokens, QK_NOPE), np.float32)
    z32 = np.zeros((n_tokens, QK_ROPE), np.float32)
    z64 = np.zeros((n_tokens, QK_NOPE), np.float32)
    m1 = np.concatenate([one, cc, z32], axis=-1) * np.float32(SM_SCALE)
    m2 = np.concatenate([z64, ss, z32], axis=-1) * np.float32(SM_SCALE)
    m3 = np.concatenate([cc, ss, z64], axis=-1)
    return tuple(jnp.asarray(m, F32) for m in (m1, m2, m3))


def _layer(x2d, mod, lw, l, final_g, n_batch, seq, cond_row0, cond_per_seq, final_norm, rope_tabs=None,
           ctx_kv=None, h0=None, tq=256, attn_heads=HEADS):
    is_ctx = ctx_kv is None
    u_mla, u_qkv, u_conv, u_gz, u_mg = _inproj(x2d, mod, lw, l, seq, cond_row0, cond_per_seq)
    outs = _prep(u_mla, u_qkv, lw, l, seq, rope_tabs, emit_ckvn=is_ctx)
    if is_ctx:
        qf, kf, vf, ckvn, qg, kg, vg, gb = outs
        kv_parts = [(kf, vf, seq)]
    else:
        qf, kf, vf, qg, kg, vg, gb = outs
        ckvn = None
        kv_parts = [ctx_kv, (kf, vf, seq)]
    o_a = _attention(qf, kv_parts, n_batch, seq, tq, attn_heads)
    o_f, o_b, h_fin = _gdn(qg, kg, vg, gb, h0, n_batch, seq)
    y = _merge(x2d, mod, o_a, u_gz, u_conv, o_f, o_b, u_mg, lw, l, final_g, seq, cond_row0, cond_per_seq, final_norm)
    return y, ckvn, u_mla[:, W_MLA - QK_ROPE:], h_fin


def kernel(x_prompt, x_sample, c, cache_ckv, cache_kpe, state_gdn, c_ctx, norm_g, w_ada, b_ada, w_in, q_norm_g,
           kv_norm_g, w_uq, w_ukv, conv_b_w, conv_qkv_w, a_log, dt_bias, gdn_norm_g, w_pa, w_pb, w_pc, w_o,
           final_norm_g):
    n_ctx, seq_ctx, _ = x_prompt.shape
    n_lat, seq_lat, _ = x_sample.shape
    past = cache_ckv.shape[2]
    depth = w_in.shape[0]
    assert n_lat + 1 <= COND_ROWS

    cond = jnp.concatenate([c, c_ctx[None, :], jnp.zeros((COND_ROWS - n_lat - 1, D_MODEL), F32)], axis=0)
    mod = _ada_mod(cond, w_ada, b_ada).reshape(depth, COND_ROWS, 1, 3 * D_MODEL)
    rope_tabs = _rope_tables(seq_lat)
    lw = _pack_weights(w_in, q_norm_g, kv_norm_g, w_uq, w_ukv, conv_b_w, conv_qkv_w, a_log, dt_bias, gdn_norm_g,
                       w_pa, w_pb, w_pc, w_o, norm_g)
    kpe_ctx = jnp.pad(cache_kpe[..., np.array(_EVEN_ODD)], ((0, 0),) * 3 + ((0, LANES - QK_ROPE),))

    xp = x_prompt.reshape(n_ctx * seq_ctx, D_MODEL)
    xs = x_sample.reshape(n_lat * seq_lat, D_MODEL)
    ckv_out, kpe_out, st_out = [], [], []
    for l in range(depth):
        last = l == depth - 1
        xp, ckvn, kpe, h_fin = _layer(xp, mod, lw, l, final_norm_g, n_ctx, seq_ctx, n_lat, 0, last)
        ckv_out.append(ckvn.reshape(n_ctx, seq_ctx, KV_LORA))
        kpe_out.append(kpe.reshape(n_ctx, seq_ctx, QK_ROPE))
        st_out.append(h_fin)
        kc, vc = _ctx_kv(cache_ckv[:, l].reshape(n_lat * past, KV_LORA), kpe_ctx[:, l].reshape(n_lat * past, LANES),
                         lw, l)
        xs, _, _, _ = _layer(xs, mod, lw, l, final_norm_g, n_lat, seq_lat, 0, 1, last, rope_tabs=rope_tabs,
                             ctx_kv=(kc, vc, past), h0=state_gdn[:, l], tq=512, attn_heads=4)
    return (xp.reshape(n_ctx, seq_ctx, D_MODEL), xs.reshape(n_lat, seq_lat, D_MODEL),
            jnp.stack(ckv_out, axis=1), jnp.stack(kpe_out, axis=1), jnp.stack(st_out, axis=1))
```

```python
import functools
import math

import jax
import jax.numpy as jnp
import numpy as np
from jax import lax
from jax.experimental import pallas as pl
from jax.experimental.pallas import tpu as pltpu

F32 = jnp.float32
BF16 = jnp.bfloat16

D_MODEL = 1024
DEPTH = 2
GRID_W = 64
EPS = 1e-6
HEADS = 8
Q_LORA = 384
KV_LORA = 256
QK_NOPE = 64
QK_ROPE = 32
V_HEAD = 64
MLA_W = HEADS * V_HEAD
ROPE_THETA = 10000.0
CONV_W = 512
GDN_DK = 64
GDN_DV = 64
GDN_W = HEADS * GDN_DK
GDN_CHUNK = 64
LANES = 128
HEAD_PAD = 128
QK_W = HEADS * HEAD_PAD
SM_SCALE = (QK_NOPE + QK_ROPE) ** -0.5
COND_ROWS = 16
VMEM_LIMIT = 56 * 1024 * 1024

_O_CQ, _O_CKV, _O_KPE, _O_GA = 0, 384, 640, 672
_O_B, _O_C, _O_X, _O_GB = 1184, 1696, 2208, 2720
_O_Q, _O_K, _O_V, _O_Z = 3232, 3744, 4256, 4768
_O_AB, _O_MG, _O_END = 5280, 5312, 8384
W_MLA = Q_LORA + KV_LORA + LANES
W_QKV = 3 * GDN_W
W_CONV = 4 * CONV_W
W_GZ = MLA_W + GDN_W
W_MG = 3 * D_MODEL
GROUP_WIDTHS = (W_MLA, W_QKV, W_CONV, W_GZ, W_MG)
GROUP_DTYPES = (F32, BF16, BF16, BF16, BF16)
W_IN_PACKED = sum(GROUP_WIDTHS)
GATE_LANE = 64


def _sigmoid(x):
    return 1.0 / (1.0 + jnp.exp(-x))


def _silu(x):
    return x * _sigmoid(x)


def _rms(x):
    return x * lax.rsqrt(jnp.mean(x * x, axis=-1, keepdims=True) + EPS)


def _dot(a, b):
    return jnp.dot(a, b, preferred_element_type=F32)


def _dot_nt(a, b):
    return lax.dot_general(a, b, (((1,), (1,)), ((), ())), preferred_element_type=F32)


def _split3(x):
    hi = x.astype(BF16)
    r = x - hi.astype(F32)
    mid = r.astype(BF16)
    lo = (r - mid.astype(F32)).astype(BF16)
    return hi, mid, lo


def _head_sums(x2, ones_bd):
    hi = x2.astype(BF16)
    lo = (x2 - hi.astype(F32)).astype(BF16)
    return _dot(hi, ones_bd) + _dot(lo, ones_bd)


def _cparams(sem):
    return pltpu.CompilerParams(dimension_semantics=sem, vmem_limit_bytes=VMEM_LIMIT)


def _const_spec(shape):
    nd = len(shape)
    return pl.BlockSpec(shape, lambda *_: (0,) * nd)


def _mod_spec(l, cond_row):
    return pl.BlockSpec((None, None, 1, 3 * D_MODEL), lambda i: (l, cond_row(i), 0, 0))


def _layer_spec(arr, l, **kw):
    nd = arr.ndim - 1
    return pl.BlockSpec((None,) + arr.shape[1:], lambda *_: (l,) + (0,) * nd, **kw)


def _ada_kernel(c_ref, w_ref, b_ref, o_ref):
    sc = _silu(c_ref[...])
    o_ref[...] = jnp.dot(sc, w_ref[...], preferred_element_type=F32, precision=lax.Precision.HIGHEST) + b_ref[...]


def _ada_mod(cond, w_ada, b_ada):
    depth = w_ada.shape[0]
    return pl.pallas_call(
        _ada_kernel,
        out_shape=jax.ShapeDtypeStruct((depth, COND_ROWS, 3 * D_MODEL), F32),
        grid=(depth, 3),
        in_specs=[
            pl.BlockSpec((COND_ROWS, D_MODEL), lambda l, j: (0, 0)),
            pl.BlockSpec((None, D_MODEL, D_MODEL), lambda l, j: (l, 0, j)),
            pl.BlockSpec((None, 1, D_MODEL), lambda l, j: (l, 0, j)),
        ],
        out_specs=pl.BlockSpec((None, COND_ROWS, D_MODEL), lambda l, j: (l, 0, j)),
        compiler_params=_cparams(("arbitrary", "arbitrary")),
        name="ada_mod",
    )(cond, w_ada, b_ada.reshape(depth, 1, 3 * D_MODEL))


IN_TM = 512
IN_CHUNK = 768


def _inproj_kernel(x_ref, mod_ref, g_ref, w_ref, *rest):
    outs, h_scr = rest[:-1], rest[-1]
    x = x_ref[...]
    shift = mod_ref[:, 0:D_MODEL]
    scale = mod_ref[:, D_MODEL:2 * D_MODEL]
    h = (_rms(x) * g_ref[...]) * (1.0 + scale) + shift
    h_scr[...] = h.astype(BF16)
    off = 0
    for o_ref in outs:
        n = o_ref.shape[1]
        for c0 in range(0, n, IN_CHUNK):
            c1 = min(c0 + IN_CHUNK, n)
            o_ref[:, c0:c1] = _dot(h_scr[...], w_ref[:, off + c0:off + c1]).astype(o_ref.dtype)
        off += n


def _inproj(x2d, mod, lw, l, seq, cond_row0, cond_per_seq):
    t = x2d.shape[0]
    assert t % IN_TM == 0 and (cond_per_seq == 0 or seq % IN_TM == 0)
    tps = max(seq // IN_TM, 1)
    return pl.pallas_call(
        _inproj_kernel,
        out_shape=[jax.ShapeDtypeStruct((t, n), dt) for n, dt in zip(GROUP_WIDTHS, GROUP_DTYPES)],
        grid=(t // IN_TM,),
        in_specs=[
            pl.BlockSpec((IN_TM, D_MODEL), lambda i: (i, 0)),
            _mod_spec(l, lambda i: cond_row0 + (i // tps) * cond_per_seq),
            _layer_spec(lw["norm_g"], l),
            _layer_spec(lw["w_in_p"], l, pipeline_mode=pl.Buffered(1)),
        ],
        out_specs=[pl.BlockSpec((IN_TM, n), lambda i: (i, 0)) for n in GROUP_WIDTHS],
        scratch_shapes=[pltpu.VMEM((IN_TM, D_MODEL), BF16)],
        compiler_params=_cparams(("parallel",)),
        name="inproj",
    )(x2d, mod, lw["norm_g"], lw["w_in_p"])


PREP_TM = 512


def _seq_edges(tm, seq, i):
    row = lax.broadcasted_iota(jnp.int32, (tm, 1), 0)
    if seq >= tm:
        tps = seq // tm
        return (row == 0) & ((i % tps) == 0), (row == tm - 1) & ((i % tps) == tps - 1)
    pos = lax.rem(row, seq)
    return pos == 0, pos == seq - 1


def _shift_rows(x, prev_row, next_row, edges):
    n = x.shape[0]
    row = lax.broadcasted_iota(jnp.int32, (n, 1), 0)
    starts, ends = edges
    down = jnp.where(starts, 0.0, jnp.where(row == 0, prev_row, pltpu.roll(x, 1, 0)))
    up = jnp.where(ends, 0.0, jnp.where(row == n - 1, next_row, pltpu.roll(x, n - 1, 0)))
    return down, up


def _prep_kernel(*refs, use_rope, seq, emit_ckvn):
    it = iter(refs)
    mla_ref, qkv_ref, qkv_prev_ref, qkv_next_ref = next(it), next(it), next(it), next(it)
    if use_rope:
        m1_ref, m2_ref, m3_ref = next(it), next(it), next(it)
    gq_ref, gkv_ref, wa_ref = next(it), next(it), next(it)
    if use_rope:
        wb_ref = next(it)
    wke_ref, wv_ref, cw_ref, alog_ref, dtb_ref, ones_ref = (next(it) for _ in range(6))
    if emit_ckvn:
        next(it), next(it)
    qf_ref, kf_ref, vf_ref = next(it), next(it), next(it)
    if emit_ckvn:
        ckvn_ref, kpe_ref = next(it), next(it)
    qg_ref, kg_ref, vg_ref, gb_ref = next(it), next(it), next(it), next(it)

    edges = _seq_edges(mla_ref.shape[0], seq, pl.program_id(0))

    cq = mla_ref[:, 0:Q_LORA]
    ckv = mla_ref[:, Q_LORA:Q_LORA + KV_LORA]
    small = mla_ref[:, Q_LORA + KV_LORA:W_MLA]
    qn = (_rms(cq) * gq_ref[...]).astype(BF16)
    qa = _dot(qn, wa_ref[...])
    if use_rope:
        qb = _dot(qn, wb_ref[...])
        m1 = m1_ref[...]
        m2 = m2_ref[...]
        for h in range(HEADS):
            sl = slice(h * HEAD_PAD, (h + 1) * HEAD_PAD)
            qf_ref[:, sl] = (qa[:, sl] * m1 + qb[:, sl] * m2).astype(BF16)
        kpe_t = small * m3_ref[...]
    else:
        qf_ref[...] = (qa * SM_SCALE).astype(BF16)
        kpe_t = small
    ckvn = _rms(ckv) * gkv_ref[...]
    if emit_ckvn:
        ckvn_ref[...] = ckvn.reshape(ckvn_ref.shape)
        kpe_ref[...] = small[:, LANES - QK_ROPE:].reshape(kpe_ref.shape)
    ckvn_b = ckvn.astype(BF16)
    kin = jnp.concatenate([ckvn_b, kpe_t.astype(BF16)], axis=1)
    kf_ref[...] = _dot(kin, wke_ref[...]).astype(BF16)
    vf_ref[...] = _dot(ckvn_b, wv_ref[...]).astype(BF16)

    z1 = small + dtb_ref[...]
    sp = jnp.maximum(z1, 0.0) + jnp.log1p(jnp.exp(-jnp.abs(z1)))
    g = -jnp.exp(alog_ref[...]) * sp
    lane = lax.broadcasted_iota(jnp.int32, (1, LANES), 1)
    gb_ref[...] = jnp.where(lane < GATE_LANE + 2 * HEADS, g, _sigmoid(small))

    ones_bd = ones_ref[...]
    for part, o_ref in enumerate((qg_ref, kg_ref, vg_ref)):
        sl = slice(part * GDN_W, (part + 1) * GDN_W)
        x = qkv_ref[:, sl].astype(F32)
        prev_row = qkv_prev_ref[HALO - 1:HALO, sl].astype(F32)
        next_row = qkv_next_ref[0:1, sl].astype(F32)
        down, up = _shift_rows(x, prev_row, next_row, edges)
        y = down * cw_ref[0:1, sl] + x * cw_ref[1:2, sl] + up * cw_ref[2:3, sl]
        y = _silu(y)
        if part == 0:
            y = y * lax.rsqrt(_head_sums(y * y, ones_bd) + EPS) * (GDN_DK ** -0.5)
        elif part == 1:
            y = y * lax.rsqrt(_head_sums(y * y, ones_bd) + EPS)
        o_ref[...] = y


HALO = 16


def _halo_specs(tm, width, n_rows, col_block=0):
    r = tm // HALO
    nb = n_rows // HALO
    prev = pl.BlockSpec((HALO, width), lambda i: (jnp.maximum(i * r - 1, 0), col_block))
    nxt = pl.BlockSpec((HALO, width), lambda i: (jnp.minimum((i + 1) * r, nb - 1), col_block))
    return prev, nxt


def _stacked_spec(tm, seq, width, l):
    if tm >= seq:
        return pl.BlockSpec((tm // seq, None, seq, width), lambda i: (i, l, 0, 0))
    tps = seq // tm
    return pl.BlockSpec((None, None, tm, width), lambda i: (i // tps, l, i % tps, 0))


def _prep(u_mla, u_qkv, lw, l, seq, rope_tabs, caches):
    t = u_mla.shape[0]
    tm = PREP_TM
    use_rope = rope_tabs is not None
    emit_ckvn = caches is not None
    assert t % tm == 0 and (seq % tm == 0 or tm % seq == 0) and (not use_rope or seq % tm == 0)
    tps = max(seq // tm, 1)
    row = lambda w: pl.BlockSpec((tm, w), lambda i: (i, 0))
    prev_spec, next_spec = _halo_specs(tm, W_QKV, t)
    in_specs = [row(W_MLA), row(W_QKV), prev_spec, next_spec]
    args = [u_mla, u_qkv, u_qkv, u_qkv]
    if use_rope:
        in_specs += [pl.BlockSpec((tm, LANES), lambda i: (i % tps, 0))] * 3
        args += list(rope_tabs)
    names = ["q_norm_g", "kv_norm_g", "w_qa"] + (["w_qb"] if use_rope else [])
    names += ["w_ke_rope" if use_rope else "w_ke", "w_v", "conv_qkv_w", "alog_lane", "dtb_lane"]
    in_specs += [_layer_spec(lw[k], l) for k in names] + [_const_spec((GDN_W, GDN_W))]
    args += [lw[k] for k in names] + [lw["ones_bd"]]
    out_shape = [jax.ShapeDtypeStruct((t, QK_W), BF16), jax.ShapeDtypeStruct((t, QK_W), BF16),
                 jax.ShapeDtypeStruct((t, MLA_W), BF16)]
    out_specs = [row(QK_W), row(QK_W), row(MLA_W)]
    aliases = {}
    if emit_ckvn:
        for a in caches:
            aliases[len(args)] = len(out_shape)
            in_specs.append(pl.BlockSpec(memory_space=pl.ANY))
            args.append(a)
            out_shape.append(jax.ShapeDtypeStruct(a.shape, a.dtype))
            out_specs.append(_stacked_spec(tm, seq, a.shape[-1], l))
    out_shape += [jax.ShapeDtypeStruct((t, GDN_W), F32)] * 3 + [jax.ShapeDtypeStruct((t, LANES), F32)]
    out_specs += [row(GDN_W)] * 3 + [row(LANES)]
    outs = pl.pallas_call(
        functools.partial(_prep_kernel, use_rope=use_rope, seq=seq, emit_ckvn=emit_ckvn),
        out_shape=out_shape,
        grid=(t // tm,),
        in_specs=in_specs,
        out_specs=out_specs,
        input_output_aliases=aliases,
        compiler_params=_cparams(("parallel",)),
        name="prep",
    )(*args)
    return outs


def _ctxkv_kernel(ckv_ref, kpe_ref, wke_ref, wv_ref, kf_ref, vf_ref):
    ckv_b = ckv_ref[...].astype(BF16)
    kin = jnp.concatenate([ckv_b, kpe_ref[...].astype(BF16)], axis=1)
    kf_ref[...] = _dot(kin, wke_ref[...]).astype(BF16)
    vf_ref[...] = _dot(ckv_b, wv_ref[...]).astype(BF16)


def _ctx_kv(cache_ckv, kpe_lanes, lw, l):
    n_batch, _, past, _ = cache_ckv.shape
    cached = lambda w: pl.BlockSpec((None, None, past, w), lambda i: (i, l, 0, 0))
    row = lambda w: pl.BlockSpec((past, w), lambda i: (i, 0))
    return pl.pallas_call(
        _ctxkv_kernel,
        out_shape=[jax.ShapeDtypeStruct((n_batch * past, QK_W), BF16), jax.ShapeDtypeStruct((n_batch * past, MLA_W), BF16)],
        grid=(n_batch,),
        in_specs=[cached(KV_LORA), cached(LANES), _layer_spec(lw["w_ke"], l), _layer_spec(lw["w_v"], l)],
        out_specs=[row(QK_W), row(MLA_W)],
        compiler_params=_cparams(("parallel",)),
        name="ctx_kv",
    )(cache_ckv, kpe_lanes, lw["w_ke"], lw["w_v"])


def _attn_kernel(*refs, n_parts, n_heads):
    q_ref = refs[0]
    k_refs = refs[1:1 + n_parts]
    v_refs = refs[1 + n_parts:1 + 2 * n_parts]
    o_ref = refs[1 + 2 * n_parts]
    lane = lax.broadcasted_iota(jnp.int32, (1, 2 * V_HEAD), 1)
    own = [lane < V_HEAD, lane >= V_HEAD]
    heads = range(n_heads)
    hl = [slice(j * HEAD_PAD, (j + 1) * HEAD_PAD) for j in heads]
    pl_ = [slice((j // 2) * 2 * V_HEAD, (j // 2 + 1) * 2 * V_HEAD) for j in heads]
    s = [[_dot_nt(q_ref[:, hl[j]], k_ref[:, hl[j]]) for k_ref in k_refs] for j in heads]
    m = []
    for j in heads:
        mj = s[j][0].max(axis=-1, keepdims=True)
        for sp in s[j][1:]:
            mj = jnp.maximum(mj, sp.max(axis=-1, keepdims=True))
        m.append(mj)
    p = [[jnp.exp(sp - m[j]).astype(BF16) for sp in s[j]] for j in heads]
    acc = []
    for j in heads:
        a = None
        for pj, v_ref in zip(p[j], v_refs):
            t = _dot(pj, jnp.where(own[j % 2], v_ref[:, pl_[j]], 1.0).astype(BF16))
            a = t if a is None else a + t
        acc.append(a)
    for j in range(0, n_heads, 2):
        l0 = acc[j][:, V_HEAD:V_HEAD + 1]
        l1 = acc[j + 1][:, 0:1]
        o_ref[:, pl_[j]] = jnp.where(own[0], acc[j] / l0, acc[j + 1] / l1).astype(o_ref.dtype)


def _attention(qf, kv_parts, n_batch, seq_q, tq, n_heads):
    n_parts = len(kv_parts)
    nq = seq_q // tq
    in_specs = [pl.BlockSpec((tq, n_heads * HEAD_PAD), lambda b, hp, qi: (b * nq + qi, hp))]
    in_specs += [pl.BlockSpec((sk, n_heads * HEAD_PAD), lambda b, hp, qi: (b, hp)) for _, _, sk in kv_parts]
    in_specs += [pl.BlockSpec((sk, n_heads * V_HEAD), lambda b, hp, qi: (b, hp)) for _, _, sk in kv_parts]
    return pl.pallas_call(
        functools.partial(_attn_kernel, n_parts=n_parts, n_heads=n_heads),
        out_shape=jax.ShapeDtypeStruct((n_batch * seq_q, MLA_W), BF16),
        grid=(n_batch, HEADS // n_heads, nq),
        in_specs=in_specs,
        out_specs=pl.BlockSpec((tq, n_heads * V_HEAD), lambda b, hp, qi: (b * nq + qi, hp)),
        compiler_params=_cparams(("parallel", "parallel", "arbitrary")),
        name="attention",
    )(qf, *[k for k, _, _ in kv_parts], *[v for _, v, _ in kv_parts])


PAIRS = HEADS // 2
PAIR_N = 2 * GDN_CHUNK


def _unit_tri_inverse(a_list, eye, blk16, blk32):
    diag = [jnp.where(blk16, a, 0.0) for a in a_list]
    powers = [[x.astype(BF16) for x in diag]]
    for _ in range(3):
        powers.append([_dot(x, x).astype(BF16) for x in powers[-1]])
    t = [eye - x for x in diag]
    for pw in powers[1:]:
        t = [x + _dot(x.astype(BF16), y) for x, y in zip(t, pw)]
    for off in ([jnp.where(blk32 & ~blk16, a, 0.0).astype(BF16) for a in a_list],
                [jnp.where(blk32, 0.0, a).astype(BF16) for a in a_list]):
        tb = [x.astype(BF16) for x in t]
        m = [_dot(x, y).astype(BF16) for x, y in zip(tb, off)]
        t = [x - _dot(y, z) for x, y, z in zip(t, m, tb)]
    return t


def _stack_pair(x, lane_lo):
    return jnp.concatenate([jnp.where(lane_lo, x, 0.0), jnp.where(lane_lo, 0.0, x)], axis=0)


def _pair_col(m, l0):
    return jnp.concatenate([m[:, l0:l0 + 1], m[:, l0 + 1:l0 + 2]], axis=0)


def _gdn_kernel(*refs, zero_init, emit_state, n_chunks, n_elems):
    it = iter(refs)
    dir_in = [tuple(next(it) for _ in range(4)) for _ in range(2)]
    h0_ref = None if zero_init else next(it)
    if emit_state:
        next(it)
    o_refs = (next(it), next(it))
    hfin_ref = next(it) if emit_state else None
    s_scr = next(it)
    c = pl.program_id(1)
    cs, n = GDN_CHUNK, PAIR_N

    @pl.when(c == 0)
    def _():
        if zero_init:
            s_scr[...] = jnp.zeros_like(s_scr)
        else:
            z = jnp.zeros((GDN_DK, GDN_DV), F32)
            for e in range(n_elems):
                for d in range(2):
                    for p in range(PAIRS):
                        top = jnp.concatenate([h0_ref[e, d, 2 * p], z], axis=1)
                        bot = jnp.concatenate([z, h0_ref[e, d, 2 * p + 1]], axis=1)
                        s_scr[e, d, p] = jnp.concatenate([top, bot], axis=0)

    ri = lax.broadcasted_iota(jnp.int32, (n, n), 0)
    ci = lax.broadcasted_iota(jnp.int32, (n, n), 1)
    same_head = (ri // cs) == (ci // cs)
    eye = (ri == ci).astype(F32)
    blk16 = (ri // 16) == (ci // 16)
    blk32 = (ri // 32) == (ci // 32)
    lane_lo = lax.broadcasted_iota(jnp.int32, (1, n), 1) < cs
    ti = lax.broadcasted_iota(jnp.int32, (cs, cs), 0)
    tj = lax.broadcasted_iota(jnp.int32, (cs, cs), 1)

    chains = [(e, d, p) for e in range(n_elems) for d in range(2) for p in range(PAIRS)]
    incl_d = [same_head & (ri >= ci), same_head & (ri <= ci)]
    strict_d = [same_head & (ri > ci), same_head & (ri < ci)]
    gates_d, gc_d, gct_d, glast_d = {}, {}, {}, {}
    for e in range(n_elems):
        for d in range(2):
            gates = dir_in[d][3][e]
            csum = ((ti >= tj) if d == 0 else (ti <= tj)).astype(BF16)
            ghi, gmid, glo = _split3(gates)
            gc_all = _dot(csum, ghi) + _dot(csum, gmid) + _dot(csum, glo)
            gates_d[e, d] = gates
            gc_d[e, d] = gc_all
            gct_d[e, d] = gc_all.T
            glast_d[e, d] = gc_all[cs - 1:cs, :] if d == 0 else gc_all[0:1, :]

    kst, qst, vst, gcol, bcol, glast, decay, egc = [], [], [], [], [], [], [], []
    for e, d, p in chains:
        q_ref, k_ref, v_ref, _ = dir_in[d]
        sl = slice(p * n, (p + 1) * n)
        lg = GATE_LANE + HEADS * d + 2 * p
        kst.append(_stack_pair(k_ref[e, :, sl], lane_lo))
        qst.append(_stack_pair(q_ref[e, :, sl], lane_lo))
        vst.append(_stack_pair(v_ref[e, :, sl], lane_lo))
        gc = _pair_col(gc_d[e, d], lg)
        gr = jnp.concatenate([gct_d[e, d][lg:lg + 1, :], gct_d[e, d][lg + 1:lg + 2, :]], axis=1)
        gcol.append(gc)
        bcol.append(_pair_col(gates_d[e, d], lg + 2 * HEADS))
        glast.append(jnp.concatenate([jnp.broadcast_to(glast_d[e, d][:, lg:lg + 1], (cs, 1)),
                                      jnp.broadcast_to(glast_d[e, d][:, lg + 1:lg + 2], (cs, 1))], axis=0))
        decay.append(jnp.exp(jnp.where(incl_d[d], gc - gr, -jnp.inf)))
        egc.append(jnp.exp(gc))
    nch = len(chains)
    kstb = [x.astype(BF16) for x in kst]
    gp = [_dot_nt(jnp.concatenate([kstb[i], qst[i].astype(BF16)], axis=0), kstb[i]) for i in range(nch)]
    a_mat = [jnp.where(strict_d[chains[i][1]], bcol[i] * gp[i][:n] * decay[i], 0.0) for i in range(nch)]
    attn = [jnp.where(incl_d[chains[i][1]], gp[i][n:] * decay[i], 0.0).astype(BF16) for i in range(nch)]
    rhs = [jnp.concatenate([vst[i] * bcol[i], kst[i] * (bcol[i] * egc[i])], axis=1).astype(BF16) for i in range(nch)]
    t_inv = _unit_tri_inverse(a_mat, eye, blk16, blk32)
    x = [_dot(t_inv[i].astype(BF16), rhs[i]) for i in range(nch)]
    s_old = [s_scr[e, d, p] for e, d, p in chains]
    sb = [s.astype(BF16) for s in s_old]
    ws = [_dot(jnp.concatenate([x[i][:, n:], qst[i] * egc[i]], axis=0).astype(BF16), sb[i]) for i in range(nch)]
    v_new = [(x[i][:, :n] - ws[i][:n]).astype(BF16) for i in range(nch)]
    o_st = [ws[i][n:] + _dot(attn[i], v_new[i]) for i in range(nch)]
    k_dec_t = [(kst[i] * jnp.exp(glast[i] - gcol[i])).T.astype(BF16) for i in range(nch)]
    for i, (e, d, p) in enumerate(chains):
        s_scr[e, d, p] = s_old[i] * jnp.exp(glast[i]) + _dot(k_dec_t[i], v_new[i])
    for e in range(n_elems):
        for d in range(2):
            o_pairs = [o_st[chains.index((e, d, p))] for p in range(PAIRS)]
            o_refs[d][e] = jnp.concatenate([o[:cs] + o[cs:] for o in o_pairs], axis=1)

    if emit_state:
        @pl.when(c == n_chunks - 1)
        def _():
            for e in range(n_elems):
                for d in range(2):
                    for p in range(PAIRS):
                        s = s_scr[e, d, p]
                        hfin_ref[e, d, 2 * p] = s[:GDN_DK, :GDN_DV]
                        hfin_ref[e, d, 2 * p + 1] = s[GDN_DK:, GDN_DV:]


GDN_ELEMS = 4


def _gdn(qg, kg, vg, gb, h0, n_batch, seq, states=None, l=0):
    nc = seq // GDN_CHUNK
    ne = GDN_ELEMS
    assert n_batch % ne == 0
    zero_init = h0 is None
    emit_state = states is not None
    fwd = lambda w: pl.BlockSpec((ne, GDN_CHUNK, w), lambda b, c: (b, c, 0))
    bwd = lambda w: pl.BlockSpec((ne, GDN_CHUNK, w), lambda b, c: (b, nc - 1 - c, 0))
    in_specs = [fwd(GDN_W), fwd(GDN_W), fwd(GDN_W), fwd(LANES), bwd(GDN_W), bwd(GDN_W), bwd(GDN_W), bwd(LANES)]
    seq3 = lambda a: a.reshape(n_batch, seq, a.shape[-1])
    args = [seq3(qg), seq3(kg), seq3(vg), seq3(gb)] * 2
    if not zero_init:
        h0_all, h0_layer = h0
        in_specs.append(pl.BlockSpec((ne, None, 2, HEADS, GDN_DK, GDN_DV), lambda b, c: (b, h0_layer, 0, 0, 0, 0)))
        args.append(h0_all)
    t = n_batch * seq
    out_shape = [jax.ShapeDtypeStruct((n_batch, seq, GDN_W), F32), jax.ShapeDtypeStruct((n_batch, seq, GDN_W), F32)]
    out_specs = [fwd(GDN_W), bwd(GDN_W)]
    aliases = {}
    if emit_state:
        aliases[len(args)] = len(out_shape)
        in_specs.append(pl.BlockSpec(memory_space=pl.ANY))
        args.append(states)
        out_shape.append(jax.ShapeDtypeStruct(states.shape, states.dtype))
        out_specs.append(pl.BlockSpec((ne, None, 2, HEADS, GDN_DK, GDN_DV), lambda b, c: (b, l, 0, 0, 0, 0)))
    outs = pl.pallas_call(
        functools.partial(_gdn_kernel, zero_init=zero_init, emit_state=emit_state, n_chunks=nc, n_elems=ne),
        out_shape=out_shape,
        grid=(n_batch // ne, nc),
        in_specs=in_specs,
        out_specs=out_specs,
        scratch_shapes=[pltpu.VMEM((ne, 2, PAIRS, PAIR_N, PAIR_N), F32)],
        input_output_aliases=aliases,
        compiler_params=_cparams(("parallel", "arbitrary")),
        name="gdn",
    )(*args)
    return outs[0].reshape(t, GDN_W), outs[1].reshape(t, GDN_W), (outs[2] if emit_state else None)


MERGE_TM = 512


def _merge_kernel(x_ref, mod_ref, oa_ref, gz_ref, conv_ref, cx_prev_ref, cx_next_ref, of_ref, ob_ref, mg_ref,
                  wpa_ref, wpb_ref, wpc_ref, wo_ref, cw_ref, gn_ref, ones_ref, fin_ref, o_ref, *,
                  seq, final_norm):
    edges = _seq_edges(x_ref.shape[0], seq, pl.program_id(0))
    o_a = oa_ref[...].astype(F32) * _silu(gz_ref[:, 0:MLA_W].astype(F32))
    pa = _dot(o_a.astype(BF16), wpa_ref[...])
    f32 = lambda ref, rows, c0: ref[rows, c0:c0 + CONV_W].astype(F32)
    every = slice(None)
    cx = f32(conv_ref, every, 0) * f32(conv_ref, every, CONV_W)
    pr, nr = slice(HALO - 1, HALO), slice(0, 1)
    prev_row = f32(cx_prev_ref, pr, 0) * f32(cx_prev_ref, pr, CONV_W)
    next_row = f32(cx_next_ref, nr, 0) * f32(cx_next_ref, nr, CONV_W)
    down, up = _shift_rows(cx, prev_row, next_row, edges)
    conv = down * cw_ref[0:1, :] + cx * cw_ref[1:2, :] + up * cw_ref[2:3, :]
    o_b = f32(conv_ref, every, 2 * CONV_W) * conv * _silu(f32(conv_ref, every, 3 * CONV_W))
    pb = _dot(o_b.astype(BF16), wpb_ref[...])
    og = of_ref[...] + ob_ref[...]
    ms = _head_sums(og * og, ones_ref[...]) * (1.0 / GDN_DV)
    o_c = og * lax.rsqrt(ms + EPS) * gn_ref[...] * _silu(gz_ref[:, MLA_W:W_GZ].astype(F32))
    pc = _dot(o_c.astype(BF16), wpc_ref[...])
    m = (_sigmoid(mg_ref[:, 0:D_MODEL].astype(F32)) * pa + _sigmoid(mg_ref[:, D_MODEL:2 * D_MODEL].astype(F32)) * pb
         + _sigmoid(mg_ref[:, 2 * D_MODEL:3 * D_MODEL].astype(F32)) * pc)
    r = _dot(m.astype(BF16), wo_ref[...])
    y = x_ref[...] + mod_ref[:, 2 * D_MODEL:3 * D_MODEL] * r
    if final_norm:
        y = _rms(y) * fin_ref[...]
    o_ref[...] = y


def _merge(x2d, mod, o_a, u_gz, u_conv, o_f, o_b, u_mg, lw, l, final_g, seq, cond_row0, cond_per_seq, final_norm):
    t = x2d.shape[0]
    tm = MERGE_TM
    assert t % tm == 0 and (seq % tm == 0 or tm % seq == 0) and (cond_per_seq == 0 or seq % tm == 0)
    tps = max(seq // tm, 1)
    row = lambda w: pl.BlockSpec((tm, w), lambda i: (i, 0))
    prev_spec, next_spec = _halo_specs(tm, 2 * CONV_W, t)
    names = ["w_pa", "w_pb", "w_pc", "w_o", "conv_b_w", "gdn_norm_lane"]
    return pl.pallas_call(
        functools.partial(_merge_kernel, seq=seq, final_norm=final_norm),
        out_shape=jax.ShapeDtypeStruct((t, D_MODEL), F32),
        grid=(t // tm,),
        in_specs=[
            row(D_MODEL),
            _mod_spec(l, lambda i: cond_row0 + (i // tps) * cond_per_seq),
            row(MLA_W), row(W_GZ), row(W_CONV), prev_spec, next_spec, row(GDN_W), row(GDN_W), row(W_MG),
        ] + [_layer_spec(lw[k], l) for k in names] + [_const_spec((GDN_W, GDN_W)), _const_spec((1, D_MODEL))],
        out_specs=row(D_MODEL),
        compiler_params=_cparams(("parallel",)),
        name="merge",
    )(x2d, mod, o_a, u_gz, u_conv, u_conv, u_conv, o_f, o_b, u_mg, *[lw[k] for k in names], lw["ones_bd"],
      final_g.reshape(1, D_MODEL))


_EVEN_ODD = list(range(0, QK_ROPE, 2)) + list(range(1, QK_ROPE, 2))
_SWAPPED = list(range(1, QK_ROPE, 2)) + list(range(0, QK_ROPE, 2))


REPACK_TN = 256
_PACK_PIECES = ((_O_CQ, _O_KPE), None,
                (_O_Q, _O_Z), (_O_C, _O_GB), (_O_B, _O_C), (_O_GB, _O_Q), (_O_GA, _O_B), (_O_Z, _O_AB),
                (_O_MG, _O_END))


def _repack_kernel(wt_ref, small_ref, o_ref):
    off = 0
    for piece in _PACK_PIECES:
        xt = small_ref[...] if piece is None else wt_ref[piece[0]:piece[1], :]
        o_ref[:, off:off + xt.shape[0]] = xt.T.astype(BF16)
        off += xt.shape[0]


def _repack_w_in(w_in):
    depth = w_in.shape[0]
    wt = jnp.swapaxes(w_in, 1, 2)
    kpe = wt[:, _O_KPE:_O_GA]
    small = jnp.concatenate([kpe[:, 0::2], kpe[:, 1::2], kpe[:, 1::2], kpe[:, 0::2], wt[:, _O_AB:_O_MG], kpe], axis=1)
    return pl.pallas_call(
        _repack_kernel,
        out_shape=jax.ShapeDtypeStruct((depth, D_MODEL, W_IN_PACKED), BF16),
        grid=(depth, D_MODEL // REPACK_TN),
        in_specs=[pl.BlockSpec((None, _O_END, REPACK_TN), lambda l, i: (l, 0, i)),
                  pl.BlockSpec((None, LANES, REPACK_TN), lambda l, i: (l, 0, i))],
        out_specs=pl.BlockSpec((None, REPACK_TN, W_IN_PACKED), lambda l, i: (l, i, 0)),
        compiler_params=_cparams(("parallel", "parallel")),
        name="repack_w_in",
    )(wt, small)


def _pack_weights(w_in, q_norm_g, kv_norm_g, w_uq, w_ukv, conv_b_w, conv_qkv_w, a_log, dt_bias, gdn_norm_g,
                  w_pa, w_pb, w_pc, w_o, norm_g):
    depth = w_in.shape[0]
    eo = np.array(_EVEN_ODD)
    sw = np.array(_SWAPPED)
    w_in_p = _repack_w_in(w_in)
    wq = w_uq.astype(BF16).reshape(depth, Q_LORA, HEADS, QK_NOPE + QK_ROPE)
    nope, rope = wq[..., :QK_NOPE], wq[..., QK_NOPE:]
    z32 = jnp.zeros((depth, Q_LORA, HEADS, HEAD_PAD - QK_NOPE - QK_ROPE), BF16)
    z64 = jnp.zeros((depth, Q_LORA, HEADS, QK_NOPE), BF16)
    w_qa = jnp.concatenate([nope, rope[..., eo], z32], axis=-1).reshape(depth, Q_LORA, QK_W)
    w_qb = jnp.concatenate([z64, rope[..., sw], z32], axis=-1).reshape(depth, Q_LORA, QK_W)
    wkv = w_ukv.astype(BF16).reshape(depth, KV_LORA, HEADS, QK_NOPE + V_HEAD)
    w_k = jnp.concatenate([wkv[..., :QK_NOPE], jnp.zeros((depth, KV_LORA, HEADS, HEAD_PAD - QK_NOPE), BF16)],
                          axis=-1).reshape(depth, KV_LORA, QK_W)
    w_v = wkv[..., QK_NOPE:].reshape(depth, KV_LORA, MLA_W)
    src = np.arange(LANES)[:, None]
    dst = np.arange(QK_W)[None, :] % HEAD_PAD - QK_NOPE
    in_rope = (dst >= 0) & (dst < QK_ROPE)
    e_plain = jnp.asarray(np.broadcast_to(in_rope & (src == dst), (depth, LANES, QK_W)), BF16)
    e_rope = jnp.asarray(np.broadcast_to(in_rope & ((src == dst) | (src == dst + QK_ROPE)), (depth, LANES, QK_W)), BF16)
    gate_pad = ((0, 0), (0, 0), (GATE_LANE, LANES - GATE_LANE - 2 * HEADS))
    hid = np.arange(GDN_W) // GDN_DK
    return {
        "w_in_p": w_in_p, "norm_g": norm_g.reshape(depth, 1, D_MODEL),
        "q_norm_g": q_norm_g.reshape(depth, 1, Q_LORA), "kv_norm_g": kv_norm_g.reshape(depth, 1, KV_LORA),
        "w_qa": w_qa, "w_qb": w_qb,
        "w_ke": jnp.concatenate([w_k, e_plain], axis=1), "w_ke_rope": jnp.concatenate([w_k, e_rope], axis=1),
        "w_v": w_v,
        "conv_qkv_w": conv_qkv_w, "conv_b_w": conv_b_w,
        "alog_lane": jnp.pad(a_log.reshape(depth, 1, 2 * HEADS), gate_pad),
        "dtb_lane": jnp.pad(dt_bias.reshape(depth, 1, 2 * HEADS), gate_pad),
        "ones_bd": jnp.asarray(hid[:, None] == hid[None, :], BF16),
        "gdn_norm_lane": jnp.tile(gdn_norm_g, (1, HEADS)).reshape(depth, 1, GDN_W),
        "w_pa": w_pa.astype(BF16), "w_pb": w_pb.astype(BF16), "w_pc": w_pc.astype(BF16), "w_o": w_o.astype(BF16),
    }


def _rope_tables(n_tokens):
    t = np.arange(n_tokens)
    row = (t // GRID_W).astype(np.float32)
    col = (t % GRID_W).astype(np.float32)
    n_freq = QK_ROPE // 4
    inv_freq = (np.float32(ROPE_THETA) ** (-np.arange(n_freq, dtype=np.float32) / n_freq)).astype(np.float32)
    ang = np.concatenate([row[:, None] * inv_freq, col[:, None] * inv_freq], axis=-1).astype(np.float32)
    cos, sin = np.cos(ang), np.sin(ang)
    cc = np.concatenate([cos, cos], axis=-1)
    ss = np.concatenate([-sin, sin], axis=-1)
    one = np.ones((n_tokens, QK_NOPE), np.float32)
    z32 = np.zeros((n_tokens, QK_ROPE), np.float32)
    z64 = np.zeros((n_tokens, QK_NOPE), np.float32)
    m1 = np.concatenate([one, cc, z32], axis=-1) * np.float32(SM_SCALE)
    m2 = np.concatenate([z64, ss, z32], axis=-1) * np.float32(SM_SCALE)
    m3 = np.concatenate([cc, ss, z64], axis=-1)
    return tuple(jnp.asarray(m, F32) for m in (m1, m2, m3))


def _layer(x2d, mod, lw, l, final_g, n_batch, seq, cond_row0, cond_per_seq, final_norm, rope_tabs=None,
           ctx_kv=None, h0=None, tq=256, attn_heads=HEADS, new_caches=None):
    is_ctx = ctx_kv is None
    u_mla, u_qkv, u_conv, u_gz, u_mg = _inproj(x2d, mod, lw, l, seq, cond_row0, cond_per_seq)
    outs = _prep(u_mla, u_qkv, lw, l, seq, rope_tabs, new_caches[:2] if is_ctx else None)
    if is_ctx:
        qf, kf, vf, ckv_new, kpe_new, qg, kg, vg, gb = outs
        kv_parts = [(kf, vf, seq)]
    else:
        qf, kf, vf, qg, kg, vg, gb = outs
        kv_parts = [ctx_kv, (kf, vf, seq)]
    o_a = _attention(qf, kv_parts, n_batch, seq, tq, attn_heads)
    o_f, o_b, states_new = _gdn(qg, kg, vg, gb, h0, n_batch, seq, new_caches[2] if is_ctx else None, l)
    y = _merge(x2d, mod, o_a, u_gz, u_conv, o_f, o_b, u_mg, lw, l, final_g, seq, cond_row0, cond_per_seq, final_norm)
    return y, ((ckv_new, kpe_new, states_new) if is_ctx else None)


def kernel(x_prompt, x_sample, c, cache_ckv, cache_kpe, state_gdn, c_ctx, norm_g, w_ada, b_ada, w_in, q_norm_g,
           kv_norm_g, w_uq, w_ukv, conv_b_w, conv_qkv_w, a_log, dt_bias, gdn_norm_g, w_pa, w_pb, w_pc, w_o,
           final_norm_g):
    n_ctx, seq_ctx, _ = x_prompt.shape
    n_lat, seq_lat, _ = x_sample.shape
    past = cache_ckv.shape[2]
    depth = w_in.shape[0]
    assert n_lat + 1 <= COND_ROWS

    cond = jnp.concatenate([c, c_ctx[None, :], jnp.zeros((COND_ROWS - n_lat - 1, D_MODEL), F32)], axis=0)
    mod = _ada_mod(cond, w_ada, b_ada).reshape(depth, COND_ROWS, 1, 3 * D_MODEL)
    rope_tabs = _rope_tables(seq_lat)
    lw = _pack_weights(w_in, q_norm_g, kv_norm_g, w_uq, w_ukv, conv_b_w, conv_qkv_w, a_log, dt_bias, gdn_norm_g,
                       w_pa, w_pb, w_pc, w_o, norm_g)
    kpe_ctx = jnp.pad(cache_kpe[..., np.array(_EVEN_ODD)], ((0, 0),) * 3 + ((0, LANES - QK_ROPE),))

    xp = x_prompt.reshape(n_ctx * seq_ctx, D_MODEL)
    xs = x_sample.reshape(n_lat * seq_lat, D_MODEL)
    new_caches = (jnp.zeros((n_ctx, depth, seq_ctx, KV_LORA), F32), jnp.zeros((n_ctx, depth, seq_ctx, QK_ROPE), F32),
                  jnp.zeros((n_ctx, depth, 2, HEADS, GDN_DK, GDN_DV), F32))
    for l in range(depth):
        last = l == depth - 1
        xp, new_caches = _layer(xp, mod, lw, l, final_norm_g, n_ctx, seq_ctx, n_lat, 0, last, new_caches=new_caches)
        kc, vc = _ctx_kv(cache_ckv, kpe_ctx, lw, l)
        xs, _ = _layer(xs, mod, lw, l, final_norm_g, n_lat, seq_lat, 0, 1, last, rope_tabs=rope_tabs,
                       ctx_kv=(kc, vc, past), h0=(state_gdn, l), tq=512, attn_heads=4)
    return (xp.reshape(n_ctx, seq_ctx, D_MODEL), xs.reshape(n_lat, seq_lat, D_MODEL)) + tuple(new_caches)
```

```python
import functools
import math

import jax
import jax.numpy as jnp
import numpy as np
from jax import lax
from jax.experimental import pallas as pl
from jax.experimental.pallas import tpu as pltpu

F32 = jnp.float32
BF16 = jnp.bfloat16

D_MODEL = 1024
DEPTH = 2
GRID_W = 64
EPS = 1e-6
HEADS = 8
Q_LORA = 384
KV_LORA = 256
QK_NOPE = 64
QK_ROPE = 32
V_HEAD = 64
MLA_W = HEADS * V_HEAD
ROPE_THETA = 10000.0
CONV_W = 512
GDN_DK = 64
GDN_DV = 64
GDN_W = HEADS * GDN_DK
GDN_CHUNK = 64
LANES = 128
HEAD_PAD = 128
QK_W = HEADS * HEAD_PAD
SM_SCALE = (QK_NOPE + QK_ROPE) ** -0.5
COND_ROWS = 16
VMEM_LIMIT = 56 * 1024 * 1024

_O_CQ, _O_CKV, _O_KPE, _O_GA = 0, 384, 640, 672
_O_B, _O_C, _O_X, _O_GB = 1184, 1696, 2208, 2720
_O_Q, _O_K, _O_V, _O_Z = 3232, 3744, 4256, 4768
_O_AB, _O_MG, _O_END = 5280, 5312, 8384
W_MLA = Q_LORA + KV_LORA + LANES
W_QKV = 3 * GDN_W
W_CONV = 4 * CONV_W
W_GZ = MLA_W + GDN_W
W_MG = 3 * D_MODEL
GROUP_WIDTHS = (W_MLA, W_QKV, W_CONV, W_GZ, W_MG)
GROUP_DTYPES = (F32, BF16, BF16, BF16, BF16)
W_IN_PACKED = sum(GROUP_WIDTHS)
GATE_LANE = 64


def _sigmoid(x):
    return 1.0 / (1.0 + jnp.exp(-x))


def _silu(x):
    return x * _sigmoid(x)


def _rms(x):
    return x * lax.rsqrt(jnp.mean(x * x, axis=-1, keepdims=True) + EPS)


def _dot(a, b):
    return jnp.dot(a, b, preferred_element_type=F32)


def _dot_nt(a, b):
    return lax.dot_general(a, b, (((1,), (1,)), ((), ())), preferred_element_type=F32)


def _split3(x):
    hi = x.astype(BF16)
    r = x - hi.astype(F32)
    mid = r.astype(BF16)
    lo = (r - mid.astype(F32)).astype(BF16)
    return hi, mid, lo


def _head_sums(x2, ones_bd):
    hi = x2.astype(BF16)
    lo = (x2 - hi.astype(F32)).astype(BF16)
    return _dot(hi, ones_bd) + _dot(lo, ones_bd)


def _cparams(sem):
    return pltpu.CompilerParams(dimension_semantics=sem, vmem_limit_bytes=VMEM_LIMIT)


def _const_spec(shape):
    nd = len(shape)
    return pl.BlockSpec(shape, lambda *_: (0,) * nd)


def _mod_spec(l, cond_row):
    return pl.BlockSpec((None, None, 1, 3 * D_MODEL), lambda i: (l, cond_row(i), 0, 0))


def _layer_spec(arr, l, **kw):
    nd = arr.ndim - 1
    return pl.BlockSpec((None,) + arr.shape[1:], lambda *_: (l,) + (0,) * nd, **kw)


def _ada_kernel(c_ref, w_ref, b_ref, o_ref):
    sc = _silu(c_ref[...])
    o_ref[...] = jnp.dot(sc, w_ref[...], preferred_element_type=F32, precision=lax.Precision.HIGHEST) + b_ref[...]


def _ada_mod(cond, w_ada, b_ada):
    depth = w_ada.shape[0]
    return pl.pallas_call(
        _ada_kernel,
        out_shape=jax.ShapeDtypeStruct((depth, COND_ROWS, 3 * D_MODEL), F32),
        grid=(depth, 3),
        in_specs=[
            pl.BlockSpec((COND_ROWS, D_MODEL), lambda l, j: (0, 0)),
            pl.BlockSpec((None, D_MODEL, D_MODEL), lambda l, j: (l, 0, j)),
            pl.BlockSpec((None, 1, D_MODEL), lambda l, j: (l, 0, j)),
        ],
        out_specs=pl.BlockSpec((None, COND_ROWS, D_MODEL), lambda l, j: (l, 0, j)),
        compiler_params=_cparams(("arbitrary", "arbitrary")),
        name="ada_mod",
    )(cond, w_ada, b_ada.reshape(depth, 1, 3 * D_MODEL))


IN_TM = 512
IN_CHUNK = 768


def _inproj_kernel(x_ref, mod_ref, g_ref, w_ref, *rest):
    outs, h_scr = rest[:-1], rest[-1]
    x = x_ref[...]
    shift = mod_ref[:, 0:D_MODEL]
    scale = mod_ref[:, D_MODEL:2 * D_MODEL]
    h = (_rms(x) * g_ref[...]) * (1.0 + scale) + shift
    h_scr[...] = h.astype(BF16)
    off = 0
    for o_ref in outs:
        n = o_ref.shape[1]
        for c0 in range(0, n, IN_CHUNK):
            c1 = min(c0 + IN_CHUNK, n)
            o_ref[:, c0:c1] = _dot(h_scr[...], w_ref[:, off + c0:off + c1]).astype(o_ref.dtype)
        off += n


def _inproj(x2d, mod, lw, l, seq, cond_row0, cond_per_seq):
    t = x2d.shape[0]
    assert t % IN_TM == 0 and (cond_per_seq == 0 or seq % IN_TM == 0)
    tps = max(seq // IN_TM, 1)
    return pl.pallas_call(
        _inproj_kernel,
        out_shape=[jax.ShapeDtypeStruct((t, n), dt) for n, dt in zip(GROUP_WIDTHS, GROUP_DTYPES)],
        grid=(t // IN_TM,),
        in_specs=[
            pl.BlockSpec((IN_TM, D_MODEL), lambda i: (i, 0)),
            _mod_spec(l, lambda i: cond_row0 + (i // tps) * cond_per_seq),
            _layer_spec(lw["norm_g"], l),
            _layer_spec(lw["w_in_p"], l, pipeline_mode=pl.Buffered(1)),
        ],
        out_specs=[pl.BlockSpec((IN_TM, n), lambda i: (i, 0)) for n in GROUP_WIDTHS],
        scratch_shapes=[pltpu.VMEM((IN_TM, D_MODEL), BF16)],
        compiler_params=_cparams(("parallel",)),
        name="inproj",
    )(x2d, mod, lw["norm_g"], lw["w_in_p"])


PREP_TM = 512


def _seq_edges(tm, seq, i):
    row = lax.broadcasted_iota(jnp.int32, (tm, 1), 0)
    if seq >= tm:
        tps = seq // tm
        return (row == 0) & ((i % tps) == 0), (row == tm - 1) & ((i % tps) == tps - 1)
    pos = lax.rem(row, seq)
    return pos == 0, pos == seq - 1


def _shift_rows(x, prev_row, next_row, edges):
    n = x.shape[0]
    row = lax.broadcasted_iota(jnp.int32, (n, 1), 0)
    starts, ends = edges
    down = jnp.where(starts, 0.0, jnp.where(row == 0, prev_row, pltpu.roll(x, 1, 0)))
    up = jnp.where(ends, 0.0, jnp.where(row == n - 1, next_row, pltpu.roll(x, n - 1, 0)))
    return down, up


def _prep_kernel(*refs, use_rope, seq, emit_ckvn):
    it = iter(refs)
    mla_ref, qkv_ref, qkv_prev_ref, qkv_next_ref = next(it), next(it), next(it), next(it)
    if use_rope:
        m1_ref, m2_ref, m3_ref = next(it), next(it), next(it)
    gq_ref, gkv_ref, wa_ref = next(it), next(it), next(it)
    if use_rope:
        wb_ref = next(it)
    wke_ref, wv_ref, cw_ref, alog_ref, dtb_ref, ones_ref = (next(it) for _ in range(6))
    if emit_ckvn:
        next(it), next(it)
    qf_ref, kf_ref, vf_ref = next(it), next(it), next(it)
    if emit_ckvn:
        ckvn_ref, kpe_ref = next(it), next(it)
    qg_ref, kg_ref, vg_ref, gb_ref = next(it), next(it), next(it), next(it)

    edges = _seq_edges(mla_ref.shape[0], seq, pl.program_id(0))

    cq = mla_ref[:, 0:Q_LORA]
    ckv = mla_ref[:, Q_LORA:Q_LORA + KV_LORA]
    small = mla_ref[:, Q_LORA + KV_LORA:W_MLA]
    qn = (_rms(cq) * gq_ref[...]).astype(BF16)
    qa = _dot(qn, wa_ref[...])
    if use_rope:
        qb = _dot(qn, wb_ref[...])
        m1 = m1_ref[...]
        m2 = m2_ref[...]
        for h in range(HEADS):
            sl = slice(h * HEAD_PAD, (h + 1) * HEAD_PAD)
            qf_ref[:, sl] = (qa[:, sl] * m1 + qb[:, sl] * m2).astype(BF16)
        kpe_t = small * m3_ref[...]
    else:
        qf_ref[...] = (qa * SM_SCALE).astype(BF16)
        kpe_t = small
    ckvn = _rms(ckv) * gkv_ref[...]
    if emit_ckvn:
        ckvn_ref[...] = ckvn.reshape(ckvn_ref.shape)
        kpe_ref[...] = small[:, LANES - QK_ROPE:].reshape(kpe_ref.shape)
    ckvn_b = ckvn.astype(BF16)
    kin = jnp.concatenate([ckvn_b, kpe_t.astype(BF16)], axis=1)
    kf_ref[...] = _dot(kin, wke_ref[...]).astype(BF16)
    vf_ref[...] = _dot(ckvn_b, wv_ref[...]).astype(BF16)

    z1 = small + dtb_ref[...]
    sp = jnp.maximum(z1, 0.0) + jnp.log1p(jnp.exp(-jnp.abs(z1)))
    g = -jnp.exp(alog_ref[...]) * sp
    lane = lax.broadcasted_iota(jnp.int32, (1, LANES), 1)
    gb_ref[...] = jnp.where(lane < GATE_LANE + 2 * HEADS, g, _sigmoid(small))

    ones_bd = ones_ref[...]
    for part, o_ref in enumerate((qg_ref, kg_ref, vg_ref)):
        sl = slice(part * GDN_W, (part + 1) * GDN_W)
        x = qkv_ref[:, sl].astype(F32)
        prev_row = qkv_prev_ref[HALO - 1:HALO, sl].astype(F32)
        next_row = qkv_next_ref[0:1, sl].astype(F32)
        down, up = _shift_rows(x, prev_row, next_row, edges)
        y = down * cw_ref[0:1, sl] + x * cw_ref[1:2, sl] + up * cw_ref[2:3, sl]
        y = _silu(y)
        if part == 0:
            y = y * lax.rsqrt(_head_sums(y * y, ones_bd) + EPS) * (GDN_DK ** -0.5)
        elif part == 1:
            y = y * lax.rsqrt(_head_sums(y * y, ones_bd) + EPS)
        o_ref[...] = y


HALO = 16


def _halo_specs(tm, width, n_rows, col_block=0):
    r = tm // HALO
    nb = n_rows // HALO
    prev = pl.BlockSpec((HALO, width), lambda i: (jnp.maximum(i * r - 1, 0), col_block))
    nxt = pl.BlockSpec((HALO, width), lambda i: (jnp.minimum((i + 1) * r, nb - 1), col_block))
    return prev, nxt


def _stacked_spec(tm, seq, width, l):
    if tm >= seq:
        return pl.BlockSpec((tm // seq, None, seq, width), lambda i: (i, l, 0, 0))
    tps = seq // tm
    return pl.BlockSpec((None, None, tm, width), lambda i: (i // tps, l, i % tps, 0))


def _prep(u_mla, u_qkv, lw, l, seq, rope_tabs, caches):
    t = u_mla.shape[0]
    tm = PREP_TM
    use_rope = rope_tabs is not None
    emit_ckvn = caches is not None
    assert t % tm == 0 and (seq % tm == 0 or tm % seq == 0) and (not use_rope or seq % tm == 0)
    tps = max(seq // tm, 1)
    row = lambda w: pl.BlockSpec((tm, w), lambda i: (i, 0))
    prev_spec, next_spec = _halo_specs(tm, W_QKV, t)
    in_specs = [row(W_MLA), row(W_QKV), prev_spec, next_spec]
    args = [u_mla, u_qkv, u_qkv, u_qkv]
    if use_rope:
        in_specs += [pl.BlockSpec((tm, LANES), lambda i: (i % tps, 0))] * 3
        args += list(rope_tabs)
    names = ["q_norm_g", "kv_norm_g", "w_qa"] + (["w_qb"] if use_rope else [])
    names += ["w_ke_rope" if use_rope else "w_ke", "w_v", "conv_qkv_w", "alog_lane", "dtb_lane"]
    in_specs += [_layer_spec(lw[k], l) for k in names] + [_const_spec((GDN_W, GDN_W))]
    args += [lw[k] for k in names] + [lw["ones_bd"]]
    out_shape = [jax.ShapeDtypeStruct((t, QK_W), BF16), jax.ShapeDtypeStruct((t, QK_W), BF16),
                 jax.ShapeDtypeStruct((t, MLA_W), BF16)]
    out_specs = [row(QK_W), row(QK_W), row(MLA_W)]
    aliases = {}
    if emit_ckvn:
        for a in caches:
            aliases[len(args)] = len(out_shape)
            in_specs.append(pl.BlockSpec(memory_space=pl.ANY))
            args.append(a)
            out_shape.append(jax.ShapeDtypeStruct(a.shape, a.dtype))
            out_specs.append(_stacked_spec(tm, seq, a.shape[-1], l))
    out_shape += [jax.ShapeDtypeStruct((t, GDN_W), F32)] * 3 + [jax.ShapeDtypeStruct((t, LANES), F32)]
    out_specs += [row(GDN_W)] * 3 + [row(LANES)]
    outs = pl.pallas_call(
        functools.partial(_prep_kernel, use_rope=use_rope, seq=seq, emit_ckvn=emit_ckvn),
        out_shape=out_shape,
        grid=(t // tm,),
        in_specs=in_specs,
        out_specs=out_specs,
        input_output_aliases=aliases,
        compiler_params=_cparams(("parallel",)),
        name="prep",
    )(*args)
    return outs


def _ctxkv_kernel(ckv_ref, kpe_ref, wke_ref, wv_ref, kf_ref, vf_ref):
    ckv_b = ckv_ref[...].astype(BF16)
    kin = jnp.concatenate([ckv_b, kpe_ref[...].astype(BF16)], axis=1)
    kf_ref[...] = _dot(kin, wke_ref[...]).astype(BF16)
    vf_ref[...] = _dot(ckv_b, wv_ref[...]).astype(BF16)


def _ctx_kv(cache_ckv, kpe_lanes, lw, l):
    n_batch, _, past, _ = cache_ckv.shape
    cached = lambda w: pl.BlockSpec((None, None, past, w), lambda i: (i, l, 0, 0))
    row = lambda w: pl.BlockSpec((past, w), lambda i: (i, 0))
    return pl.pallas_call(
        _ctxkv_kernel,
        out_shape=[jax.ShapeDtypeStruct((n_batch * past, QK_W), BF16), jax.ShapeDtypeStruct((n_batch * past, MLA_W), BF16)],
        grid=(n_batch,),
        in_specs=[cached(KV_LORA), cached(LANES), _layer_spec(lw["w_ke"], l), _layer_spec(lw["w_v"], l)],
        out_specs=[row(QK_W), row(MLA_W)],
        compiler_params=_cparams(("parallel",)),
        name="ctx_kv",
    )(cache_ckv, kpe_lanes, lw["w_ke"], lw["w_v"])


def _attn_kernel(*refs, n_parts, n_heads):
    q_ref = refs[0]
    k_refs = refs[1:1 + n_parts]
    v_refs = refs[1 + n_parts:1 + 2 * n_parts]
    o_ref = refs[1 + 2 * n_parts]
    lane = lax.broadcasted_iota(jnp.int32, (1, 2 * V_HEAD), 1)
    own = [lane < V_HEAD, lane >= V_HEAD]
    heads = range(n_heads)
    hl = [slice(j * HEAD_PAD, (j + 1) * HEAD_PAD) for j in heads]
    pl_ = [slice((j // 2) * 2 * V_HEAD, (j // 2 + 1) * 2 * V_HEAD) for j in heads]
    s = [[_dot_nt(q_ref[:, hl[j]], k_ref[:, hl[j]]) for k_ref in k_refs] for j in heads]
    m = []
    for j in heads:
        mj = s[j][0].max(axis=-1, keepdims=True)
        for sp in s[j][1:]:
            mj = jnp.maximum(mj, sp.max(axis=-1, keepdims=True))
        m.append(mj)
    p = [[jnp.exp(sp - m[j]).astype(BF16) for sp in s[j]] for j in heads]
    acc = []
    for j in heads:
        a = None
        for pj, v_ref in zip(p[j], v_refs):
            t = _dot(pj, jnp.where(own[j % 2], v_ref[:, pl_[j]], 1.0).astype(BF16))
            a = t if a is None else a + t
        acc.append(a)
    for j in range(0, n_heads, 2):
        l0 = acc[j][:, V_HEAD:V_HEAD + 1]
        l1 = acc[j + 1][:, 0:1]
        o_ref[:, pl_[j]] = jnp.where(own[0], acc[j] / l0, acc[j + 1] / l1).astype(o_ref.dtype)


def _attention(qf, kv_parts, n_batch, seq_q, tq, n_heads):
    n_parts = len(kv_parts)
    nq = seq_q // tq
    in_specs = [pl.BlockSpec((tq, n_heads * HEAD_PAD), lambda b, hp, qi: (b * nq + qi, hp))]
    in_specs += [pl.BlockSpec((sk, n_heads * HEAD_PAD), lambda b, hp, qi: (b, hp)) for _, _, sk in kv_parts]
    in_specs += [pl.BlockSpec((sk, n_heads * V_HEAD), lambda b, hp, qi: (b, hp)) for _, _, sk in kv_parts]
    return pl.pallas_call(
        functools.partial(_attn_kernel, n_parts=n_parts, n_heads=n_heads),
        out_shape=jax.ShapeDtypeStruct((n_batch * seq_q, MLA_W), BF16),
        grid=(n_batch, HEADS // n_heads, nq),
        in_specs=in_specs,
        out_specs=pl.BlockSpec((tq, n_heads * V_HEAD), lambda b, hp, qi: (b * nq + qi, hp)),
        compiler_params=_cparams(("parallel", "parallel", "arbitrary")),
        name="attention",
    )(qf, *[k for k, _, _ in kv_parts], *[v for _, v, _ in kv_parts])


PAIRS = HEADS // 2
PAIR_N = 2 * GDN_CHUNK


def _unit_tri_inverse(a_list, eye, blk16, blk32):
    diag = [jnp.where(blk16, a, 0.0) for a in a_list]
    powers = [[x.astype(BF16) for x in diag]]
    for _ in range(3):
        powers.append([_dot(x, x).astype(BF16) for x in powers[-1]])
    t = [eye - x for x in diag]
    for pw in powers[1:]:
        t = [x + _dot(x.astype(BF16), y) for x, y in zip(t, pw)]
    for off in ([jnp.where(blk32 & ~blk16, a, 0.0).astype(BF16) for a in a_list],
                [jnp.where(blk32, 0.0, a).astype(BF16) for a in a_list]):
        tb = [x.astype(BF16) for x in t]
        m = [_dot(x, y).astype(BF16) for x, y in zip(tb, off)]
        t = [x - _dot(y, z) for x, y, z in zip(t, m, tb)]
    return t


def _stack_pair(x, lane_lo):
    return jnp.concatenate([jnp.where(lane_lo, x, 0.0), jnp.where(lane_lo, 0.0, x)], axis=0)


def _pair_col(m, l0):
    return jnp.concatenate([m[:, l0:l0 + 1], m[:, l0 + 1:l0 + 2]], axis=0)


def _gdn_kernel(*refs, zero_init, emit_state, n_chunks, n_elems):
    it = iter(refs)
    dir_in = [tuple(next(it) for _ in range(4)) for _ in range(2)]
    h0_ref = None if zero_init else next(it)
    if emit_state:
        next(it)
    o_refs = (next(it), next(it))
    hfin_ref = next(it) if emit_state else None
    s_scr = next(it)
    c = pl.program_id(1)
    cs, n = GDN_CHUNK, PAIR_N

    @pl.when(c == 0)
    def _():
        if zero_init:
            s_scr[...] = jnp.zeros_like(s_scr)
        else:
            z = jnp.zeros((GDN_DK, GDN_DV), F32)
            for e in range(n_elems):
                for d in range(2):
                    for p in range(PAIRS):
                        top = jnp.concatenate([h0_ref[e, d, 2 * p], z], axis=1)
                        bot = jnp.concatenate([z, h0_ref[e, d, 2 * p + 1]], axis=1)
                        s_scr[e, d, p] = jnp.concatenate([top, bot], axis=0)

    ri = lax.broadcasted_iota(jnp.int32, (n, n), 0)
    ci = lax.broadcasted_iota(jnp.int32, (n, n), 1)
    same_head = (ri // cs) == (ci // cs)
    eye = (ri == ci).astype(F32)
    blk16 = (ri // 16) == (ci // 16)
    blk32 = (ri // 32) == (ci // 32)
    lane_lo = lax.broadcasted_iota(jnp.int32, (1, n), 1) < cs
    ti = lax.broadcasted_iota(jnp.int32, (cs, cs), 0)
    tj = lax.broadcasted_iota(jnp.int32, (cs, cs), 1)

    chains = [(e, d, p) for e in range(n_elems) for d in range(2) for p in range(PAIRS)]
    incl_d = [same_head & (ri >= ci), same_head & (ri <= ci)]
    strict_d = [same_head & (ri > ci), same_head & (ri < ci)]
    gates_d, gc_d, gct_d, glast_d = {}, {}, {}, {}
    for e in range(n_elems):
        for d in range(2):
            gates = dir_in[d][3][e]
            csum = ((ti >= tj) if d == 0 else (ti <= tj)).astype(BF16)
            ghi, gmid, glo = _split3(gates)
            gc_all = _dot(csum, ghi) + _dot(csum, gmid) + _dot(csum, glo)
            gates_d[e, d] = gates
            gc_d[e, d] = gc_all
            gct_d[e, d] = gc_all.T
            glast_d[e, d] = gc_all[cs - 1:cs, :] if d == 0 else gc_all[0:1, :]

    kst, qst, vst, gcol, bcol, glast, decay, egc = [], [], [], [], [], [], [], []
    for e, d, p in chains:
        q_ref, k_ref, v_ref, _ = dir_in[d]
        sl = slice(p * n, (p + 1) * n)
        lg = GATE_LANE + HEADS * d + 2 * p
        kst.append(_stack_pair(k_ref[e, :, sl], lane_lo))
        qst.append(_stack_pair(q_ref[e, :, sl], lane_lo))
        vst.append(_stack_pair(v_ref[e, :, sl], lane_lo))
        gc = _pair_col(gc_d[e, d], lg)
        gr = jnp.concatenate([gct_d[e, d][lg:lg + 1, :], gct_d[e, d][lg + 1:lg + 2, :]], axis=1)
        gcol.append(gc)
        bcol.append(_pair_col(gates_d[e, d], lg + 2 * HEADS))
        glast.append(jnp.concatenate([jnp.broadcast_to(glast_d[e, d][:, lg:lg + 1], (cs, 1)),
                                      jnp.broadcast_to(glast_d[e, d][:, lg + 1:lg + 2], (cs, 1))], axis=0))
        decay.append(jnp.exp(jnp.where(incl_d[d], gc - gr, -jnp.inf)))
        egc.append(jnp.exp(gc))
    nch = len(chains)
    kstb = [x.astype(BF16) for x in kst]
    gp = [_dot_nt(jnp.concatenate([kstb[i], qst[i].astype(BF16)], axis=0), kstb[i]) for i in range(nch)]
    a_mat = [jnp.where(strict_d[chains[i][1]], bcol[i] * gp[i][:n] * decay[i], 0.0) for i in range(nch)]
    attn = [jnp.where(incl_d[chains[i][1]], gp[i][n:] * decay[i], 0.0).astype(BF16) for i in range(nch)]
    rhs = [jnp.concatenate([vst[i] * bcol[i], kst[i] * (bcol[i] * egc[i])], axis=1).astype(BF16) for i in range(nch)]
    t_inv = _unit_tri_inverse(a_mat, eye, blk16, blk32)
    x = [_dot(t_inv[i].astype(BF16), rhs[i]) for i in range(nch)]
    s_old = [s_scr[e, d, p] for e, d, p in chains]
    sb = [s.astype(BF16) for s in s_old]
    v_new = [(x[i][:, :n] - _dot(x[i][:, n:].astype(BF16), sb[i])).astype(BF16) for i in range(nch)]
    o_st = [_dot(jnp.concatenate([(qst[i] * egc[i]).astype(BF16), attn[i]], axis=1),
                 jnp.concatenate([sb[i], v_new[i]], axis=0)) for i in range(nch)]
    k_dec_t = [(kst[i] * jnp.exp(glast[i] - gcol[i])).T.astype(BF16) for i in range(nch)]
    for i, (e, d, p) in enumerate(chains):
        s_scr[e, d, p] = s_old[i] * jnp.exp(glast[i]) + _dot(k_dec_t[i], v_new[i])
    for e in range(n_elems):
        for d in range(2):
            o_pairs = [o_st[chains.index((e, d, p))] for p in range(PAIRS)]
            o_refs[d][e] = jnp.concatenate([o[:cs] + o[cs:] for o in o_pairs], axis=1)

    if emit_state:
        @pl.when(c == n_chunks - 1)
        def _():
            for e in range(n_elems):
                for d in range(2):
                    for p in range(PAIRS):
                        s = s_scr[e, d, p]
                        hfin_ref[e, d, 2 * p] = s[:GDN_DK, :GDN_DV]
                        hfin_ref[e, d, 2 * p + 1] = s[GDN_DK:, GDN_DV:]


GDN_ELEMS = 4


def _gdn(qg, kg, vg, gb, h0, n_batch, seq, states=None, l=0):
    nc = seq // GDN_CHUNK
    ne = GDN_ELEMS
    assert n_batch % ne == 0
    zero_init = h0 is None
    emit_state = states is not None
    fwd = lambda w: pl.BlockSpec((ne, GDN_CHUNK, w), lambda b, c: (b, c, 0))
    bwd = lambda w: pl.BlockSpec((ne, GDN_CHUNK, w), lambda b, c: (b, nc - 1 - c, 0))
    in_specs = [fwd(GDN_W), fwd(GDN_W), fwd(GDN_W), fwd(LANES), bwd(GDN_W), bwd(GDN_W), bwd(GDN_W), bwd(LANES)]
    seq3 = lambda a: a.reshape(n_batch, seq, a.shape[-1])
    args = [seq3(qg), seq3(kg), seq3(vg), seq3(gb)] * 2
    if not zero_init:
        h0_all, h0_layer = h0
        in_specs.append(pl.BlockSpec((ne, None, 2, HEADS, GDN_DK, GDN_DV), lambda b, c: (b, h0_layer, 0, 0, 0, 0)))
        args.append(h0_all)
    t = n_batch * seq
    out_shape = [jax.ShapeDtypeStruct((n_batch, seq, GDN_W), F32), jax.ShapeDtypeStruct((n_batch, seq, GDN_W), F32)]
    out_specs = [fwd(GDN_W), bwd(GDN_W)]
    aliases = {}
    if emit_state:
        aliases[len(args)] = len(out_shape)
        in_specs.append(pl.BlockSpec(memory_space=pl.ANY))
        args.append(states)
        out_shape.append(jax.ShapeDtypeStruct(states.shape, states.dtype))
        out_specs.append(pl.BlockSpec((ne, None, 2, HEADS, GDN_DK, GDN_DV), lambda b, c: (b, l, 0, 0, 0, 0)))
    outs = pl.pallas_call(
        functools.partial(_gdn_kernel, zero_init=zero_init, emit_state=emit_state, n_chunks=nc, n_elems=ne),
        out_shape=out_shape,
        grid=(n_batch // ne, nc),
        in_specs=in_specs,
        out_specs=out_specs,
        scratch_shapes=[pltpu.VMEM((ne, 2, PAIRS, PAIR_N, PAIR_N), F32)],
        input_output_aliases=aliases,
        compiler_params=_cparams(("parallel", "arbitrary")),
        name="gdn",
    )(*args)
    return outs[0].reshape(t, GDN_W), outs[1].reshape(t, GDN_W), (outs[2] if emit_state else None)


MERGE_TM = 512


def _merge_kernel(x_ref, mod_ref, oa_ref, gz_ref, conv_ref, cx_prev_ref, cx_next_ref, of_ref, ob_ref, mg_ref,
                  wpa_ref, wpb_ref, wpc_ref, wo_ref, cw_ref, gn_ref, ones_ref, fin_ref, o_ref, *,
                  seq, final_norm):
    edges = _seq_edges(x_ref.shape[0], seq, pl.program_id(0))
    o_a = oa_ref[...].astype(F32) * _silu(gz_ref[:, 0:MLA_W].astype(F32))
    pa = _dot(o_a.astype(BF16), wpa_ref[...])
    f32 = lambda ref, rows, c0: ref[rows, c0:c0 + CONV_W].astype(F32)
    every = slice(None)
    cx = f32(conv_ref, every, 0) * f32(conv_ref, every, CONV_W)
    pr, nr = slice(HALO - 1, HALO), slice(0, 1)
    prev_row = f32(cx_prev_ref, pr, 0) * f32(cx_prev_ref, pr, CONV_W)
    next_row = f32(cx_next_ref, nr, 0) * f32(cx_next_ref, nr, CONV_W)
    down, up = _shift_rows(cx, prev_row, next_row, edges)
    conv = down * cw_ref[0:1, :] + cx * cw_ref[1:2, :] + up * cw_ref[2:3, :]
    o_b = f32(conv_ref, every, 2 * CONV_W) * conv * _silu(f32(conv_ref, every, 3 * CONV_W))
    pb = _dot(o_b.astype(BF16), wpb_ref[...])
    og = of_ref[...] + ob_ref[...]
    ms = _head_sums(og * og, ones_ref[...]) * (1.0 / GDN_DV)
    o_c = og * lax.rsqrt(ms + EPS) * gn_ref[...] * _silu(gz_ref[:, MLA_W:W_GZ].astype(F32))
    pc = _dot(o_c.astype(BF16), wpc_ref[...])
    m = (_sigmoid(mg_ref[:, 0:D_MODEL].astype(F32)) * pa + _sigmoid(mg_ref[:, D_MODEL:2 * D_MODEL].astype(F32)) * pb
         + _sigmoid(mg_ref[:, 2 * D_MODEL:3 * D_MODEL].astype(F32)) * pc)
    r = _dot(m.astype(BF16), wo_ref[...])
    y = x_ref[...] + mod_ref[:, 2 * D_MODEL:3 * D_MODEL] * r
    if final_norm:
        y = _rms(y) * fin_ref[...]
    o_ref[...] = y


def _merge(x2d, mod, o_a, u_gz, u_conv, o_f, o_b, u_mg, lw, l, final_g, seq, cond_row0, cond_per_seq, final_norm):
    t = x2d.shape[0]
    tm = MERGE_TM
    assert t % tm == 0 and (seq % tm == 0 or tm % seq == 0) and (cond_per_seq == 0 or seq % tm == 0)
    tps = max(seq // tm, 1)
    row = lambda w: pl.BlockSpec((tm, w), lambda i: (i, 0))
    prev_spec, next_spec = _halo_specs(tm, 2 * CONV_W, t)
    names = ["w_pa", "w_pb", "w_pc", "w_o", "conv_b_w", "gdn_norm_lane"]
    return pl.pallas_call(
        functools.partial(_merge_kernel, seq=seq, final_norm=final_norm),
        out_shape=jax.ShapeDtypeStruct((t, D_MODEL), F32),
        grid=(t // tm,),
        in_specs=[
            row(D_MODEL),
            _mod_spec(l, lambda i: cond_row0 + (i // tps) * cond_per_seq),
            row(MLA_W), row(W_GZ), row(W_CONV), prev_spec, next_spec, row(GDN_W), row(GDN_W), row(W_MG),
        ] + [_layer_spec(lw[k], l) for k in names] + [_const_spec((GDN_W, GDN_W)), _const_spec((1, D_MODEL))],
        out_specs=row(D_MODEL),
        compiler_params=_cparams(("parallel",)),
        name="merge",
    )(x2d, mod, o_a, u_gz, u_conv, u_conv, u_conv, o_f, o_b, u_mg, *[lw[k] for k in names], lw["ones_bd"],
      final_g.reshape(1, D_MODEL))


_EVEN_ODD = list(range(0, QK_ROPE, 2)) + list(range(1, QK_ROPE, 2))
_SWAPPED = list(range(1, QK_ROPE, 2)) + list(range(0, QK_ROPE, 2))


REPACK_TN = 256
_PACK_PIECES = ((_O_CQ, _O_KPE), None,
                (_O_Q, _O_Z), (_O_C, _O_GB), (_O_B, _O_C), (_O_GB, _O_Q), (_O_GA, _O_B), (_O_Z, _O_AB),
                (_O_MG, _O_END))


def _repack_kernel(wt_ref, small_ref, o_ref):
    off = 0
    for piece in _PACK_PIECES:
        xt = small_ref[...] if piece is None else wt_ref[piece[0]:piece[1], :]
        o_ref[:, off:off + xt.shape[0]] = xt.T.astype(BF16)
        off += xt.shape[0]


def _repack_w_in(w_in):
    depth = w_in.shape[0]
    wt = jnp.swapaxes(w_in, 1, 2)
    kpe = wt[:, _O_KPE:_O_GA]
    small = jnp.concatenate([kpe[:, 0::2], kpe[:, 1::2], kpe[:, 1::2], kpe[:, 0::2], wt[:, _O_AB:_O_MG], kpe], axis=1)
    return pl.pallas_call(
        _repack_kernel,
        out_shape=jax.ShapeDtypeStruct((depth, D_MODEL, W_IN_PACKED), BF16),
        grid=(depth, D_MODEL // REPACK_TN),
        in_specs=[pl.BlockSpec((None, _O_END, REPACK_TN), lambda l, i: (l, 0, i)),
                  pl.BlockSpec((None, LANES, REPACK_TN), lambda l, i: (l, 0, i))],
        out_specs=pl.BlockSpec((None, REPACK_TN, W_IN_PACKED), lambda l, i: (l, i, 0)),
        compiler_params=_cparams(("parallel", "parallel")),
        name="repack_w_in",
    )(wt, small)


def _pack_weights(w_in, q_norm_g, kv_norm_g, w_uq, w_ukv, conv_b_w, conv_qkv_w, a_log, dt_bias, gdn_norm_g,
                  w_pa, w_pb, w_pc, w_o, norm_g):
    depth = w_in.shape[0]
    eo = np.array(_EVEN_ODD)
    sw = np.array(_SWAPPED)
    w_in_p = _repack_w_in(w_in)
    wq = w_uq.astype(BF16).reshape(depth, Q_LORA, HEADS, QK_NOPE + QK_ROPE)
    nope, rope = wq[..., :QK_NOPE], wq[..., QK_NOPE:]
    z32 = jnp.zeros((depth, Q_LORA, HEADS, HEAD_PAD - QK_NOPE - QK_ROPE), BF16)
    z64 = jnp.zeros((depth, Q_LORA, HEADS, QK_NOPE), BF16)
    w_qa = jnp.concatenate([nope, rope[..., eo], z32], axis=-1).reshape(depth, Q_LORA, QK_W)
    w_qb = jnp.concatenate([z64, rope[..., sw], z32], axis=-1).reshape(depth, Q_LORA, QK_W)
    wkv = w_ukv.astype(BF16).reshape(depth, KV_LORA, HEADS, QK_NOPE + V_HEAD)
    w_k = jnp.concatenate([wkv[..., :QK_NOPE], jnp.zeros((depth, KV_LORA, HEADS, HEAD_PAD - QK_NOPE), BF16)],
                          axis=-1).reshape(depth, KV_LORA, QK_W)
    w_v = wkv[..., QK_NOPE:].reshape(depth, KV_LORA, MLA_W)
    src = np.arange(LANES)[:, None]
    dst = np.arange(QK_W)[None, :] % HEAD_PAD - QK_NOPE
    in_rope = (dst >= 0) & (dst < QK_ROPE)
    e_plain = jnp.asarray(np.broadcast_to(in_rope & (src == dst), (depth, LANES, QK_W)), BF16)
    e_rope = jnp.asarray(np.broadcast_to(in_rope & ((src == dst) | (src == dst + QK_ROPE)), (depth, LANES, QK_W)), BF16)
    gate_pad = ((0, 0), (0, 0), (GATE_LANE, LANES - GATE_LANE - 2 * HEADS))
    hid = np.arange(GDN_W) // GDN_DK
    return {
        "w_in_p": w_in_p, "norm_g": norm_g.reshape(depth, 1, D_MODEL),
        "q_norm_g": q_norm_g.reshape(depth, 1, Q_LORA), "kv_norm_g": kv_norm_g.reshape(depth, 1, KV_LORA),
        "w_qa": w_qa, "w_qb": w_qb,
        "w_ke": jnp.concatenate([w_k, e_plain], axis=1), "w_ke_rope": jnp.concatenate([w_k, e_rope], axis=1),
        "w_v": w_v,
        "conv_qkv_w": conv_qkv_w, "conv_b_w": conv_b_w,
        "alog_lane": jnp.pad(a_log.reshape(depth, 1, 2 * HEADS), gate_pad),
        "dtb_lane": jnp.pad(dt_bias.reshape(depth, 1, 2 * HEADS), gate_pad),
        "ones_bd": jnp.asarray(hid[:, None] == hid[None, :], BF16),
        "gdn_norm_lane": jnp.tile(gdn_norm_g, (1, HEADS)).reshape(depth, 1, GDN_W),
        "w_pa": w_pa.astype(BF16), "w_pb": w_pb.astype(BF16), "w_pc": w_pc.astype(BF16), "w_o": w_o.astype(BF16),
    }


def _rope_tables(n_tokens):
    t = np.arange(n_tokens)
    row = (t // GRID_W).astype(np.float32)
    col = (t % GRID_W).astype(np.float32)
    n_freq = QK_ROPE // 4
    inv_freq = (np.float32(ROPE_THETA) ** (-np.arange(n_freq, dtype=np.float32) / n_freq)).astype(np.float32)
    ang = np.concatenate([row[:, None] * inv_freq, col[:, None] * inv_freq], axis=-1).astype(np.float32)
    cos, sin = np.cos(ang), np.sin(ang)
    cc = np.concatenate([cos, cos], axis=-1)
    ss = np.concatenate([-sin, sin], axis=-1)
    one = np.ones((n_tokens, QK_NOPE), np.float32)
    z32 = np.zeros((n_tokens, QK_ROPE), np.float32)
    z64 = np.zeros((n_tokens, QK_NOPE), np.float32)
    m1 = np.concatenate([one, cc, z32], axis=-1) * np.float32(SM_SCALE)
    m2 = np.concatenate([z64, ss, z32], axis=-1) * np.float32(SM_SCALE)
    m3 = np.concatenate([cc, ss, z64], axis=-1)
    return tuple(jnp.asarray(m, F32) for m in (m1, m2, m3))


def _layer(x2d, mod, lw, l, final_g, n_batch, seq, cond_row0, cond_per_seq, final_norm, rope_tabs=None,
           ctx_kv=None, h0=None, tq=256, attn_heads=HEADS, new_caches=None):
    is_ctx = ctx_kv is None
    u_mla, u_qkv, u_conv, u_gz, u_mg = _inproj(x2d, mod, lw, l, seq, cond_row0, cond_per_seq)
    outs = _prep(u_mla, u_qkv, lw, l, seq, rope_tabs, new_caches[:2] if is_ctx else None)
    if is_ctx:
        qf, kf, vf, ckv_new, kpe_new, qg, kg, vg, gb = outs
        kv_parts = [(kf, vf, seq)]
    else:
        qf, kf, vf, qg, kg, vg, gb = outs
        kv_parts = [ctx_kv, (kf, vf, seq)]
    o_a = _attention(qf, kv_parts, n_batch, seq, tq, attn_heads)
    o_f, o_b, states_new = _gdn(qg, kg, vg, gb, h0, n_batch, seq, new_caches[2] if is_ctx else None, l)
    y = _merge(x2d, mod, o_a, u_gz, u_conv, o_f, o_b, u_mg, lw, l, final_g, seq, cond_row0, cond_per_seq, final_norm)
    return y, ((ckv_new, kpe_new, states_new) if is_ctx else None)


def kernel(x_prompt, x_sample, c, cache_ckv, cache_kpe, state_gdn, c_ctx, norm_g, w_ada, b_ada, w_in, q_norm_g,
           kv_norm_g, w_uq, w_ukv, conv_b_w, conv_qkv_w, a_log, dt_bias, gdn_norm_g, w_pa, w_pb, w_pc, w_o,
           final_norm_g):
    n_ctx, seq_ctx, _ = x_prompt.shape
    n_lat, seq_lat, _ = x_sample.shape
    past = cache_ckv.shape[2]
    depth = w_in.shape[0]
    assert n_lat + 1 <= COND_ROWS

    cond = jnp.concatenate([c, c_ctx[None, :], jnp.zeros((COND_ROWS - n_lat - 1, D_MODEL), F32)], axis=0)
    mod = _ada_mod(cond, w_ada, b_ada).reshape(depth, COND_ROWS, 1, 3 * D_MODEL)
    rope_tabs = _rope_tables(seq_lat)
    lw = _pack_weights(w_in, q_norm_g, kv_norm_g, w_uq, w_ukv, conv_b_w, conv_qkv_w, a_log, dt_bias, gdn_norm_g,
                       w_pa, w_pb, w_pc, w_o, norm_g)
    kpe_ctx = jnp.pad(cache_kpe[..., np.array(_EVEN_ODD)], ((0, 0),) * 3 + ((0, LANES - QK_ROPE),))

    xp = x_prompt.reshape(n_ctx * seq_ctx, D_MODEL)
    xs = x_sample.reshape(n_lat * seq_lat, D_MODEL)
    new_caches = (jnp.zeros((n_ctx, depth, seq_ctx, KV_LORA), F32), jnp.zeros((n_ctx, depth, seq_ctx, QK_ROPE), F32),
                  jnp.zeros((n_ctx, depth, 2, HEADS, GDN_DK, GDN_DV), F32))
    for l in range(depth):
        last = l == depth - 1
        xp, new_caches = _layer(xp, mod, lw, l, final_norm_g, n_ctx, seq_ctx, n_lat, 0, last, new_caches=new_caches)
        kc, vc = _ctx_kv(cache_ckv, kpe_ctx, lw, l)
        xs, _ = _layer(xs, mod, lw, l, final_norm_g, n_lat, seq_lat, 0, 1, last, rope_tabs=rope_tabs,
                       ctx_kv=(kc, vc, past), h0=(state_gdn, l), tq=1024, attn_heads=2)
    return (xp.reshape(n_ctx, seq_ctx, D_MODEL), xs.reshape(n_lat, seq_lat, D_MODEL)) + tuple(new_caches)
```

```python
import functools
import math

import jax
import jax.numpy as jnp
import numpy as np
from jax import lax
from jax.experimental import pallas as pl
from jax.experimental.pallas import tpu as pltpu

F32 = jnp.float32
BF16 = jnp.bfloat16

D_MODEL = 1024
DEPTH = 2
GRID_W = 64
EPS = 1e-6
HEADS = 8
Q_LORA = 384
KV_LORA = 256
QK_NOPE = 64
QK_ROPE = 32
V_HEAD = 64
MLA_W = HEADS * V_HEAD
ROPE_THETA = 10000.0
CONV_W = 512
GDN_DK = 64
GDN_DV = 64
GDN_W = HEADS * GDN_DK
GDN_CHUNK = 64
LANES = 128
HEAD_PAD = 128
QK_W = HEADS * HEAD_PAD
SM_SCALE = (QK_NOPE + QK_ROPE) ** -0.5
COND_ROWS = 16
VMEM_LIMIT = 56 * 1024 * 1024

_O_CQ, _O_CKV, _O_KPE, _O_GA = 0, 384, 640, 672
_O_B, _O_C, _O_X, _O_GB = 1184, 1696, 2208, 2720
_O_Q, _O_K, _O_V, _O_Z = 3232, 3744, 4256, 4768
_O_AB, _O_MG, _O_END = 5280, 5312, 8384
W_MLA = Q_LORA + KV_LORA + LANES
W_QKV = 3 * GDN_W
W_CONV = 4 * CONV_W
W_GZ = MLA_W + GDN_W
W_MG = 3 * D_MODEL
GROUP_WIDTHS = (W_MLA, W_QKV, W_CONV, W_GZ, W_MG)
GROUP_DTYPES = (F32, BF16, BF16, BF16, BF16)
W_IN_PACKED = sum(GROUP_WIDTHS)
GATE_LANE = 64


def _sigmoid(x):
    return 1.0 / (1.0 + jnp.exp(-x))


def _silu(x):
    return x * _sigmoid(x)


def _rms(x):
    return x * lax.rsqrt(jnp.mean(x * x, axis=-1, keepdims=True) + EPS)


def _dot(a, b):
    return jnp.dot(a, b, preferred_element_type=F32)


def _dot_nt(a, b):
    return lax.dot_general(a, b, (((1,), (1,)), ((), ())), preferred_element_type=F32)


def _split3(x):
    hi = x.astype(BF16)
    r = x - hi.astype(F32)
    mid = r.astype(BF16)
    lo = (r - mid.astype(F32)).astype(BF16)
    return hi, mid, lo


def _head_sums(x2, ones_bd):
    return _dot(x2.astype(BF16), ones_bd)


def _cparams(sem):
    return pltpu.CompilerParams(dimension_semantics=sem, vmem_limit_bytes=VMEM_LIMIT)


def _const_spec(shape):
    nd = len(shape)
    return pl.BlockSpec(shape, lambda *_: (0,) * nd)


def _mod_spec(l, cond_row):
    return pl.BlockSpec((None, None, 1, 3 * D_MODEL), lambda i: (l, cond_row(i), 0, 0))


def _layer_spec(arr, l, **kw):
    nd = arr.ndim - 1
    return pl.BlockSpec((None,) + arr.shape[1:], lambda *_: (l,) + (0,) * nd, **kw)


def _ada_kernel(c_ref, w_ref, b_ref, o_ref):
    sc = _silu(c_ref[...])
    o_ref[...] = jnp.dot(sc, w_ref[...], preferred_element_type=F32, precision=lax.Precision.HIGHEST) + b_ref[...]


def _ada_mod(cond, w_ada, b_ada):
    depth = w_ada.shape[0]
    return pl.pallas_call(
        _ada_kernel,
        out_shape=jax.ShapeDtypeStruct((depth, COND_ROWS, 3 * D_MODEL), F32),
        grid=(depth, 3),
        in_specs=[
            pl.BlockSpec((COND_ROWS, D_MODEL), lambda l, j: (0, 0)),
            pl.BlockSpec((None, D_MODEL, D_MODEL), lambda l, j: (l, 0, j)),
            pl.BlockSpec((None, 1, D_MODEL), lambda l, j: (l, 0, j)),
        ],
        out_specs=pl.BlockSpec((None, COND_ROWS, D_MODEL), lambda l, j: (l, 0, j)),
        compiler_params=_cparams(("arbitrary", "arbitrary")),
        name="ada_mod",
    )(cond, w_ada, b_ada.reshape(depth, 1, 3 * D_MODEL))


IN_TM = 512
IN_CHUNK = 768


def _inproj_kernel(x_ref, mod_ref, g_ref, w_ref, *rest):
    outs, h_scr = rest[:-1], rest[-1]
    x = x_ref[...]
    shift = mod_ref[:, 0:D_MODEL]
    scale = mod_ref[:, D_MODEL:2 * D_MODEL]
    h = (_rms(x) * g_ref[...]) * (1.0 + scale) + shift
    h_scr[...] = h.astype(BF16)
    off = 0
    for o_ref in outs:
        n = o_ref.shape[1]
        for c0 in range(0, n, IN_CHUNK):
            c1 = min(c0 + IN_CHUNK, n)
            o_ref[:, c0:c1] = _dot(h_scr[...], w_ref[:, off + c0:off + c1]).astype(o_ref.dtype)
        off += n


def _inproj(x2d, mod, lw, l, seq, cond_row0, cond_per_seq):
    t = x2d.shape[0]
    assert t % IN_TM == 0 and (cond_per_seq == 0 or seq % IN_TM == 0)
    tps = max(seq // IN_TM, 1)
    return pl.pallas_call(
        _inproj_kernel,
        out_shape=[jax.ShapeDtypeStruct((t, n), dt) for n, dt in zip(GROUP_WIDTHS, GROUP_DTYPES)],
        grid=(t // IN_TM,),
        in_specs=[
            pl.BlockSpec((IN_TM, D_MODEL), lambda i: (i, 0)),
            _mod_spec(l, lambda i: cond_row0 + (i // tps) * cond_per_seq),
            _layer_spec(lw["norm_g"], l),
            _layer_spec(lw["w_in_p"], l, pipeline_mode=pl.Buffered(1)),
        ],
        out_specs=[pl.BlockSpec((IN_TM, n), lambda i: (i, 0)) for n in GROUP_WIDTHS],
        scratch_shapes=[pltpu.VMEM((IN_TM, D_MODEL), BF16)],
        compiler_params=_cparams(("parallel",)),
        name="inproj",
    )(x2d, mod, lw["norm_g"], lw["w_in_p"])


PREP_TM = 512


def _seq_edges(tm, seq, i):
    row = lax.broadcasted_iota(jnp.int32, (tm, 1), 0)
    if seq >= tm:
        tps = seq // tm
        return (row == 0) & ((i % tps) == 0), (row == tm - 1) & ((i % tps) == tps - 1)
    pos = lax.rem(row, seq)
    return pos == 0, pos == seq - 1


def _shift_rows(x, prev_row, next_row, edges):
    n = x.shape[0]
    row = lax.broadcasted_iota(jnp.int32, (n, 1), 0)
    starts, ends = edges
    down = jnp.where(starts, 0.0, jnp.where(row == 0, prev_row, pltpu.roll(x, 1, 0)))
    up = jnp.where(ends, 0.0, jnp.where(row == n - 1, next_row, pltpu.roll(x, n - 1, 0)))
    return down, up


def _prep_kernel(*refs, use_rope, seq, emit_ckvn):
    it = iter(refs)
    mla_ref, qkv_ref, qkv_prev_ref, qkv_next_ref = next(it), next(it), next(it), next(it)
    if use_rope:
        m1_ref, m2_ref, m3_ref = next(it), next(it), next(it)
    gq_ref, gkv_ref, wa_ref = next(it), next(it), next(it)
    if use_rope:
        wb_ref = next(it)
    wke_ref, wv_ref, cw_ref, alog_ref, dtb_ref, ones_ref = (next(it) for _ in range(6))
    if emit_ckvn:
        next(it), next(it)
    qf_ref, kf_ref, vf_ref = next(it), next(it), next(it)
    if emit_ckvn:
        ckvn_ref, kpe_ref = next(it), next(it)
    qg_ref, kg_ref, vg_ref, gb_ref = next(it), next(it), next(it), next(it)

    edges = _seq_edges(mla_ref.shape[0], seq, pl.program_id(0))

    cq = mla_ref[:, 0:Q_LORA]
    ckv = mla_ref[:, Q_LORA:Q_LORA + KV_LORA]
    small = mla_ref[:, Q_LORA + KV_LORA:W_MLA]
    qn = (_rms(cq) * gq_ref[...]).astype(BF16)
    qa = _dot(qn, wa_ref[...])
    if use_rope:
        qb = _dot(qn, wb_ref[...])
        m1 = m1_ref[...]
        m2 = m2_ref[...]
        for h in range(HEADS):
            sl = slice(h * HEAD_PAD, (h + 1) * HEAD_PAD)
            qf_ref[:, sl] = (qa[:, sl] * m1 + qb[:, sl] * m2).astype(BF16)
        kpe_t = small * m3_ref[...]
    else:
        qf_ref[...] = (qa * SM_SCALE).astype(BF16)
        kpe_t = small
    ckvn = _rms(ckv) * gkv_ref[...]
    if emit_ckvn:
        ckvn_ref[...] = ckvn.reshape(ckvn_ref.shape)
        kpe_ref[...] = small[:, LANES - QK_ROPE:].reshape(kpe_ref.shape)
    ckvn_b = ckvn.astype(BF16)
    kin = jnp.concatenate([ckvn_b, kpe_t.astype(BF16)], axis=1)
    kf_ref[...] = _dot(kin, wke_ref[...]).astype(BF16)
    vf_ref[...] = _dot(ckvn_b, wv_ref[...]).astype(BF16)

    z1 = small + dtb_ref[...]
    sp = jnp.maximum(z1, 0.0) + jnp.log1p(jnp.exp(-jnp.abs(z1)))
    g = -jnp.exp(alog_ref[...]) * sp
    lane = lax.broadcasted_iota(jnp.int32, (1, LANES), 1)
    gb_ref[...] = jnp.where(lane < GATE_LANE + 2 * HEADS, g, _sigmoid(small))

    ones_bd = ones_ref[...]
    for part, o_ref in enumerate((qg_ref, kg_ref, vg_ref)):
        sl = slice(part * GDN_W, (part + 1) * GDN_W)
        x = qkv_ref[:, sl].astype(F32)
        prev_row = qkv_prev_ref[HALO - 1:HALO, sl].astype(F32)
        next_row = qkv_next_ref[0:1, sl].astype(F32)
        down, up = _shift_rows(x, prev_row, next_row, edges)
        y = down * cw_ref[0:1, sl] + x * cw_ref[1:2, sl] + up * cw_ref[2:3, sl]
        y = _silu(y)
        if part == 0:
            y = y * lax.rsqrt(_head_sums(y * y, ones_bd) + EPS) * (GDN_DK ** -0.5)
        elif part == 1:
            y = y * lax.rsqrt(_head_sums(y * y, ones_bd) + EPS)
        o_ref[...] = y


HALO = 16


def _halo_specs(tm, width, n_rows, col_block=0):
    r = tm // HALO
    nb = n_rows // HALO
    prev = pl.BlockSpec((HALO, width), lambda i: (jnp.maximum(i * r - 1, 0), col_block))
    nxt = pl.BlockSpec((HALO, width), lambda i: (jnp.minimum((i + 1) * r, nb - 1), col_block))
    return prev, nxt


def _stacked_spec(tm, seq, width, l):
    if tm >= seq:
        return pl.BlockSpec((tm // seq, None, seq, width), lambda i: (i, l, 0, 0))
    tps = seq // tm
    return pl.BlockSpec((None, None, tm, width), lambda i: (i // tps, l, i % tps, 0))


def _prep(u_mla, u_qkv, lw, l, seq, rope_tabs, caches):
    t = u_mla.shape[0]
    tm = PREP_TM
    use_rope = rope_tabs is not None
    emit_ckvn = caches is not None
    assert t % tm == 0 and (seq % tm == 0 or tm % seq == 0) and (not use_rope or seq % tm == 0)
    tps = max(seq // tm, 1)
    row = lambda w: pl.BlockSpec((tm, w), lambda i: (i, 0))
    prev_spec, next_spec = _halo_specs(tm, W_QKV, t)
    in_specs = [row(W_MLA), row(W_QKV), prev_spec, next_spec]
    args = [u_mla, u_qkv, u_qkv, u_qkv]
    if use_rope:
        in_specs += [pl.BlockSpec((tm, LANES), lambda i: (i % tps, 0))] * 3
        args += list(rope_tabs)
    names = ["q_norm_g", "kv_norm_g", "w_qa"] + (["w_qb"] if use_rope else [])
    names += ["w_ke_rope" if use_rope else "w_ke", "w_v", "conv_qkv_w", "alog_lane", "dtb_lane"]
    in_specs += [_layer_spec(lw[k], l) for k in names] + [_const_spec((GDN_W, GDN_W))]
    args += [lw[k] for k in names] + [lw["ones_bd"]]
    out_shape = [jax.ShapeDtypeStruct((t, QK_W), BF16), jax.ShapeDtypeStruct((t, QK_W), BF16),
                 jax.ShapeDtypeStruct((t, MLA_W), BF16)]
    out_specs = [row(QK_W), row(QK_W), row(MLA_W)]
    aliases = {}
    if emit_ckvn:
        for a in caches:
            aliases[len(args)] = len(out_shape)
            in_specs.append(pl.BlockSpec(memory_space=pl.ANY))
            args.append(a)
            out_shape.append(jax.ShapeDtypeStruct(a.shape, a.dtype))
            out_specs.append(_stacked_spec(tm, seq, a.shape[-1], l))
    out_shape += [jax.ShapeDtypeStruct((t, GDN_W), F32)] * 3 + [jax.ShapeDtypeStruct((t, LANES), F32)]
    out_specs += [row(GDN_W)] * 3 + [row(LANES)]
    outs = pl.pallas_call(
        functools.partial(_prep_kernel, use_rope=use_rope, seq=seq, emit_ckvn=emit_ckvn),
        out_shape=out_shape,
        grid=(t // tm,),
        in_specs=in_specs,
        out_specs=out_specs,
        input_output_aliases=aliases,
        compiler_params=_cparams(("parallel",)),
        name="prep",
    )(*args)
    return outs


def _ctxkv_kernel(ckv_ref, kpe_ref, wke_ref, wv_ref, kf_ref, vf_ref):
    ckv_b = ckv_ref[...].astype(BF16)
    kin = jnp.concatenate([ckv_b, kpe_ref[...].astype(BF16)], axis=1)
    kf_ref[...] = _dot(kin, wke_ref[...]).astype(BF16)
    vf_ref[...] = _dot(ckv_b, wv_ref[...]).astype(BF16)


def _ctx_kv(cache_ckv, kpe_lanes, lw, l):
    n_batch, _, past, _ = cache_ckv.shape
    cached = lambda w: pl.BlockSpec((None, None, past, w), lambda i: (i, l, 0, 0))
    row = lambda w: pl.BlockSpec((past, w), lambda i: (i, 0))
    return pl.pallas_call(
        _ctxkv_kernel,
        out_shape=[jax.ShapeDtypeStruct((n_batch * past, QK_W), BF16), jax.ShapeDtypeStruct((n_batch * past, MLA_W), BF16)],
        grid=(n_batch,),
        in_specs=[cached(KV_LORA), cached(LANES), _layer_spec(lw["w_ke"], l), _layer_spec(lw["w_v"], l)],
        out_specs=[row(QK_W), row(MLA_W)],
        compiler_params=_cparams(("parallel",)),
        name="ctx_kv",
    )(cache_ckv, kpe_lanes, lw["w_ke"], lw["w_v"])


def _attn_kernel(*refs, n_parts, n_heads):
    q_ref = refs[0]
    k_refs = refs[1:1 + n_parts]
    v_refs = refs[1 + n_parts:1 + 2 * n_parts]
    o_ref = refs[1 + 2 * n_parts]
    lane = lax.broadcasted_iota(jnp.int32, (1, 2 * V_HEAD), 1)
    own = [lane < V_HEAD, lane >= V_HEAD]
    heads = range(n_heads)
    hl = [slice(j * HEAD_PAD, (j + 1) * HEAD_PAD) for j in heads]
    pl_ = [slice((j // 2) * 2 * V_HEAD, (j // 2 + 1) * 2 * V_HEAD) for j in heads]
    s = [[_dot_nt(q_ref[:, hl[j]], k_ref[:, hl[j]]) for k_ref in k_refs] for j in heads]
    m = []
    for j in heads:
        mj = s[j][0].max(axis=-1, keepdims=True)
        for sp in s[j][1:]:
            mj = jnp.maximum(mj, sp.max(axis=-1, keepdims=True))
        m.append(mj)
    p = [[jnp.exp(sp - m[j]).astype(BF16) for sp in s[j]] for j in heads]
    acc = []
    for j in heads:
        a = None
        for pj, v_ref in zip(p[j], v_refs):
            t = _dot(pj, jnp.where(own[j % 2], v_ref[:, pl_[j]], 1.0).astype(BF16))
            a = t if a is None else a + t
        acc.append(a)
    for j in range(0, n_heads, 2):
        l0 = acc[j][:, V_HEAD:V_HEAD + 1]
        l1 = acc[j + 1][:, 0:1]
        o_ref[:, pl_[j]] = jnp.where(own[0], acc[j] / l0, acc[j + 1] / l1).astype(o_ref.dtype)


def _attention(qf, kv_parts, n_batch, seq_q, tq, n_heads):
    n_parts = len(kv_parts)
    nq = seq_q // tq
    in_specs = [pl.BlockSpec((tq, n_heads * HEAD_PAD), lambda b, hp, qi: (b * nq + qi, hp))]
    in_specs += [pl.BlockSpec((sk, n_heads * HEAD_PAD), lambda b, hp, qi: (b, hp)) for _, _, sk in kv_parts]
    in_specs += [pl.BlockSpec((sk, n_heads * V_HEAD), lambda b, hp, qi: (b, hp)) for _, _, sk in kv_parts]
    return pl.pallas_call(
        functools.partial(_attn_kernel, n_parts=n_parts, n_heads=n_heads),
        out_shape=jax.ShapeDtypeStruct((n_batch * seq_q, MLA_W), BF16),
        grid=(n_batch, HEADS // n_heads, nq),
        in_specs=in_specs,
        out_specs=pl.BlockSpec((tq, n_heads * V_HEAD), lambda b, hp, qi: (b * nq + qi, hp)),
        compiler_params=_cparams(("parallel", "parallel", "arbitrary")),
        name="attention",
    )(qf, *[k for k, _, _ in kv_parts], *[v for _, v, _ in kv_parts])


PAIRS = HEADS // 2
PAIR_N = 2 * GDN_CHUNK


def _unit_tri_inverse(a_list, eye, blk16, blk32):
    diag = [jnp.where(blk16, a, 0.0) for a in a_list]
    powers = [[x.astype(BF16) for x in diag]]
    for _ in range(3):
        powers.append([_dot(x, x).astype(BF16) for x in powers[-1]])
    t = [eye - x for x in diag]
    for pw in powers[1:]:
        t = [x + _dot(x.astype(BF16), y) for x, y in zip(t, pw)]
    for off in ([jnp.where(blk32 & ~blk16, a, 0.0).astype(BF16) for a in a_list],
                [jnp.where(blk32, 0.0, a).astype(BF16) for a in a_list]):
        tb = [x.astype(BF16) for x in t]
        m = [_dot(x, y).astype(BF16) for x, y in zip(tb, off)]
        t = [x - _dot(y, z) for x, y, z in zip(t, m, tb)]
    return t


def _stack_pair(x, lane_lo):
    return jnp.concatenate([jnp.where(lane_lo, x, 0.0), jnp.where(lane_lo, 0.0, x)], axis=0)


def _pair_col(m, l0):
    return jnp.concatenate([m[:, l0:l0 + 1], m[:, l0 + 1:l0 + 2]], axis=0)


def _gdn_kernel(*refs, zero_init, emit_state, n_chunks, n_elems):
    it = iter(refs)
    dir_in = [tuple(next(it) for _ in range(4)) for _ in range(2)]
    h0_ref = None if zero_init else next(it)
    if emit_state:
        next(it)
    o_refs = (next(it), next(it))
    hfin_ref = next(it) if emit_state else None
    s_scr = next(it)
    c = pl.program_id(1)
    cs, n = GDN_CHUNK, PAIR_N

    @pl.when(c == 0)
    def _():
        if zero_init:
            s_scr[...] = jnp.zeros_like(s_scr)
        else:
            z = jnp.zeros((GDN_DK, GDN_DV), F32)
            for e in range(n_elems):
                for d in range(2):
                    for p in range(PAIRS):
                        top = jnp.concatenate([h0_ref[e, d, 2 * p], z], axis=1)
                        bot = jnp.concatenate([z, h0_ref[e, d, 2 * p + 1]], axis=1)
                        s_scr[e, d, p] = jnp.concatenate([top, bot], axis=0)

    ri = lax.broadcasted_iota(jnp.int32, (n, n), 0)
    ci = lax.broadcasted_iota(jnp.int32, (n, n), 1)
    same_head = (ri // cs) == (ci // cs)
    eye = (ri == ci).astype(F32)
    blk16 = (ri // 16) == (ci // 16)
    blk32 = (ri // 32) == (ci // 32)
    lane_lo = lax.broadcasted_iota(jnp.int32, (1, n), 1) < cs
    ti = lax.broadcasted_iota(jnp.int32, (cs, cs), 0)
    tj = lax.broadcasted_iota(jnp.int32, (cs, cs), 1)

    chains = [(e, d, p) for e in range(n_elems) for d in range(2) for p in range(PAIRS)]
    incl_d = [same_head & (ri >= ci), same_head & (ri <= ci)]
    strict_d = [same_head & (ri > ci), same_head & (ri < ci)]
    gates_d, gc_d, gct_d, glast_d = {}, {}, {}, {}
    for e in range(n_elems):
        for d in range(2):
            gates = dir_in[d][3][e]
            csum = ((ti >= tj) if d == 0 else (ti <= tj)).astype(BF16)
            ghi, gmid, glo = _split3(gates)
            gc_all = _dot(csum, ghi) + _dot(csum, gmid) + _dot(csum, glo)
            gates_d[e, d] = gates
            gc_d[e, d] = gc_all
            gct_d[e, d] = gc_all.T
            glast_d[e, d] = gc_all[cs - 1:cs, :] if d == 0 else gc_all[0:1, :]

    kst, qst, vst, gcol, bcol, glast, decay, egc = [], [], [], [], [], [], [], []
    for e, d, p in chains:
        q_ref, k_ref, v_ref, _ = dir_in[d]
        sl = slice(p * n, (p + 1) * n)
        lg = GATE_LANE + HEADS * d + 2 * p
        kst.append(_stack_pair(k_ref[e, :, sl], lane_lo))
        qst.append(_stack_pair(q_ref[e, :, sl], lane_lo))
        vst.append(_stack_pair(v_ref[e, :, sl], lane_lo))
        gc = _pair_col(gc_d[e, d], lg)
        gr = jnp.concatenate([gct_d[e, d][lg:lg + 1, :], gct_d[e, d][lg + 1:lg + 2, :]], axis=1)
        gcol.append(gc)
        bcol.append(_pair_col(gates_d[e, d], lg + 2 * HEADS))
        glast.append(jnp.concatenate([jnp.broadcast_to(glast_d[e, d][:, lg:lg + 1], (cs, 1)),
                                      jnp.broadcast_to(glast_d[e, d][:, lg + 1:lg + 2], (cs, 1))], axis=0))
        decay.append(jnp.exp(jnp.where(incl_d[d], gc - gr, -jnp.inf)))
        egc.append(jnp.exp(gc))
    nch = len(chains)
    kstb = [x.astype(BF16) for x in kst]
    gp = [_dot_nt(jnp.concatenate([kstb[i], qst[i].astype(BF16)], axis=0), kstb[i]) for i in range(nch)]
    a_mat = [jnp.where(strict_d[chains[i][1]], bcol[i] * gp[i][:n] * decay[i], 0.0) for i in range(nch)]
    attn = [jnp.where(incl_d[chains[i][1]], gp[i][n:] * decay[i], 0.0).astype(BF16) for i in range(nch)]
    rhs = [jnp.concatenate([vst[i] * bcol[i], kst[i] * (bcol[i] * egc[i])], axis=1).astype(BF16) for i in range(nch)]
    t_inv = _unit_tri_inverse(a_mat, eye, blk16, blk32)
    x = [_dot(t_inv[i].astype(BF16), rhs[i]) for i in range(nch)]
    s_old = [s_scr[e, d, p] for e, d, p in chains]
    sb = [s.astype(BF16) for s in s_old]
    ws = [_dot(jnp.concatenate([x[i][:, n:], qst[i] * egc[i]], axis=0).astype(BF16), sb[i]) for i in range(nch)]
    v_new = [(x[i][:, :n] - ws[i][:n]).astype(BF16) for i in range(nch)]
    o_st = [ws[i][n:] + _dot(attn[i], v_new[i]) for i in range(nch)]
    k_dec_t = [(kst[i] * jnp.exp(glast[i] - gcol[i])).T.astype(BF16) for i in range(nch)]
    for i, (e, d, p) in enumerate(chains):
        s_scr[e, d, p] = s_old[i] * jnp.exp(glast[i]) + _dot(k_dec_t[i], v_new[i])
    for e in range(n_elems):
        for d in range(2):
            o_pairs = [o_st[chains.index((e, d, p))] for p in range(PAIRS)]
            o_refs[d][e] = jnp.concatenate([o[:cs] + o[cs:] for o in o_pairs], axis=1)

    if emit_state:
        @pl.when(c == n_chunks - 1)
        def _():
            for e in range(n_elems):
                for d in range(2):
                    for p in range(PAIRS):
                        s = s_scr[e, d, p]
                        hfin_ref[e, d, 2 * p] = s[:GDN_DK, :GDN_DV]
                        hfin_ref[e, d, 2 * p + 1] = s[GDN_DK:, GDN_DV:]


GDN_ELEMS = 4


def _gdn(qg, kg, vg, gb, h0, n_batch, seq, states=None, l=0):
    nc = seq // GDN_CHUNK
    ne = GDN_ELEMS
    assert n_batch % ne == 0
    zero_init = h0 is None
    emit_state = states is not None
    fwd = lambda w: pl.BlockSpec((ne, GDN_CHUNK, w), lambda b, c: (b, c, 0))
    bwd = lambda w: pl.BlockSpec((ne, GDN_CHUNK, w), lambda b, c: (b, nc - 1 - c, 0))
    in_specs = [fwd(GDN_W), fwd(GDN_W), fwd(GDN_W), fwd(LANES), bwd(GDN_W), bwd(GDN_W), bwd(GDN_W), bwd(LANES)]
    seq3 = lambda a: a.reshape(n_batch, seq, a.shape[-1])
    args = [seq3(qg), seq3(kg), seq3(vg), seq3(gb)] * 2
    if not zero_init:
        h0_all, h0_layer = h0
        in_specs.append(pl.BlockSpec((ne, None, 2, HEADS, GDN_DK, GDN_DV), lambda b, c: (b, h0_layer, 0, 0, 0, 0)))
        args.append(h0_all)
    t = n_batch * seq
    out_shape = [jax.ShapeDtypeStruct((n_batch, seq, GDN_W), F32), jax.ShapeDtypeStruct((n_batch, seq, GDN_W), F32)]
    out_specs = [fwd(GDN_W), bwd(GDN_W)]
    aliases = {}
    if emit_state:
        aliases[len(args)] = len(out_shape)
        in_specs.append(pl.BlockSpec(memory_space=pl.ANY))
        args.append(states)
        out_shape.append(jax.ShapeDtypeStruct(states.shape, states.dtype))
        out_specs.append(pl.BlockSpec((ne, None, 2, HEADS, GDN_DK, GDN_DV), lambda b, c: (b, l, 0, 0, 0, 0)))
    outs = pl.pallas_call(
        functools.partial(_gdn_kernel, zero_init=zero_init, emit_state=emit_state, n_chunks=nc, n_elems=ne),
        out_shape=out_shape,
        grid=(n_batch // ne, nc),
        in_specs=in_specs,
        out_specs=out_specs,
        scratch_shapes=[pltpu.VMEM((ne, 2, PAIRS, PAIR_N, PAIR_N), F32)],
        input_output_aliases=aliases,
        compiler_params=_cparams(("parallel", "arbitrary")),
        name="gdn",
    )(*args)
    return outs[0].reshape(t, GDN_W), outs[1].reshape(t, GDN_W), (outs[2] if emit_state else None)


MERGE_TM = 512


def _merge_kernel(x_ref, mod_ref, oa_ref, gz_ref, conv_ref, cx_prev_ref, cx_next_ref, of_ref, ob_ref, mg_ref,
                  wpa_ref, wpb_ref, wpc_ref, wo_ref, cw_ref, gn_ref, ones_ref, fin_ref, o_ref, *,
                  seq, final_norm):
    edges = _seq_edges(x_ref.shape[0], seq, pl.program_id(0))
    o_a = oa_ref[...].astype(F32) * _silu(gz_ref[:, 0:MLA_W].astype(F32))
    pa = _dot(o_a.astype(BF16), wpa_ref[...])
    f32 = lambda ref, rows, c0: ref[rows, c0:c0 + CONV_W].astype(F32)
    every = slice(None)
    cx = f32(conv_ref, every, 0) * f32(conv_ref, every, CONV_W)
    pr, nr = slice(HALO - 1, HALO), slice(0, 1)
    prev_row = f32(cx_prev_ref, pr, 0) * f32(cx_prev_ref, pr, CONV_W)
    next_row = f32(cx_next_ref, nr, 0) * f32(cx_next_ref, nr, CONV_W)
    down, up = _shift_rows(cx, prev_row, next_row, edges)
    conv = down * cw_ref[0:1, :] + cx * cw_ref[1:2, :] + up * cw_ref[2:3, :]
    o_b = f32(conv_ref, every, 2 * CONV_W) * conv * _silu(f32(conv_ref, every, 3 * CONV_W))
    pb = _dot(o_b.astype(BF16), wpb_ref[...])
    og = of_ref[...] + ob_ref[...]
    ms = _head_sums(og * og, ones_ref[...]) * (1.0 / GDN_DV)
    o_c = og * lax.rsqrt(ms + EPS) * gn_ref[...] * _silu(gz_ref[:, MLA_W:W_GZ].astype(F32))
    pc = _dot(o_c.astype(BF16), wpc_ref[...])
    m = (_sigmoid(mg_ref[:, 0:D_MODEL].astype(F32)) * pa + _sigmoid(mg_ref[:, D_MODEL:2 * D_MODEL].astype(F32)) * pb
         + _sigmoid(mg_ref[:, 2 * D_MODEL:3 * D_MODEL].astype(F32)) * pc)
    r = _dot(m.astype(BF16), wo_ref[...])
    y = x_ref[...] + mod_ref[:, 2 * D_MODEL:3 * D_MODEL] * r
    if final_norm:
        y = _rms(y) * fin_ref[...]
    o_ref[...] = y


def _merge(x2d, mod, o_a, u_gz, u_conv, o_f, o_b, u_mg, lw, l, final_g, seq, cond_row0, cond_per_seq, final_norm):
    t = x2d.shape[0]
    tm = MERGE_TM
    assert t % tm == 0 and (seq % tm == 0 or tm % seq == 0) and (cond_per_seq == 0 or seq % tm == 0)
    tps = max(seq // tm, 1)
    row = lambda w: pl.BlockSpec((tm, w), lambda i: (i, 0))
    prev_spec, next_spec = _halo_specs(tm, 2 * CONV_W, t)
    names = ["w_pa", "w_pb", "w_pc", "w_o", "conv_b_w", "gdn_norm_lane"]
    return pl.pallas_call(
        functools.partial(_merge_kernel, seq=seq, final_norm=final_norm),
        out_shape=jax.ShapeDtypeStruct((t, D_MODEL), F32),
        grid=(t // tm,),
        in_specs=[
            row(D_MODEL),
            _mod_spec(l, lambda i: cond_row0 + (i // tps) * cond_per_seq),
            row(MLA_W), row(W_GZ), row(W_CONV), prev_spec, next_spec, row(GDN_W), row(GDN_W), row(W_MG),
        ] + [_layer_spec(lw[k], l) for k in names] + [_const_spec((GDN_W, GDN_W)), _const_spec((1, D_MODEL))],
        out_specs=row(D_MODEL),
        compiler_params=_cparams(("parallel",)),
        name="merge",
    )(x2d, mod, o_a, u_gz, u_conv, u_conv, u_conv, o_f, o_b, u_mg, *[lw[k] for k in names], lw["ones_bd"],
      final_g.reshape(1, D_MODEL))


_EVEN_ODD = list(range(0, QK_ROPE, 2)) + list(range(1, QK_ROPE, 2))
_SWAPPED = list(range(1, QK_ROPE, 2)) + list(range(0, QK_ROPE, 2))


REPACK_TN = 256
_PACK_PIECES = ((_O_CQ, _O_KPE), None,
                (_O_Q, _O_Z), (_O_C, _O_GB), (_O_B, _O_C), (_O_GB, _O_Q), (_O_GA, _O_B), (_O_Z, _O_AB),
                (_O_MG, _O_END))


def _repack_kernel(wt_ref, small_ref, o_ref):
    off = 0
    for piece in _PACK_PIECES:
        xt = small_ref[...] if piece is None else wt_ref[piece[0]:piece[1], :]
        o_ref[:, off:off + xt.shape[0]] = xt.T.astype(BF16)
        off += xt.shape[0]


def _repack_w_in(w_in):
    depth = w_in.shape[0]
    wt = jnp.swapaxes(w_in, 1, 2)
    kpe = wt[:, _O_KPE:_O_GA]
    small = jnp.concatenate([kpe[:, 0::2], kpe[:, 1::2], kpe[:, 1::2], kpe[:, 0::2], wt[:, _O_AB:_O_MG], kpe], axis=1)
    return pl.pallas_call(
        _repack_kernel,
        out_shape=jax.ShapeDtypeStruct((depth, D_MODEL, W_IN_PACKED), BF16),
        grid=(depth, D_MODEL // REPACK_TN),
        in_specs=[pl.BlockSpec((None, _O_END, REPACK_TN), lambda l, i: (l, 0, i)),
                  pl.BlockSpec((None, LANES, REPACK_TN), lambda l, i: (l, 0, i))],
        out_specs=pl.BlockSpec((None, REPACK_TN, W_IN_PACKED), lambda l, i: (l, i, 0)),
        compiler_params=_cparams(("parallel", "parallel")),
        name="repack_w_in",
    )(wt, small)


def _pack_weights(w_in, q_norm_g, kv_norm_g, w_uq, w_ukv, conv_b_w, conv_qkv_w, a_log, dt_bias, gdn_norm_g,
                  w_pa, w_pb, w_pc, w_o, norm_g):
    depth = w_in.shape[0]
    eo = np.array(_EVEN_ODD)
    sw = np.array(_SWAPPED)
    w_in_p = _repack_w_in(w_in)
    wq = w_uq.astype(BF16).reshape(depth, Q_LORA, HEADS, QK_NOPE + QK_ROPE)
    nope, rope = wq[..., :QK_NOPE], wq[..., QK_NOPE:]
    z32 = jnp.zeros((depth, Q_LORA, HEADS, HEAD_PAD - QK_NOPE - QK_ROPE), BF16)
    z64 = jnp.zeros((depth, Q_LORA, HEADS, QK_NOPE), BF16)
    w_qa = jnp.concatenate([nope, rope[..., eo], z32], axis=-1).reshape(depth, Q_LORA, QK_W)
    w_qb = jnp.concatenate([z64, rope[..., sw], z32], axis=-1).reshape(depth, Q_LORA, QK_W)
    wkv = w_ukv.astype(BF16).reshape(depth, KV_LORA, HEADS, QK_NOPE + V_HEAD)
    w_k = jnp.concatenate([wkv[..., :QK_NOPE], jnp.zeros((depth, KV_LORA, HEADS, HEAD_PAD - QK_NOPE), BF16)],
                          axis=-1).reshape(depth, KV_LORA, QK_W)
    w_v = wkv[..., QK_NOPE:].reshape(depth, KV_LORA, MLA_W)
    src = np.arange(LANES)[:, None]
    dst = np.arange(QK_W)[None, :] % HEAD_PAD - QK_NOPE
    in_rope = (dst >= 0) & (dst < QK_ROPE)
    e_plain = jnp.asarray(np.broadcast_to(in_rope & (src == dst), (depth, LANES, QK_W)), BF16)
    e_rope = jnp.asarray(np.broadcast_to(in_rope & ((src == dst) | (src == dst + QK_ROPE)), (depth, LANES, QK_W)), BF16)
    gate_pad = ((0, 0), (0, 0), (GATE_LANE, LANES - GATE_LANE - 2 * HEADS))
    hid = np.arange(GDN_W) // GDN_DK
    return {
        "w_in_p": w_in_p, "norm_g": norm_g.reshape(depth, 1, D_MODEL),
        "q_norm_g": q_norm_g.reshape(depth, 1, Q_LORA), "kv_norm_g": kv_norm_g.reshape(depth, 1, KV_LORA),
        "w_qa": w_qa, "w_qb": w_qb,
        "w_ke": jnp.concatenate([w_k, e_plain], axis=1), "w_ke_rope": jnp.concatenate([w_k, e_rope], axis=1),
        "w_v": w_v,
        "conv_qkv_w": conv_qkv_w, "conv_b_w": conv_b_w,
        "alog_lane": jnp.pad(a_log.reshape(depth, 1, 2 * HEADS), gate_pad),
        "dtb_lane": jnp.pad(dt_bias.reshape(depth, 1, 2 * HEADS), gate_pad),
        "ones_bd": jnp.asarray(hid[:, None] == hid[None, :], BF16),
        "gdn_norm_lane": jnp.tile(gdn_norm_g, (1, HEADS)).reshape(depth, 1, GDN_W),
        "w_pa": w_pa.astype(BF16), "w_pb": w_pb.astype(BF16), "w_pc": w_pc.astype(BF16), "w_o": w_o.astype(BF16),
    }


def _rope_tables(n_tokens):
    t = np.arange(n_tokens)
    row = (t // GRID_W).astype(np.float32)
    col = (t % GRID_W).astype(np.float32)
    n_freq = QK_ROPE // 4
    inv_freq = (np.float32(ROPE_THETA) ** (-np.arange(n_freq, dtype=np.float32) / n_freq)).astype(np.float32)
    ang = np.concatenate([row[:, None] * inv_freq, col[:, None] * inv_freq], axis=-1).astype(np.float32)
    cos, sin = np.cos(ang), np.sin(ang)
    cc = np.concatenate([cos, cos], axis=-1)
    ss = np.concatenate([-sin, sin], axis=-1)
    one = np.ones((n_tokens, QK_NOPE), np.float32)
    z32 = np.zeros((n_tokens, QK_ROPE), np.float32)
    z64 = np.zeros((n_tokens, QK_NOPE), np.float32)
    m1 = np.concatenate([one, cc, z32], axis=-1) * np.float32(SM_SCALE)
    m2 = np.concatenate([z64, ss, z32], axis=-1) * np.float32(SM_SCALE)
    m3 = np.concatenate([cc, ss, z64], axis=-1)
    return tuple(jnp.asarray(m, F32) for m in (m1, m2, m3))


def _layer(x2d, mod, lw, l, final_g, n_batch, seq, cond_row0, cond_per_seq, final_norm, rope_tabs=None,
           ctx_kv=None, h0=None, tq=256, attn_heads=HEADS, new_caches=None):
    is_ctx = ctx_kv is None
    u_mla, u_qkv, u_conv, u_gz, u_mg = _inproj(x2d, mod, lw, l, seq, cond_row0, cond_per_seq)
    outs = _prep(u_mla, u_qkv, lw, l, seq, rope_tabs, new_caches[:2] if is_ctx else None)
    if is_ctx:
        qf, kf, vf, ckv_new, kpe_new, qg, kg, vg, gb = outs
        kv_parts = [(kf, vf, seq)]
    else:
        qf, kf, vf, qg, kg, vg, gb = outs
        kv_parts = [ctx_kv, (kf, vf, seq)]
    o_a = _attention(qf, kv_parts, n_batch, seq, tq, attn_heads)
    o_f, o_b, states_new = _gdn(qg, kg, vg, gb, h0, n_batch, seq, new_caches[2] if is_ctx else None, l)
    y = _merge(x2d, mod, o_a, u_gz, u_conv, o_f, o_b, u_mg, lw, l, final_g, seq, cond_row0, cond_per_seq, final_norm)
    return y, ((ckv_new, kpe_new, states_new) if is_ctx else None)


def kernel(x_prompt, x_sample, c, cache_ckv, cache_kpe, state_gdn, c_ctx, norm_g, w_ada, b_ada, w_in, q_norm_g,
           kv_norm_g, w_uq, w_ukv, conv_b_w, conv_qkv_w, a_log, dt_bias, gdn_norm_g, w_pa, w_pb, w_pc, w_o,
           final_norm_g):
    n_ctx, seq_ctx, _ = x_prompt.shape
    n_lat, seq_lat, _ = x_sample.shape
    past = cache_ckv.shape[2]
    depth = w_in.shape[0]
    assert n_lat + 1 <= COND_ROWS

    cond = jnp.concatenate([c, c_ctx[None, :], jnp.zeros((COND_ROWS - n_lat - 1, D_MODEL), F32)], axis=0)
    mod = _ada_mod(cond, w_ada, b_ada).reshape(depth, COND_ROWS, 1, 3 * D_MODEL)
    rope_tabs = _rope_tables(seq_lat)
    lw = _pack_weights(w_in, q_norm_g, kv_norm_g, w_uq, w_ukv, conv_b_w, conv_qkv_w, a_log, dt_bias, gdn_norm_g,
                       w_pa, w_pb, w_pc, w_o, norm_g)
    kpe_ctx = jnp.pad(cache_kpe[..., np.array(_EVEN_ODD)], ((0, 0),) * 3 + ((0, LANES - QK_ROPE),))

    xp = x_prompt.reshape(n_ctx * seq_ctx, D_MODEL)
    xs = x_sample.reshape(n_lat * seq_lat, D_MODEL)
    new_caches = (jnp.zeros((n_ctx, depth, seq_ctx, KV_LORA), F32), jnp.zeros((n_ctx, depth, seq_ctx, QK_ROPE), F32),
                  jnp.zeros((n_ctx, depth, 2, HEADS, GDN_DK, GDN_DV), F32))
    for l in range(depth):
        last = l == depth - 1
        xp, new_caches = _layer(xp, mod, lw, l, final_norm_g, n_ctx, seq_ctx, n_lat, 0, last, new_caches=new_caches)
        kc, vc = _ctx_kv(cache_ckv, kpe_ctx, lw, l)
        xs, _ = _layer(xs, mod, lw, l, final_norm_g, n_lat, seq_lat, 0, 1, last, rope_tabs=rope_tabs,
                       ctx_kv=(kc, vc, past), h0=(state_gdn, l), tq=1024, attn_heads=2)
    return (xp.reshape(n_ctx, seq_ctx, D_MODEL), xs.reshape(n_lat, seq_lat, D_MODEL)) + tuple(new_caches)
```

```python
import functools
import math

import jax
import jax.numpy as jnp
import numpy as np
from jax import lax
from jax.experimental import pallas as pl
from jax.experimental.pallas import tpu as pltpu

F32 = jnp.float32
BF16 = jnp.bfloat16

D_MODEL = 1024
DEPTH = 2
GRID_W = 64
EPS = 1e-6
HEADS = 8
Q_LORA = 384
KV_LORA = 256
QK_NOPE = 64
QK_ROPE = 32
V_HEAD = 64
MLA_W = HEADS * V_HEAD
ROPE_THETA = 10000.0
CONV_W = 512
GDN_DK = 64
GDN_DV = 64
GDN_W = HEADS * GDN_DK
GDN_CHUNK = 64
LANES = 128
HEAD_PAD = 128
QK_W = HEADS * HEAD_PAD
SM_SCALE = (QK_NOPE + QK_ROPE) ** -0.5
COND_ROWS = 16
VMEM_LIMIT = 56 * 1024 * 1024

_O_CQ, _O_CKV, _O_KPE, _O_GA = 0, 384, 640, 672
_O_B, _O_C, _O_X, _O_GB = 1184, 1696, 2208, 2720
_O_Q, _O_K, _O_V, _O_Z = 3232, 3744, 4256, 4768
_O_AB, _O_MG, _O_END = 5280, 5312, 8384
W_MLA = Q_LORA + KV_LORA + LANES
W_QKV = 3 * GDN_W
W_CONV = 4 * CONV_W
W_GZ = MLA_W + GDN_W
W_MG = 3 * D_MODEL
GROUP_WIDTHS = (W_MLA, W_QKV, W_CONV, W_GZ, W_MG)
GROUP_DTYPES = (F32, BF16, BF16, BF16, BF16)
W_IN_PACKED = sum(GROUP_WIDTHS)
GATE_LANE = 64


def _sigmoid(x):
    return 1.0 / (1.0 + jnp.exp(-x))


def _silu(x):
    return x * _sigmoid(x)


def _rms(x):
    return x * lax.rsqrt(jnp.mean(x * x, axis=-1, keepdims=True) + EPS)


def _dot(a, b):
    return jnp.dot(a, b, preferred_element_type=F32)


def _dot_nt(a, b):
    return lax.dot_general(a, b, (((1,), (1,)), ((), ())), preferred_element_type=F32)


def _split3(x):
    hi = x.astype(BF16)
    r = x - hi.astype(F32)
    mid = r.astype(BF16)
    lo = (r - mid.astype(F32)).astype(BF16)
    return hi, mid, lo


def _head_sums(x2, ones_bd):
    return _dot(x2.astype(BF16), ones_bd)


def _cparams(sem):
    return pltpu.CompilerParams(dimension_semantics=sem, vmem_limit_bytes=VMEM_LIMIT)


def _const_spec(shape):
    nd = len(shape)
    return pl.BlockSpec(shape, lambda *_: (0,) * nd)


def _mod_spec(l, cond_row):
    return pl.BlockSpec((None, None, 1, 3 * D_MODEL), lambda i: (l, cond_row(i), 0, 0))


def _layer_spec(arr, l, **kw):
    nd = arr.ndim - 1
    return pl.BlockSpec((None,) + arr.shape[1:], lambda *_: (l,) + (0,) * nd, **kw)


def _ada_kernel(c_ref, w_ref, b_ref, o_ref):
    sc = _silu(c_ref[...])
    o_ref[...] = jnp.dot(sc, w_ref[...], preferred_element_type=F32, precision=lax.Precision.HIGHEST) + b_ref[...]


def _ada_mod(cond, w_ada, b_ada):
    depth = w_ada.shape[0]
    return pl.pallas_call(
        _ada_kernel,
        out_shape=jax.ShapeDtypeStruct((depth, COND_ROWS, 3 * D_MODEL), F32),
        grid=(depth, 3),
        in_specs=[
            pl.BlockSpec((COND_ROWS, D_MODEL), lambda l, j: (0, 0)),
            pl.BlockSpec((None, D_MODEL, D_MODEL), lambda l, j: (l, 0, j)),
            pl.BlockSpec((None, 1, D_MODEL), lambda l, j: (l, 0, j)),
        ],
        out_specs=pl.BlockSpec((None, COND_ROWS, D_MODEL), lambda l, j: (l, 0, j)),
        compiler_params=_cparams(("arbitrary", "arbitrary")),
        name="ada_mod",
    )(cond, w_ada, b_ada.reshape(depth, 1, 3 * D_MODEL))


IN_TM = 512
IN_CHUNK = 768


def _inproj_kernel(x_ref, mod_ref, g_ref, w_ref, *rest):
    outs, h_scr = rest[:-1], rest[-1]
    x = x_ref[...]
    shift = mod_ref[:, 0:D_MODEL]
    scale = mod_ref[:, D_MODEL:2 * D_MODEL]
    h = (_rms(x) * g_ref[...]) * (1.0 + scale) + shift
    h_scr[...] = h.astype(BF16)
    off = 0
    for o_ref in outs:
        n = o_ref.shape[1]
        for c0 in range(0, n, IN_CHUNK):
            c1 = min(c0 + IN_CHUNK, n)
            o_ref[:, c0:c1] = _dot(h_scr[...], w_ref[:, off + c0:off + c1]).astype(o_ref.dtype)
        off += n


def _inproj(x2d, mod, lw, l, seq, cond_row0, cond_per_seq):
    t = x2d.shape[0]
    assert t % IN_TM == 0 and (cond_per_seq == 0 or seq % IN_TM == 0)
    tps = max(seq // IN_TM, 1)
    return pl.pallas_call(
        _inproj_kernel,
        out_shape=[jax.ShapeDtypeStruct((t, n), dt) for n, dt in zip(GROUP_WIDTHS, GROUP_DTYPES)],
        grid=(t // IN_TM,),
        in_specs=[
            pl.BlockSpec((IN_TM, D_MODEL), lambda i: (i, 0)),
            _mod_spec(l, lambda i: cond_row0 + (i // tps) * cond_per_seq),
            _layer_spec(lw["norm_g"], l),
            _layer_spec(lw["w_in_p"], l, pipeline_mode=pl.Buffered(1)),
        ],
        out_specs=[pl.BlockSpec((IN_TM, n), lambda i: (i, 0)) for n in GROUP_WIDTHS],
        scratch_shapes=[pltpu.VMEM((IN_TM, D_MODEL), BF16)],
        compiler_params=_cparams(("parallel",)),
        name="inproj",
    )(x2d, mod, lw["norm_g"], lw["w_in_p"])


PREP_TM = 512


def _seq_edges(tm, seq, i):
    row = lax.broadcasted_iota(jnp.int32, (tm, 1), 0)
    if seq >= tm:
        tps = seq // tm
        return (row == 0) & ((i % tps) == 0), (row == tm - 1) & ((i % tps) == tps - 1)
    pos = lax.rem(row, seq)
    return pos == 0, pos == seq - 1


def _shift_rows(x, prev_row, next_row, edges):
    n = x.shape[0]
    row = lax.broadcasted_iota(jnp.int32, (n, 1), 0)
    starts, ends = edges
    down = jnp.where(starts, 0.0, jnp.where(row == 0, prev_row, pltpu.roll(x, 1, 0)))
    up = jnp.where(ends, 0.0, jnp.where(row == n - 1, next_row, pltpu.roll(x, n - 1, 0)))
    return down, up


def _store_keys(kf_ref, k_nope, k_rope):
    lane = lax.broadcasted_iota(jnp.int32, (1, HEAD_PAD), 1)
    k_rope = jnp.where((lane >= QK_NOPE) & (lane < QK_NOPE + QK_ROPE), k_rope, 0.0)
    for h in range(HEADS):
        sl = slice(h * HEAD_PAD, (h + 1) * HEAD_PAD)
        kf_ref[:, sl] = (k_nope[:, sl] + k_rope).astype(BF16)


def _prep_kernel(*refs, use_rope, seq, emit_ckvn):
    it = iter(refs)
    mla_ref, qkv_ref, qkv_prev_ref, qkv_next_ref = next(it), next(it), next(it), next(it)
    if use_rope:
        m1_ref, m2_ref, m3_ref = next(it), next(it), next(it)
    gq_ref, gkv_ref, wa_ref = next(it), next(it), next(it)
    wk_ref, wv_ref, cw_ref, alog_ref, dtb_ref, ones_ref = (next(it) for _ in range(6))
    if emit_ckvn:
        next(it), next(it)
    qf_ref, kf_ref, vf_ref = next(it), next(it), next(it)
    if emit_ckvn:
        ckvn_ref, kpe_ref = next(it), next(it)
    qg_ref, kg_ref, vg_ref, gb_ref = next(it), next(it), next(it), next(it)

    edges = _seq_edges(mla_ref.shape[0], seq, pl.program_id(0))

    cq = mla_ref[:, 0:Q_LORA]
    ckv = mla_ref[:, Q_LORA:Q_LORA + KV_LORA]
    small = mla_ref[:, Q_LORA + KV_LORA:W_MLA]
    qn = (_rms(cq) * gq_ref[...]).astype(BF16)
    qa = _dot(qn, wa_ref[...])
    if use_rope:
        m1 = m1_ref[...]
        m2 = m2_ref[...]
        lane = lax.broadcasted_iota(jnp.int32, (1, HEAD_PAD), 1)
        half = QK_ROPE // 2
        first_half = lane < QK_NOPE + half
        for h in range(HEADS):
            sl = slice(h * HEAD_PAD, (h + 1) * HEAD_PAD)
            qh = qa[:, sl]
            partner = jnp.where(first_half, pltpu.roll(qh, HEAD_PAD - half, 1), pltpu.roll(qh, half, 1))
            qf_ref[:, sl] = (qh * m1 + partner * m2).astype(BF16)
        kpe_t = small * m3_ref[...]
        k_rope = pltpu.roll(kpe_t, QK_NOPE, 1) + pltpu.roll(kpe_t, QK_NOPE - QK_ROPE, 1)
    else:
        qf_ref[...] = (qa * SM_SCALE).astype(BF16)
        k_rope = pltpu.roll(small, QK_NOPE, 1)
    ckvn = _rms(ckv) * gkv_ref[...]
    if emit_ckvn:
        ckvn_ref[...] = ckvn.reshape(ckvn_ref.shape)
        kpe_ref[...] = small[:, LANES - QK_ROPE:].reshape(kpe_ref.shape)
    ckvn_b = ckvn.astype(BF16)
    _store_keys(kf_ref, _dot(ckvn_b, wk_ref[...]), k_rope)
    vf_ref[...] = _dot(ckvn_b, wv_ref[...]).astype(BF16)

    z1 = small + dtb_ref[...]
    sp = jnp.maximum(z1, 0.0) + jnp.log1p(jnp.exp(-jnp.abs(z1)))
    g = -jnp.exp(alog_ref[...]) * sp
    lane = lax.broadcasted_iota(jnp.int32, (1, LANES), 1)
    gb_ref[...] = jnp.where(lane < GATE_LANE + 2 * HEADS, g, _sigmoid(small))

    ones_bd = ones_ref[...]
    for part, o_ref in enumerate((qg_ref, kg_ref, vg_ref)):
        sl = slice(part * GDN_W, (part + 1) * GDN_W)
        x = qkv_ref[:, sl].astype(F32)
        prev_row = qkv_prev_ref[HALO - 1:HALO, sl].astype(F32)
        next_row = qkv_next_ref[0:1, sl].astype(F32)
        down, up = _shift_rows(x, prev_row, next_row, edges)
        y = down * cw_ref[0:1, sl] + x * cw_ref[1:2, sl] + up * cw_ref[2:3, sl]
        y = _silu(y)
        if part == 0:
            y = y * lax.rsqrt(_head_sums(y * y, ones_bd) + EPS) * (GDN_DK ** -0.5)
        elif part == 1:
            y = y * lax.rsqrt(_head_sums(y * y, ones_bd) + EPS)
        o_ref[...] = y


HALO = 16


def _halo_specs(tm, width, n_rows, col_block=0):
    r = tm // HALO
    nb = n_rows // HALO
    prev = pl.BlockSpec((HALO, width), lambda i: (jnp.maximum(i * r - 1, 0), col_block))
    nxt = pl.BlockSpec((HALO, width), lambda i: (jnp.minimum((i + 1) * r, nb - 1), col_block))
    return prev, nxt


def _stacked_spec(tm, seq, width, l):
    if tm >= seq:
        return pl.BlockSpec((tm // seq, None, seq, width), lambda i: (i, l, 0, 0))
    tps = seq // tm
    return pl.BlockSpec((None, None, tm, width), lambda i: (i // tps, l, i % tps, 0))


def _prep(u_mla, u_qkv, lw, l, seq, rope_tabs, caches):
    t = u_mla.shape[0]
    tm = PREP_TM
    use_rope = rope_tabs is not None
    emit_ckvn = caches is not None
    assert t % tm == 0 and (seq % tm == 0 or tm % seq == 0) and (not use_rope or seq % tm == 0)
    tps = max(seq // tm, 1)
    row = lambda w: pl.BlockSpec((tm, w), lambda i: (i, 0))
    prev_spec, next_spec = _halo_specs(tm, W_QKV, t)
    in_specs = [row(W_MLA), row(W_QKV), prev_spec, next_spec]
    args = [u_mla, u_qkv, u_qkv, u_qkv]
    if use_rope:
        in_specs += [pl.BlockSpec((tm, LANES), lambda i: (i % tps, 0))] * 3
        args += list(rope_tabs)
    names = ["q_norm_g", "kv_norm_g", "w_qa", "w_k", "w_v", "conv_qkv_w", "alog_lane", "dtb_lane"]
    in_specs += [_layer_spec(lw[k], l) for k in names] + [_const_spec(lw["ones_bd"].shape)]
    args += [lw[k] for k in names] + [lw["ones_bd"]]
    out_shape = [jax.ShapeDtypeStruct((t, QK_W), BF16), jax.ShapeDtypeStruct((t, QK_W), BF16),
                 jax.ShapeDtypeStruct((t, MLA_W), BF16)]
    out_specs = [row(QK_W), row(QK_W), row(MLA_W)]
    aliases = {}
    if emit_ckvn:
        for a in caches:
            aliases[len(args)] = len(out_shape)
            in_specs.append(pl.BlockSpec(memory_space=pl.ANY))
            args.append(a)
            out_shape.append(jax.ShapeDtypeStruct(a.shape, a.dtype))
            out_specs.append(_stacked_spec(tm, seq, a.shape[-1], l))
    out_shape += [jax.ShapeDtypeStruct((t, GDN_W), F32)] * 3 + [jax.ShapeDtypeStruct((t, LANES), F32)]
    out_specs += [row(GDN_W)] * 3 + [row(LANES)]
    outs = pl.pallas_call(
        functools.partial(_prep_kernel, use_rope=use_rope, seq=seq, emit_ckvn=emit_ckvn),
        out_shape=out_shape,
        grid=(t // tm,),
        in_specs=in_specs,
        out_specs=out_specs,
        input_output_aliases=aliases,
        compiler_params=_cparams(("parallel",)),
        name="prep",
    )(*args)
    return outs


def _ctxkv_kernel(ckv_ref, kpe_ref, wk_ref, wv_ref, kf_ref, vf_ref):
    ckv_b = ckv_ref[...].astype(BF16)
    _store_keys(kf_ref, _dot(ckv_b, wk_ref[...]), pltpu.roll(kpe_ref[...], QK_NOPE, 1))
    vf_ref[...] = _dot(ckv_b, wv_ref[...]).astype(BF16)


def _ctx_kv(cache_ckv, kpe_lanes, lw, l):
    n_batch, _, past, _ = cache_ckv.shape
    cached = lambda w: pl.BlockSpec((None, None, past, w), lambda i: (i, l, 0, 0))
    row = lambda w: pl.BlockSpec((past, w), lambda i: (i, 0))
    return pl.pallas_call(
        _ctxkv_kernel,
        out_shape=[jax.ShapeDtypeStruct((n_batch * past, QK_W), BF16), jax.ShapeDtypeStruct((n_batch * past, MLA_W), BF16)],
        grid=(n_batch,),
        in_specs=[cached(KV_LORA), cached(LANES), _layer_spec(lw["w_k"], l), _layer_spec(lw["w_v"], l)],
        out_specs=[row(QK_W), row(MLA_W)],
        compiler_params=_cparams(("parallel",)),
        name="ctx_kv",
    )(cache_ckv, kpe_lanes, lw["w_k"], lw["w_v"])


def _attn_kernel(*refs, n_parts, n_heads):
    q_ref = refs[0]
    k_refs = refs[1:1 + n_parts]
    v_refs = refs[1 + n_parts:1 + 2 * n_parts]
    o_ref = refs[1 + 2 * n_parts]
    lane = lax.broadcasted_iota(jnp.int32, (1, 2 * V_HEAD), 1)
    own = [lane < V_HEAD, lane >= V_HEAD]
    heads = range(n_heads)
    hl = [slice(j * HEAD_PAD, (j + 1) * HEAD_PAD) for j in heads]
    pl_ = [slice((j // 2) * 2 * V_HEAD, (j // 2 + 1) * 2 * V_HEAD) for j in heads]
    s = [[_dot_nt(q_ref[:, hl[j]], k_ref[:, hl[j]]) for k_ref in k_refs] for j in heads]
    m = []
    for j in heads:
        mj = s[j][0].max(axis=-1, keepdims=True)
        for sp in s[j][1:]:
            mj = jnp.maximum(mj, sp.max(axis=-1, keepdims=True))
        m.append(mj)
    p = [[jnp.exp(sp - m[j]).astype(BF16) for sp in s[j]] for j in heads]
    acc = []
    for j in heads:
        a = None
        for pj, v_ref in zip(p[j], v_refs):
            t = _dot(pj, jnp.where(own[j % 2], v_ref[:, pl_[j]], 1.0).astype(BF16))
            a = t if a is None else a + t
        acc.append(a)
    for j in range(0, n_heads, 2):
        l0 = acc[j][:, V_HEAD:V_HEAD + 1]
        l1 = acc[j + 1][:, 0:1]
        o_ref[:, pl_[j]] = jnp.where(own[0], acc[j] / l0, acc[j + 1] / l1).astype(o_ref.dtype)


def _attention(qf, kv_parts, n_batch, seq_q, tq, n_heads):
    n_parts = len(kv_parts)
    nq = seq_q // tq
    in_specs = [pl.BlockSpec((tq, n_heads * HEAD_PAD), lambda b, hp, qi: (b * nq + qi, hp))]
    in_specs += [pl.BlockSpec((sk, n_heads * HEAD_PAD), lambda b, hp, qi: (b, hp)) for _, _, sk in kv_parts]
    in_specs += [pl.BlockSpec((sk, n_heads * V_HEAD), lambda b, hp, qi: (b, hp)) for _, _, sk in kv_parts]
    return pl.pallas_call(
        functools.partial(_attn_kernel, n_parts=n_parts, n_heads=n_heads),
        out_shape=jax.ShapeDtypeStruct((n_batch * seq_q, MLA_W), BF16),
        grid=(n_batch, HEADS // n_heads, nq),
        in_specs=in_specs,
        out_specs=pl.BlockSpec((tq, n_heads * V_HEAD), lambda b, hp, qi: (b * nq + qi, hp)),
        compiler_params=_cparams(("parallel", "parallel", "arbitrary")),
        name="attention",
    )(qf, *[k for k, _, _ in kv_parts], *[v for _, v, _ in kv_parts])


PAIRS = HEADS // 2
PAIR_N = 2 * GDN_CHUNK


def _unit_tri_inverse(a_list, eye, blk16, blk32):
    diag = [jnp.where(blk16, a, 0.0) for a in a_list]
    powers = [[x.astype(BF16) for x in diag]]
    for _ in range(3):
        powers.append([_dot(x, x).astype(BF16) for x in powers[-1]])
    t = [eye - x for x in diag]
    for pw in powers[1:]:
        t = [x + _dot(x.astype(BF16), y) for x, y in zip(t, pw)]
    for off in ([jnp.where(blk32 & ~blk16, a, 0.0).astype(BF16) for a in a_list],
                [jnp.where(blk32, 0.0, a).astype(BF16) for a in a_list]):
        tb = [x.astype(BF16) for x in t]
        m = [_dot(x, y).astype(BF16) for x, y in zip(tb, off)]
        t = [x - _dot(y, z) for x, y, z in zip(t, m, tb)]
    return t


def _stack_pair(x, lane_lo):
    return jnp.concatenate([jnp.where(lane_lo, x, 0.0), jnp.where(lane_lo, 0.0, x)], axis=0)


def _pair_col(m, l0):
    return jnp.concatenate([m[:, l0:l0 + 1], m[:, l0 + 1:l0 + 2]], axis=0)


def _gdn_kernel(*refs, zero_init, emit_state, n_chunks, n_elems):
    it = iter(refs)
    dir_in = [tuple(next(it) for _ in range(4)) for _ in range(2)]
    h0_ref = None if zero_init else next(it)
    if emit_state:
        next(it)
    o_refs = (next(it), next(it))
    hfin_ref = next(it) if emit_state else None
    s_scr = next(it)
    c = pl.program_id(1)
    cs, n = GDN_CHUNK, PAIR_N

    @pl.when(c == 0)
    def _():
        if zero_init:
            s_scr[...] = jnp.zeros_like(s_scr)
        else:
            z = jnp.zeros((GDN_DK, GDN_DV), F32)
            for e in range(n_elems):
                for d in range(2):
                    for p in range(PAIRS):
                        top = jnp.concatenate([h0_ref[e, d, 2 * p], z], axis=1)
                        bot = jnp.concatenate([z, h0_ref[e, d, 2 * p + 1]], axis=1)
                        s_scr[e, d, p] = jnp.concatenate([top, bot], axis=0)

    ri = lax.broadcasted_iota(jnp.int32, (n, n), 0)
    ci = lax.broadcasted_iota(jnp.int32, (n, n), 1)
    same_head = (ri // cs) == (ci // cs)
    eye = (ri == ci).astype(F32)
    blk16 = (ri // 16) == (ci // 16)
    blk32 = (ri // 32) == (ci // 32)
    lane_lo = lax.broadcasted_iota(jnp.int32, (1, n), 1) < cs
    ti = lax.broadcasted_iota(jnp.int32, (cs, cs), 0)
    tj = lax.broadcasted_iota(jnp.int32, (cs, cs), 1)

    chains = [(e, d, p) for e in range(n_elems) for d in range(2) for p in range(PAIRS)]
    incl_d = [same_head & (ri >= ci), same_head & (ri <= ci)]
    strict_d = [same_head & (ri > ci), same_head & (ri < ci)]
    gates_d, gc_d, gct_d, glast_d = {}, {}, {}, {}
    for e in range(n_elems):
        for d in range(2):
            gates = dir_in[d][3][e]
            csum = ((ti >= tj) if d == 0 else (ti <= tj)).astype(BF16)
            ghi, gmid, glo = _split3(gates)
            gc_all = _dot(csum, ghi) + _dot(csum, gmid) + _dot(csum, glo)
            gates_d[e, d] = gates
            gc_d[e, d] = gc_all
            gct_d[e, d] = gc_all.T
            glast_d[e, d] = gc_all[cs - 1:cs, :] if d == 0 else gc_all[0:1, :]

    kst, qst, vst, gcol, bcol, glast, decay, egc = [], [], [], [], [], [], [], []
    for e, d, p in chains:
        q_ref, k_ref, v_ref, _ = dir_in[d]
        sl = slice(p * n, (p + 1) * n)
        lg = GATE_LANE + HEADS * d + 2 * p
        kst.append(_stack_pair(k_ref[e, :, sl], lane_lo))
        qst.append(_stack_pair(q_ref[e, :, sl], lane_lo))
        vst.append(_stack_pair(v_ref[e, :, sl], lane_lo))
        gc = _pair_col(gc_d[e, d], lg)
        gr = jnp.concatenate([gct_d[e, d][lg:lg + 1, :], gct_d[e, d][lg + 1:lg + 2, :]], axis=1)
        gcol.append(gc)
        bcol.append(_pair_col(gates_d[e, d], lg + 2 * HEADS))
        glast.append(jnp.concatenate([jnp.broadcast_to(glast_d[e, d][:, lg:lg + 1], (cs, 1)),
                                      jnp.broadcast_to(glast_d[e, d][:, lg + 1:lg + 2], (cs, 1))], axis=0))
        decay.append(jnp.exp(jnp.where(incl_d[d], gc - gr, -jnp.inf)))
        egc.append(jnp.exp(gc))
    nch = len(chains)
    kstb = [x.astype(BF16) for x in kst]
    gp = [_dot_nt(jnp.concatenate([kstb[i], qst[i].astype(BF16)], axis=0), kstb[i]) for i in range(nch)]
    a_mat = [jnp.where(strict_d[chains[i][1]], bcol[i] * gp[i][:n] * decay[i], 0.0) for i in range(nch)]
    attn = [jnp.where(incl_d[chains[i][1]], gp[i][n:] * decay[i], 0.0).astype(BF16) for i in range(nch)]
    rhs = [jnp.concatenate([vst[i] * bcol[i], kst[i] * (bcol[i] * egc[i])], axis=1).astype(BF16) for i in range(nch)]
    t_inv = _unit_tri_inverse(a_mat, eye, blk16, blk32)
    x = [_dot(t_inv[i].astype(BF16), rhs[i]) for i in range(nch)]
    s_old = [s_scr[e, d, p] for e, d, p in chains]
    sb = [s.astype(BF16) for s in s_old]
    ws = [_dot(jnp.concatenate([x[i][:, n:], qst[i] * egc[i]], axis=0).astype(BF16), sb[i]) for i in range(nch)]
    v_new = [(x[i][:, :n] - ws[i][:n]).astype(BF16) for i in range(nch)]
    o_st = [ws[i][n:] + _dot(attn[i], v_new[i]) for i in range(nch)]
    k_dec_t = [(kst[i] * jnp.exp(glast[i] - gcol[i])).T.astype(BF16) for i in range(nch)]
    for i, (e, d, p) in enumerate(chains):
        s_scr[e, d, p] = s_old[i] * jnp.exp(glast[i]) + _dot(k_dec_t[i], v_new[i])
    for e in range(n_elems):
        for d in range(2):
            o_pairs = [o_st[chains.index((e, d, p))] for p in range(PAIRS)]
            o_refs[d][e] = jnp.concatenate([o[:cs] + o[cs:] for o in o_pairs], axis=1)

    if emit_state:
        @pl.when(c == n_chunks - 1)
        def _():
            for e in range(n_elems):
                for d in range(2):
                    for p in range(PAIRS):
                        s = s_scr[e, d, p]
                        hfin_ref[e, d, 2 * p] = s[:GDN_DK, :GDN_DV]
                        hfin_ref[e, d, 2 * p + 1] = s[GDN_DK:, GDN_DV:]


GDN_ELEMS = 4


def _gdn(qg, kg, vg, gb, h0, n_batch, seq, states=None, l=0):
    nc = seq // GDN_CHUNK
    ne = GDN_ELEMS
    assert n_batch % ne == 0
    zero_init = h0 is None
    emit_state = states is not None
    fwd = lambda w: pl.BlockSpec((ne, GDN_CHUNK, w), lambda b, c: (b, c, 0))
    bwd = lambda w: pl.BlockSpec((ne, GDN_CHUNK, w), lambda b, c: (b, nc - 1 - c, 0))
    in_specs = [fwd(GDN_W), fwd(GDN_W), fwd(GDN_W), fwd(LANES), bwd(GDN_W), bwd(GDN_W), bwd(GDN_W), bwd(LANES)]
    seq3 = lambda a: a.reshape(n_batch, seq, a.shape[-1])
    args = [seq3(qg), seq3(kg), seq3(vg), seq3(gb)] * 2
    if not zero_init:
        h0_all, h0_layer = h0
        in_specs.append(pl.BlockSpec((ne, None, 2, HEADS, GDN_DK, GDN_DV), lambda b, c: (b, h0_layer, 0, 0, 0, 0)))
        args.append(h0_all)
    t = n_batch * seq
    out_shape = [jax.ShapeDtypeStruct((n_batch, seq, GDN_W), F32), jax.ShapeDtypeStruct((n_batch, seq, GDN_W), F32)]
    out_specs = [fwd(GDN_W), bwd(GDN_W)]
    aliases = {}
    if emit_state:
        aliases[len(args)] = len(out_shape)
        in_specs.append(pl.BlockSpec(memory_space=pl.ANY))
        args.append(states)
        out_shape.append(jax.ShapeDtypeStruct(states.shape, states.dtype))
        out_specs.append(pl.BlockSpec((ne, None, 2, HEADS, GDN_DK, GDN_DV), lambda b, c: (b, l, 0, 0, 0, 0)))
    outs = pl.pallas_call(
        functools.partial(_gdn_kernel, zero_init=zero_init, emit_state=emit_state, n_chunks=nc, n_elems=ne),
        out_shape=out_shape,
        grid=(n_batch // ne, nc),
        in_specs=in_specs,
        out_specs=out_specs,
        scratch_shapes=[pltpu.VMEM((ne, 2, PAIRS, PAIR_N, PAIR_N), F32)],
        input_output_aliases=aliases,
        compiler_params=_cparams(("parallel", "arbitrary")),
        name="gdn",
    )(*args)
    return outs[0].reshape(t, GDN_W), outs[1].reshape(t, GDN_W), (outs[2] if emit_state else None)


MERGE_TM = 512


def _merge_kernel(x_ref, mod_ref, oa_ref, gz_ref, conv_ref, cx_prev_ref, cx_next_ref, of_ref, ob_ref, mg_ref,
                  wpa_ref, wpb_ref, wpc_ref, wo_ref, cw_ref, gn_ref, ones_ref, fin_ref, o_ref, *,
                  seq, final_norm):
    edges = _seq_edges(x_ref.shape[0], seq, pl.program_id(0))
    o_a = oa_ref[...].astype(F32) * _silu(gz_ref[:, 0:MLA_W].astype(F32))
    pa = _dot(o_a.astype(BF16), wpa_ref[...])
    f32 = lambda ref, rows, c0: ref[rows, c0:c0 + CONV_W].astype(F32)
    every = slice(None)
    cx = f32(conv_ref, every, 0) * f32(conv_ref, every, CONV_W)
    pr, nr = slice(HALO - 1, HALO), slice(0, 1)
    prev_row = f32(cx_prev_ref, pr, 0) * f32(cx_prev_ref, pr, CONV_W)
    next_row = f32(cx_next_ref, nr, 0) * f32(cx_next_ref, nr, CONV_W)
    down, up = _shift_rows(cx, prev_row, next_row, edges)
    conv = down * cw_ref[0:1, :] + cx * cw_ref[1:2, :] + up * cw_ref[2:3, :]
    o_b = f32(conv_ref, every, 2 * CONV_W) * conv * _silu(f32(conv_ref, every, 3 * CONV_W))
    pb = _dot(o_b.astype(BF16), wpb_ref[...])
    og = of_ref[...] + ob_ref[...]
    ms = _head_sums(og * og, ones_ref[...]) * (1.0 / GDN_DV)
    o_c = og * lax.rsqrt(ms + EPS) * gn_ref[...] * _silu(gz_ref[:, MLA_W:W_GZ].astype(F32))
    pc = _dot(o_c.astype(BF16), wpc_ref[...])
    m = (_sigmoid(mg_ref[:, 0:D_MODEL].astype(F32)) * pa + _sigmoid(mg_ref[:, D_MODEL:2 * D_MODEL].astype(F32)) * pb
         + _sigmoid(mg_ref[:, 2 * D_MODEL:3 * D_MODEL].astype(F32)) * pc)
    r = _dot(m.astype(BF16), wo_ref[...])
    y = x_ref[...] + mod_ref[:, 2 * D_MODEL:3 * D_MODEL] * r
    if final_norm:
        y = _rms(y) * fin_ref[...]
    o_ref[...] = y


def _merge(x2d, mod, o_a, u_gz, u_conv, o_f, o_b, u_mg, lw, l, final_g, seq, cond_row0, cond_per_seq, final_norm):
    t = x2d.shape[0]
    tm = MERGE_TM
    assert t % tm == 0 and (seq % tm == 0 or tm % seq == 0) and (cond_per_seq == 0 or seq % tm == 0)
    tps = max(seq // tm, 1)
    row = lambda w: pl.BlockSpec((tm, w), lambda i: (i, 0))
    prev_spec, next_spec = _halo_specs(tm, 2 * CONV_W, t)
    names = ["w_pa", "w_pb", "w_pc", "w_o", "conv_b_w", "gdn_norm_lane"]
    return pl.pallas_call(
        functools.partial(_merge_kernel, seq=seq, final_norm=final_norm),
        out_shape=jax.ShapeDtypeStruct((t, D_MODEL), F32),
        grid=(t // tm,),
        in_specs=[
            row(D_MODEL),
            _mod_spec(l, lambda i: cond_row0 + (i // tps) * cond_per_seq),
            row(MLA_W), row(W_GZ), row(W_CONV), prev_spec, next_spec, row(GDN_W), row(GDN_W), row(W_MG),
        ] + [_layer_spec(lw[k], l) for k in names] + [_const_spec(lw["ones_bd"].shape), _const_spec((1, D_MODEL))],
        out_specs=row(D_MODEL),
        compiler_params=_cparams(("parallel",)),
        name="merge",
    )(x2d, mod, o_a, u_gz, u_conv, u_conv, u_conv, o_f, o_b, u_mg, *[lw[k] for k in names], lw["ones_bd"],
      final_g.reshape(1, D_MODEL))


_EVEN_ODD = list(range(0, QK_ROPE, 2)) + list(range(1, QK_ROPE, 2))


REPACK_TN = 256
_PACK_PIECES = ((_O_CQ, _O_KPE), None,
                (_O_Q, _O_Z), (_O_C, _O_GB), (_O_B, _O_C), (_O_GB, _O_Q), (_O_GA, _O_B), (_O_Z, _O_AB),
                (_O_MG, _O_END))


def _repack_kernel(wt_ref, small_ref, o_ref):
    off = 0
    for piece in _PACK_PIECES:
        xt = small_ref[...] if piece is None else wt_ref[piece[0]:piece[1], :]
        o_ref[:, off:off + xt.shape[0]] = xt.T.astype(BF16)
        off += xt.shape[0]


def _repack_w_in(w_in):
    depth = w_in.shape[0]
    wt = jnp.swapaxes(w_in, 1, 2)
    kpe = wt[:, _O_KPE:_O_GA]
    small = jnp.concatenate([kpe[:, 0::2], kpe[:, 1::2], kpe[:, 1::2], kpe[:, 0::2], wt[:, _O_AB:_O_MG], kpe], axis=1)
    return pl.pallas_call(
        _repack_kernel,
        out_shape=jax.ShapeDtypeStruct((depth, D_MODEL, W_IN_PACKED), BF16),
        grid=(depth, D_MODEL // REPACK_TN),
        in_specs=[pl.BlockSpec((None, _O_END, REPACK_TN), lambda l, i: (l, 0, i)),
                  pl.BlockSpec((None, LANES, REPACK_TN), lambda l, i: (l, 0, i))],
        out_specs=pl.BlockSpec((None, REPACK_TN, W_IN_PACKED), lambda l, i: (l, i, 0)),
        compiler_params=_cparams(("parallel", "parallel")),
        name="repack_w_in",
    )(wt, small)


def _pack_weights(w_in, q_norm_g, kv_norm_g, w_uq, w_ukv, conv_b_w, conv_qkv_w, a_log, dt_bias, gdn_norm_g,
                  w_pa, w_pb, w_pc, w_o, norm_g):
    depth = w_in.shape[0]
    eo = np.array(_EVEN_ODD)
    w_in_p = _repack_w_in(w_in)
    wq = w_uq.astype(BF16).reshape(depth, Q_LORA, HEADS, QK_NOPE + QK_ROPE)
    nope, rope = wq[..., :QK_NOPE], wq[..., QK_NOPE:]
    z32 = jnp.zeros((depth, Q_LORA, HEADS, HEAD_PAD - QK_NOPE - QK_ROPE), BF16)
    w_qa = jnp.concatenate([nope, rope[..., eo], z32], axis=-1).reshape(depth, Q_LORA, QK_W)
    wkv = w_ukv.astype(BF16).reshape(depth, KV_LORA, HEADS, QK_NOPE + V_HEAD)
    w_k = jnp.concatenate([wkv[..., :QK_NOPE], jnp.zeros((depth, KV_LORA, HEADS, HEAD_PAD - QK_NOPE), BF16)],
                          axis=-1).reshape(depth, KV_LORA, QK_W)
    w_v = wkv[..., QK_NOPE:].reshape(depth, KV_LORA, MLA_W)
    gate_pad = ((0, 0), (0, 0), (GATE_LANE, LANES - GATE_LANE - 2 * HEADS))
    hid = np.arange(GDN_W) // GDN_DK
    return {
        "w_in_p": w_in_p, "norm_g": norm_g.reshape(depth, 1, D_MODEL),
        "q_norm_g": q_norm_g.reshape(depth, 1, Q_LORA), "kv_norm_g": kv_norm_g.reshape(depth, 1, KV_LORA),
        "w_qa": w_qa, "w_k": w_k, "w_v": w_v,
        "conv_qkv_w": conv_qkv_w, "conv_b_w": conv_b_w,
        "alog_lane": jnp.pad(a_log.reshape(depth, 1, 2 * HEADS), gate_pad),
        "dtb_lane": jnp.pad(dt_bias.reshape(depth, 1, 2 * HEADS), gate_pad),
        "ones_bd": jnp.asarray(hid[:, None] == hid[None, :], BF16),
        "gdn_norm_lane": jnp.tile(gdn_norm_g, (1, HEADS)).reshape(depth, 1, GDN_W),
        "w_pa": w_pa.astype(BF16), "w_pb": w_pb.astype(BF16), "w_pc": w_pc.astype(BF16), "w_o": w_o.astype(BF16),
    }


def _rope_tables(n_tokens):
    t = np.arange(n_tokens)
    row = (t // GRID_W).astype(np.float32)
    col = (t % GRID_W).astype(np.float32)
    n_freq = QK_ROPE // 4
    inv_freq = (np.float32(ROPE_THETA) ** (-np.arange(n_freq, dtype=np.float32) / n_freq)).astype(np.float32)
    ang = np.concatenate([row[:, None] * inv_freq, col[:, None] * inv_freq], axis=-1).astype(np.float32)
    cos, sin = np.cos(ang), np.sin(ang)
    cc = np.concatenate([cos, cos], axis=-1)
    ss = np.concatenate([-sin, sin], axis=-1)
    one = np.ones((n_tokens, QK_NOPE), np.float32)
    z32 = np.zeros((n_tokens, QK_ROPE), np.float32)
    z64 = np.zeros((n_tokens, QK_NOPE), np.float32)
    m1 = np.concatenate([one, cc, z32], axis=-1) * np.float32(SM_SCALE)
    m2 = np.concatenate([z64, ss, z32], axis=-1) * np.float32(SM_SCALE)
    m3 = np.concatenate([cc, ss, z64], axis=-1)
    return tuple(jnp.asarray(m, F32) for m in (m1, m2, m3))


def _layer(x2d, mod, lw, l, final_g, n_batch, seq, cond_row0, cond_per_seq, final_norm, rope_tabs=None,
           ctx_kv=None, h0=None, tq=256, attn_heads=HEADS, new_caches=None):
    is_ctx = ctx_kv is None
    u_mla, u_qkv, u_conv, u_gz, u_mg = _inproj(x2d, mod, lw, l, seq, cond_row0, cond_per_seq)
    outs = _prep(u_mla, u_qkv, lw, l, seq, rope_tabs, new_caches[:2] if is_ctx else None)
    if is_ctx:
        qf, kf, vf, ckv_new, kpe_new, qg, kg, vg, gb = outs
        kv_parts = [(kf, vf, seq)]
    else:
        qf, kf, vf, qg, kg, vg, gb = outs
        kv_parts = [ctx_kv, (kf, vf, seq)]
    o_a = _attention(qf, kv_parts, n_batch, seq, tq, attn_heads)
    o_f, o_b, states_new = _gdn(qg, kg, vg, gb, h0, n_batch, seq, new_caches[2] if is_ctx else None, l)
    y = _merge(x2d, mod, o_a, u_gz, u_conv, o_f, o_b, u_mg, lw, l, final_g, seq, cond_row0, cond_per_seq, final_norm)
    return y, ((ckv_new, kpe_new, states_new) if is_ctx else None)


def kernel(x_prompt, x_sample, c, cache_ckv, cache_kpe, state_gdn, c_ctx, norm_g, w_ada, b_ada, w_in, q_norm_g,
           kv_norm_g, w_uq, w_ukv, conv_b_w, conv_qkv_w, a_log, dt_bias, gdn_norm_g, w_pa, w_pb, w_pc, w_o,
           final_norm_g):
    n_ctx, seq_ctx, _ = x_prompt.shape
    n_lat, seq_lat, _ = x_sample.shape
    past = cache_ckv.shape[2]
    depth = w_in.shape[0]
    assert n_lat + 1 <= COND_ROWS

    cond = jnp.concatenate([c, c_ctx[None, :], jnp.zeros((COND_ROWS - n_lat - 1, D_MODEL), F32)], axis=0)
    mod = _ada_mod(cond, w_ada, b_ada).reshape(depth, COND_ROWS, 1, 3 * D_MODEL)
    rope_tabs = _rope_tables(seq_lat)
    lw = _pack_weights(w_in, q_norm_g, kv_norm_g, w_uq, w_ukv, conv_b_w, conv_qkv_w, a_log, dt_bias, gdn_norm_g,
                       w_pa, w_pb, w_pc, w_o, norm_g)
    kpe_ctx = jnp.pad(cache_kpe[..., np.array(_EVEN_ODD)], ((0, 0),) * 3 + ((0, LANES - QK_ROPE),))

    xp = x_prompt.reshape(n_ctx * seq_ctx, D_MODEL)
    xs = x_sample.reshape(n_lat * seq_lat, D_MODEL)
    new_caches = (jnp.zeros((n_ctx, depth, seq_ctx, KV_LORA), F32), jnp.zeros((n_ctx, depth, seq_ctx, QK_ROPE), F32),
                  jnp.zeros((n_ctx, depth, 2, HEADS, GDN_DK, GDN_DV), F32))
    for l in range(depth):
        last = l == depth - 1
        xp, new_caches = _layer(xp, mod, lw, l, final_norm_g, n_ctx, seq_ctx, n_lat, 0, last, new_caches=new_caches)
        kc, vc = _ctx_kv(cache_ckv, kpe_ctx, lw, l)
        xs, _ = _layer(xs, mod, lw, l, final_norm_g, n_lat, seq_lat, 0, 1, last, rope_tabs=rope_tabs,
                       ctx_kv=(kc, vc, past), h0=(state_gdn, l), tq=1024, attn_heads=2)
    return (xp.reshape(n_ctx, seq_ctx, D_MODEL), xs.reshape(n_lat, seq_lat, D_MODEL)) + tuple(new_caches)
```

```python
import functools
import math

import jax
import jax.numpy as jnp
import numpy as np
from jax import lax
from jax.experimental import pallas as pl
from jax.experimental.pallas import tpu as pltpu

F32 = jnp.float32
BF16 = jnp.bfloat16

D_MODEL = 1024
DEPTH = 2
GRID_W = 64
EPS = 1e-6
HEADS = 8
Q_LORA = 384
KV_LORA = 256
QK_NOPE = 64
QK_ROPE = 32
V_HEAD = 64
MLA_W = HEADS * V_HEAD
ROPE_THETA = 10000.0
CONV_W = 512
GDN_DK = 64
GDN_DV = 64
GDN_W = HEADS * GDN_DK
GDN_CHUNK = 64
LANES = 128
HEAD_PAD = 128
QK_W = HEADS * HEAD_PAD
SM_SCALE = (QK_NOPE + QK_ROPE) ** -0.5
COND_ROWS = 16
VMEM_LIMIT = 56 * 1024 * 1024

_O_CQ, _O_CKV, _O_KPE, _O_GA = 0, 384, 640, 672
_O_B, _O_C, _O_X, _O_GB = 1184, 1696, 2208, 2720
_O_Q, _O_K, _O_V, _O_Z = 3232, 3744, 4256, 4768
_O_AB, _O_MG, _O_END = 5280, 5312, 8384
W_MLA = Q_LORA + KV_LORA + LANES
W_QKV = 3 * GDN_W
W_CONV = 4 * CONV_W
W_GZ = MLA_W + GDN_W
W_MG = 3 * D_MODEL
GROUP_WIDTHS = (W_MLA, W_QKV, W_CONV, W_GZ, W_MG)
GROUP_DTYPES = (F32, BF16, BF16, BF16, BF16)
W_IN_PACKED = sum(GROUP_WIDTHS)
GATE_LANE = 64


def _sigmoid(x):
    return 1.0 / (1.0 + jnp.exp(-x))


def _silu(x):
    return x * _sigmoid(x)


def _rms(x):
    return x * lax.rsqrt(jnp.mean(x * x, axis=-1, keepdims=True) + EPS)


def _dot(a, b):
    return jnp.dot(a, b, preferred_element_type=F32)


def _dot_nt(a, b):
    return lax.dot_general(a, b, (((1,), (1,)), ((), ())), preferred_element_type=F32)


def _split3(x):
    hi = x.astype(BF16)
    r = x - hi.astype(F32)
    mid = r.astype(BF16)
    lo = (r - mid.astype(F32)).astype(BF16)
    return hi, mid, lo


def _head_sums(x2, ones_bd):
    return _dot(x2.astype(BF16), ones_bd)


def _cparams(sem):
    return pltpu.CompilerParams(dimension_semantics=sem, vmem_limit_bytes=VMEM_LIMIT)


def _const_spec(shape):
    nd = len(shape)
    return pl.BlockSpec(shape, lambda *_: (0,) * nd)


def _mod_spec(l, cond_row):
    return pl.BlockSpec((None, None, 1, 3 * D_MODEL), lambda i: (l, cond_row(i), 0, 0))


def _layer_spec(arr, l, **kw):
    nd = arr.ndim - 1
    return pl.BlockSpec((None,) + arr.shape[1:], lambda *_: (l,) + (0,) * nd, **kw)


def _ada_kernel(c_ref, w_ref, b_ref, o_ref):
    sc = _silu(c_ref[...])
    o_ref[...] = jnp.dot(sc, w_ref[...], preferred_element_type=F32, precision=lax.Precision.HIGHEST) + b_ref[...]


def _ada_mod(cond, w_ada, b_ada):
    depth = w_ada.shape[0]
    return pl.pallas_call(
        _ada_kernel,
        out_shape=jax.ShapeDtypeStruct((depth, COND_ROWS, 3 * D_MODEL), F32),
        grid=(depth, 3),
        in_specs=[
            pl.BlockSpec((COND_ROWS, D_MODEL), lambda l, j: (0, 0)),
            pl.BlockSpec((None, D_MODEL, D_MODEL), lambda l, j: (l, 0, j)),
            pl.BlockSpec((None, 1, D_MODEL), lambda l, j: (l, 0, j)),
        ],
        out_specs=pl.BlockSpec((None, COND_ROWS, D_MODEL), lambda l, j: (l, 0, j)),
        compiler_params=_cparams(("arbitrary", "arbitrary")),
        name="ada_mod",
    )(cond, w_ada, b_ada.reshape(depth, 1, 3 * D_MODEL))


IN_TM = 512
IN_CHUNK = 768


def _inproj_kernel(x_ref, mod_ref, g_ref, w_ref, *rest):
    outs, h_scr = rest[:-1], rest[-1]
    x = x_ref[...]
    shift = mod_ref[:, 0:D_MODEL]
    scale = mod_ref[:, D_MODEL:2 * D_MODEL]
    h = (_rms(x) * g_ref[...]) * (1.0 + scale) + shift
    h_scr[...] = h.astype(BF16)
    off = 0
    for o_ref in outs:
        n = o_ref.shape[1]
        for c0 in range(0, n, IN_CHUNK):
            c1 = min(c0 + IN_CHUNK, n)
            o_ref[:, c0:c1] = _dot(h_scr[...], w_ref[:, off + c0:off + c1]).astype(o_ref.dtype)
        off += n


def _inproj(x2d, mod, lw, l, seq, cond_row0, cond_per_seq):
    t = x2d.shape[0]
    assert t % IN_TM == 0 and (cond_per_seq == 0 or seq % IN_TM == 0)
    tps = max(seq // IN_TM, 1)
    return pl.pallas_call(
        _inproj_kernel,
        out_shape=[jax.ShapeDtypeStruct((t, n), dt) for n, dt in zip(GROUP_WIDTHS, GROUP_DTYPES)],
        grid=(t // IN_TM,),
        in_specs=[
            pl.BlockSpec((IN_TM, D_MODEL), lambda i: (i, 0)),
            _mod_spec(l, lambda i: cond_row0 + (i // tps) * cond_per_seq),
            _layer_spec(lw["norm_g"], l),
            _layer_spec(lw["w_in_p"], l, pipeline_mode=pl.Buffered(1)),
        ],
        out_specs=[pl.BlockSpec((IN_TM, n), lambda i: (i, 0)) for n in GROUP_WIDTHS],
        scratch_shapes=[pltpu.VMEM((IN_TM, D_MODEL), BF16)],
        compiler_params=_cparams(("parallel",)),
        name="inproj",
    )(x2d, mod, lw["norm_g"], lw["w_in_p"])


PREP_TM = 512


def _seq_edges(tm, seq, i):
    row = lax.broadcasted_iota(jnp.int32, (tm, 1), 0)
    if seq >= tm:
        tps = seq // tm
        return (row == 0) & ((i % tps) == 0), (row == tm - 1) & ((i % tps) == tps - 1)
    pos = lax.rem(row, seq)
    return pos == 0, pos == seq - 1


def _shift_rows(x, prev_row, next_row, edges):
    n = x.shape[0]
    row = lax.broadcasted_iota(jnp.int32, (n, 1), 0)
    starts, ends = edges
    down = jnp.where(starts, 0.0, jnp.where(row == 0, prev_row, pltpu.roll(x, 1, 0)))
    up = jnp.where(ends, 0.0, jnp.where(row == n - 1, next_row, pltpu.roll(x, n - 1, 0)))
    return down, up


def _store_keys(kf_ref, k_nope, k_rope):
    lane = lax.broadcasted_iota(jnp.int32, (1, HEAD_PAD), 1)
    k_rope = jnp.where((lane >= QK_NOPE) & (lane < QK_NOPE + QK_ROPE), k_rope, 0.0)
    for h in range(HEADS):
        sl = slice(h * HEAD_PAD, (h + 1) * HEAD_PAD)
        kf_ref[:, sl] = (k_nope[:, sl] + k_rope).astype(BF16)


def _prep_kernel(*refs, use_rope, seq, emit_ckvn):
    it = iter(refs)
    mla_ref, qkv_ref, qkv_prev_ref, qkv_next_ref = next(it), next(it), next(it), next(it)
    if use_rope:
        m1_ref, m2_ref, m3_ref = next(it), next(it), next(it)
    gq_ref, gkv_ref, wa_ref = next(it), next(it), next(it)
    wk_ref, wv_ref, cw_ref, alog_ref, dtb_ref, ones_ref = (next(it) for _ in range(6))
    if emit_ckvn:
        next(it), next(it)
    qf_ref, kf_ref, vf_ref = next(it), next(it), next(it)
    if emit_ckvn:
        ckvn_ref, kpe_ref = next(it), next(it)
    qg_ref, kg_ref, vg_ref, gb_ref = next(it), next(it), next(it), next(it)

    edges = _seq_edges(mla_ref.shape[0], seq, pl.program_id(0))

    cq = mla_ref[:, 0:Q_LORA]
    ckv = mla_ref[:, Q_LORA:Q_LORA + KV_LORA]
    small = mla_ref[:, Q_LORA + KV_LORA:W_MLA]
    qn = (_rms(cq) * gq_ref[...]).astype(BF16)
    qa = _dot(qn, wa_ref[...])
    if use_rope:
        m1 = m1_ref[...]
        m2 = m2_ref[...]
        lane = lax.broadcasted_iota(jnp.int32, (1, HEAD_PAD), 1)
        half = QK_ROPE // 2
        first_half = lane < QK_NOPE + half
        for h in range(HEADS):
            sl = slice(h * HEAD_PAD, (h + 1) * HEAD_PAD)
            qh = qa[:, sl]
            partner = jnp.where(first_half, pltpu.roll(qh, HEAD_PAD - half, 1), pltpu.roll(qh, half, 1))
            qf_ref[:, sl] = (qh * m1 + partner * m2).astype(BF16)
        kpe_t = small * m3_ref[...]
        k_rope = pltpu.roll(kpe_t, QK_NOPE, 1) + pltpu.roll(kpe_t, QK_NOPE - QK_ROPE, 1)
    else:
        qf_ref[...] = (qa * SM_SCALE).astype(BF16)
        k_rope = pltpu.roll(small, QK_NOPE, 1)
    ckvn = _rms(ckv) * gkv_ref[...]
    if emit_ckvn:
        ckvn_ref[...] = ckvn.reshape(ckvn_ref.shape)
        kpe_ref[...] = small[:, LANES - QK_ROPE:].reshape(kpe_ref.shape)
    ckvn_b = ckvn.astype(BF16)
    _store_keys(kf_ref, _dot(ckvn_b, wk_ref[...]), k_rope)
    vf_ref[...] = _dot(ckvn_b, wv_ref[...]).astype(BF16)

    z1 = small + dtb_ref[...]
    sp = jnp.maximum(z1, 0.0) + jnp.log1p(jnp.exp(-jnp.abs(z1)))
    g = -jnp.exp(alog_ref[...]) * sp
    lane = lax.broadcasted_iota(jnp.int32, (1, LANES), 1)
    gb_ref[...] = jnp.where(lane < GATE_LANE + 2 * HEADS, g, _sigmoid(small))

    ones_bd = ones_ref[...]
    for part, o_ref in enumerate((qg_ref, kg_ref, vg_ref)):
        sl = slice(part * GDN_W, (part + 1) * GDN_W)
        x = qkv_ref[:, sl].astype(F32)
        prev_row = qkv_prev_ref[HALO - 1:HALO, sl].astype(F32)
        next_row = qkv_next_ref[0:1, sl].astype(F32)
        down, up = _shift_rows(x, prev_row, next_row, edges)
        y = down * cw_ref[0:1, sl] + x * cw_ref[1:2, sl] + up * cw_ref[2:3, sl]
        y = _silu(y)
        if part == 0:
            y = y * lax.rsqrt(_head_sums(y * y, ones_bd) + EPS) * (GDN_DK ** -0.5)
        elif part == 1:
            y = y * lax.rsqrt(_head_sums(y * y, ones_bd) + EPS)
        o_ref[...] = y


HALO = 16


def _halo_specs(tm, width, n_rows, col_block=0):
    r = tm // HALO
    nb = n_rows // HALO
    prev = pl.BlockSpec((HALO, width), lambda i: (jnp.maximum(i * r - 1, 0), col_block))
    nxt = pl.BlockSpec((HALO, width), lambda i: (jnp.minimum((i + 1) * r, nb - 1), col_block))
    return prev, nxt


def _stacked_spec(tm, seq, width, l):
    if tm >= seq:
        return pl.BlockSpec((tm // seq, None, seq, width), lambda i: (i, l, 0, 0))
    tps = seq // tm
    return pl.BlockSpec((None, None, tm, width), lambda i: (i // tps, l, i % tps, 0))


def _prep(u_mla, u_qkv, lw, l, seq, rope_tabs, caches):
    t = u_mla.shape[0]
    tm = PREP_TM
    use_rope = rope_tabs is not None
    emit_ckvn = caches is not None
    assert t % tm == 0 and (seq % tm == 0 or tm % seq == 0) and (not use_rope or seq % tm == 0)
    tps = max(seq // tm, 1)
    row = lambda w: pl.BlockSpec((tm, w), lambda i: (i, 0))
    prev_spec, next_spec = _halo_specs(tm, W_QKV, t)
    in_specs = [row(W_MLA), row(W_QKV), prev_spec, next_spec]
    args = [u_mla, u_qkv, u_qkv, u_qkv]
    if use_rope:
        in_specs += [pl.BlockSpec((tm, LANES), lambda i: (i % tps, 0))] * 3
        args += list(rope_tabs)
    names = ["q_norm_g", "kv_norm_g", "w_qa", "w_k", "w_v", "conv_qkv_w", "alog_lane", "dtb_lane"]
    in_specs += [_layer_spec(lw[k], l) for k in names] + [_const_spec(lw["ones_bd"].shape)]
    args += [lw[k] for k in names] + [lw["ones_bd"]]
    out_shape = [jax.ShapeDtypeStruct((t, QK_W), BF16), jax.ShapeDtypeStruct((t, QK_W), BF16),
                 jax.ShapeDtypeStruct((t, MLA_W), BF16)]
    out_specs = [row(QK_W), row(QK_W), row(MLA_W)]
    aliases = {}
    if emit_ckvn:
        for a in caches:
            aliases[len(args)] = len(out_shape)
            in_specs.append(pl.BlockSpec(memory_space=pl.ANY))
            args.append(a)
            out_shape.append(jax.ShapeDtypeStruct(a.shape, a.dtype))
            out_specs.append(_stacked_spec(tm, seq, a.shape[-1], l))
    out_shape += [jax.ShapeDtypeStruct((t, GDN_W), F32)] * 3 + [jax.ShapeDtypeStruct((t, LANES), F32)]
    out_specs += [row(GDN_W)] * 3 + [row(LANES)]
    outs = pl.pallas_call(
        functools.partial(_prep_kernel, use_rope=use_rope, seq=seq, emit_ckvn=emit_ckvn),
        out_shape=out_shape,
        grid=(t // tm,),
        in_specs=in_specs,
        out_specs=out_specs,
        input_output_aliases=aliases,
        compiler_params=_cparams(("parallel",)),
        name="prep",
    )(*args)
    return outs


def _ctxkv_kernel(ckv_ref, kpe_ref, wk_ref, wv_ref, kf_ref, vf_ref):
    ckv_b = ckv_ref[...].astype(BF16)
    _store_keys(kf_ref, _dot(ckv_b, wk_ref[...]), pltpu.roll(kpe_ref[...], QK_NOPE, 1))
    vf_ref[...] = _dot(ckv_b, wv_ref[...]).astype(BF16)


def _ctx_kv(cache_ckv, kpe_lanes, lw, l):
    n_batch, _, past, _ = cache_ckv.shape
    cached = lambda w: pl.BlockSpec((None, None, past, w), lambda i: (i, l, 0, 0))
    row = lambda w: pl.BlockSpec((past, w), lambda i: (i, 0))
    return pl.pallas_call(
        _ctxkv_kernel,
        out_shape=[jax.ShapeDtypeStruct((n_batch * past, QK_W), BF16), jax.ShapeDtypeStruct((n_batch * past, MLA_W), BF16)],
        grid=(n_batch,),
        in_specs=[cached(KV_LORA), cached(LANES), _layer_spec(lw["w_k"], l), _layer_spec(lw["w_v"], l)],
        out_specs=[row(QK_W), row(MLA_W)],
        compiler_params=_cparams(("parallel",)),
        name="ctx_kv",
    )(cache_ckv, kpe_lanes, lw["w_k"], lw["w_v"])


def _attn_kernel(*refs, n_parts, n_heads):
    q_ref = refs[0]
    k_refs = refs[1:1 + n_parts]
    v_refs = refs[1 + n_parts:1 + 2 * n_parts]
    o_ref = refs[1 + 2 * n_parts]
    lane = lax.broadcasted_iota(jnp.int32, (1, 2 * V_HEAD), 1)
    own = [lane < V_HEAD, lane >= V_HEAD]
    heads = range(n_heads)
    hl = [slice(j * HEAD_PAD, (j + 1) * HEAD_PAD) for j in heads]
    pl_ = [slice((j // 2) * 2 * V_HEAD, (j // 2 + 1) * 2 * V_HEAD) for j in heads]
    s = [[_dot_nt(q_ref[:, hl[j]], k_ref[:, hl[j]]) for k_ref in k_refs] for j in heads]
    m = []
    for j in heads:
        mj = s[j][0].max(axis=-1, keepdims=True)
        for sp in s[j][1:]:
            mj = jnp.maximum(mj, sp.max(axis=-1, keepdims=True))
        m.append(mj)
    p = [[jnp.exp(sp - m[j]).astype(BF16) for sp in s[j]] for j in heads]
    acc = []
    for j in heads:
        a = None
        for pj, v_ref in zip(p[j], v_refs):
            t = _dot(pj, jnp.where(own[j % 2], v_ref[:, pl_[j]], 1.0).astype(BF16))
            a = t if a is None else a + t
        acc.append(a)
    for j in range(0, n_heads, 2):
        l0 = acc[j][:, V_HEAD:V_HEAD + 1]
        l1 = acc[j + 1][:, 0:1]
        o_ref[:, pl_[j]] = jnp.where(own[0], acc[j] / l0, acc[j + 1] / l1).astype(o_ref.dtype)


def _attention(qf, kv_parts, n_batch, seq_q, tq, n_heads):
    n_parts = len(kv_parts)
    nq = seq_q // tq
    in_specs = [pl.BlockSpec((tq, n_heads * HEAD_PAD), lambda b, hp, qi: (b * nq + qi, hp))]
    in_specs += [pl.BlockSpec((sk, n_heads * HEAD_PAD), lambda b, hp, qi: (b, hp)) for _, _, sk in kv_parts]
    in_specs += [pl.BlockSpec((sk, n_heads * V_HEAD), lambda b, hp, qi: (b, hp)) for _, _, sk in kv_parts]
    return pl.pallas_call(
        functools.partial(_attn_kernel, n_parts=n_parts, n_heads=n_heads),
        out_shape=jax.ShapeDtypeStruct((n_batch * seq_q, MLA_W), BF16),
        grid=(n_batch, HEADS // n_heads, nq),
        in_specs=in_specs,
        out_specs=pl.BlockSpec((tq, n_heads * V_HEAD), lambda b, hp, qi: (b * nq + qi, hp)),
        compiler_params=_cparams(("parallel", "parallel", "arbitrary")),
        name="attention",
    )(qf, *[k for k, _, _ in kv_parts], *[v for _, v, _ in kv_parts])


PAIRS = HEADS // 2
PAIR_N = 2 * GDN_CHUNK


def _unit_tri_inverse(a_list, eye, blk16, blk32):
    diag = [jnp.where(blk16, a, 0.0) for a in a_list]
    powers = [[x.astype(BF16) for x in diag]]
    for _ in range(3):
        powers.append([_dot(x, x).astype(BF16) for x in powers[-1]])
    t = [eye - x for x in diag]
    for pw in powers[1:]:
        t = [x + _dot(x.astype(BF16), y) for x, y in zip(t, pw)]
    for off in ([jnp.where(blk32 & ~blk16, a, 0.0).astype(BF16) for a in a_list],
                [jnp.where(blk32, 0.0, a).astype(BF16) for a in a_list]):
        tb = [x.astype(BF16) for x in t]
        m = [_dot(x, y).astype(BF16) for x, y in zip(tb, off)]
        t = [x - _dot(y, z) for x, y, z in zip(t, m, tb)]
    return t


def _stack_pair(x, lane_lo):
    return jnp.concatenate([jnp.where(lane_lo, x, 0.0), jnp.where(lane_lo, 0.0, x)], axis=0)


def _pair_col(m, l0):
    return jnp.concatenate([m[:, l0:l0 + 1], m[:, l0 + 1:l0 + 2]], axis=0)


def _gdn_kernel(*refs, zero_init, emit_state, n_chunks, n_elems):
    it = iter(refs)
    dir_in = [tuple(next(it) for _ in range(4)) for _ in range(2)]
    h0_ref = None if zero_init else next(it)
    if emit_state:
        next(it)
    o_refs = (next(it), next(it))
    hfin_ref = next(it) if emit_state else None
    s_scr = next(it)
    c = pl.program_id(1)
    cs, n = GDN_CHUNK, PAIR_N

    @pl.when(c == 0)
    def _():
        if zero_init:
            s_scr[...] = jnp.zeros_like(s_scr)
        else:
            z = jnp.zeros((GDN_DK, GDN_DV), F32)
            for e in range(n_elems):
                for d in range(2):
                    for p in range(PAIRS):
                        top = jnp.concatenate([h0_ref[e, d, 2 * p], z], axis=1)
                        bot = jnp.concatenate([z, h0_ref[e, d, 2 * p + 1]], axis=1)
                        s_scr[e, d, p] = jnp.concatenate([top, bot], axis=0)

    ri = lax.broadcasted_iota(jnp.int32, (n, n), 0)
    ci = lax.broadcasted_iota(jnp.int32, (n, n), 1)
    same_head = (ri // cs) == (ci // cs)
    eye = (ri == ci).astype(F32)
    blk16 = (ri // 16) == (ci // 16)
    blk32 = (ri // 32) == (ci // 32)
    lane_lo = lax.broadcasted_iota(jnp.int32, (1, n), 1) < cs
    ti = lax.broadcasted_iota(jnp.int32, (cs, cs), 0)
    tj = lax.broadcasted_iota(jnp.int32, (cs, cs), 1)

    incl_d =[same_head & (ri >= ci), same_head & (ri <= ci)]
    strict_d = [same_head & (ri > ci), same_head & (ri < ci)]
    gates_d, gc_d, gct_d, glast_d = {}, {}, {}, {}
    for e in range(n_elems):
        for d in range(2):
            gates = dir_in[d][3][e]
            csum = ((ti >= tj) if d == 0 else (ti <= tj)).astype(BF16)
            ghi, gmid, glo = _split3(gates)
            gc_all = _dot(csum, ghi) + _dot(csum, gmid) + _dot(csum, glo)
            gates_d[e, d] = gates
            gc_d[e, d] = gc_all
            gct_d[e, d] = gc_all.T
            glast_d[e, d] = gc_all[cs - 1:cs, :] if d == 0 else gc_all[0:1, :]

    def run_group(elems):
        chains = [(e, d, p) for e in elems for d in range(2) for p in range(PAIRS)]
        kst, qst, vst, gcol, bcol, glast, decay, egc = [], [], [], [], [], [], [], []
        for e, d, p in chains:
            q_ref, k_ref, v_ref, _ = dir_in[d]
            sl = slice(p * n, (p + 1) * n)
            lg = GATE_LANE + HEADS * d + 2 * p
            kst.append(_stack_pair(k_ref[e, :, sl], lane_lo))
            qst.append(_stack_pair(q_ref[e, :, sl], lane_lo))
            vst.append(_stack_pair(v_ref[e, :, sl], lane_lo))
            gc = _pair_col(gc_d[e, d], lg)
            gr = jnp.concatenate([gct_d[e, d][lg:lg + 1, :], gct_d[e, d][lg + 1:lg + 2, :]], axis=1)
            gcol.append(gc)
            bcol.append(_pair_col(gates_d[e, d], lg + 2 * HEADS))
            glast.append(jnp.concatenate([jnp.broadcast_to(glast_d[e, d][:, lg:lg + 1], (cs, 1)),
                                          jnp.broadcast_to(glast_d[e, d][:, lg + 1:lg + 2], (cs, 1))], axis=0))
            decay.append(jnp.exp(jnp.where(incl_d[d], gc - gr, -jnp.inf)))
            egc.append(jnp.exp(gc))
        nch = len(chains)
        kstb = [x.astype(BF16) for x in kst]
        gp = [_dot_nt(jnp.concatenate([kstb[i], qst[i].astype(BF16)], axis=0), kstb[i]) for i in range(nch)]
        a_mat = [jnp.where(strict_d[chains[i][1]], bcol[i] * gp[i][:n] * decay[i], 0.0) for i in range(nch)]
        attn = [jnp.where(incl_d[chains[i][1]], gp[i][n:] * decay[i], 0.0).astype(BF16) for i in range(nch)]
        rhs = [jnp.concatenate([vst[i] * bcol[i], kst[i] * (bcol[i] * egc[i])], axis=1).astype(BF16)
               for i in range(nch)]
        t_inv = _unit_tri_inverse(a_mat, eye, blk16, blk32)
        x = [_dot(t_inv[i].astype(BF16), rhs[i]) for i in range(nch)]
        s_old = [s_scr[e, d, p] for e, d, p in chains]
        sb = [s.astype(BF16) for s in s_old]
        ws = [_dot(jnp.concatenate([x[i][:, n:], qst[i] * egc[i]], axis=0).astype(BF16), sb[i]) for i in range(nch)]
        v_new = [(x[i][:, :n] - ws[i][:n]).astype(BF16) for i in range(nch)]
        o_st = [ws[i][n:] + _dot(attn[i], v_new[i]) for i in range(nch)]
        k_dec_t = [(kst[i] * jnp.exp(glast[i] - gcol[i])).T.astype(BF16) for i in range(nch)]
        for i, (e, d, p) in enumerate(chains):
            s_scr[e, d, p] = s_old[i] * jnp.exp(glast[i]) + _dot(k_dec_t[i], v_new[i])
        for e in elems:
            for d in range(2):
                o_pairs = [o_st[chains.index((e, d, p))] for p in range(PAIRS)]
                o_refs[d][e] = jnp.concatenate([o[:cs] + o[cs:] for o in o_pairs], axis=1)

    for e0 in range(0, n_elems, GDN_GROUP):
        run_group(range(e0, min(e0 + GDN_GROUP, n_elems)))

    if emit_state:
        @pl.when(c == n_chunks - 1)
        def _():
            for e in range(n_elems):
                for d in range(2):
                    for p in range(PAIRS):
                        s = s_scr[e, d, p]
                        hfin_ref[e, d, 2 * p] = s[:GDN_DK, :GDN_DV]
                        hfin_ref[e, d, 2 * p + 1] = s[GDN_DK:, GDN_DV:]


GDN_ELEMS = 4
GDN_GROUP = 2


def _gdn(qg, kg, vg, gb, h0, n_batch, seq, states=None, l=0):
    nc = seq // GDN_CHUNK
    ne = GDN_ELEMS
    assert n_batch % ne == 0
    zero_init = h0 is None
    emit_state = states is not None
    fwd = lambda w: pl.BlockSpec((ne, GDN_CHUNK, w), lambda b, c: (b, c, 0))
    bwd = lambda w: pl.BlockSpec((ne, GDN_CHUNK, w), lambda b, c: (b, nc - 1 - c, 0))
    in_specs = [fwd(GDN_W), fwd(GDN_W), fwd(GDN_W), fwd(LANES), bwd(GDN_W), bwd(GDN_W), bwd(GDN_W), bwd(LANES)]
    seq3 = lambda a: a.reshape(n_batch, seq, a.shape[-1])
    args = [seq3(qg), seq3(kg), seq3(vg), seq3(gb)] * 2
    if not zero_init:
        h0_all, h0_layer = h0
        in_specs.append(pl.BlockSpec((ne, None, 2, HEADS, GDN_DK, GDN_DV), lambda b, c: (b, h0_layer, 0, 0, 0, 0)))
        args.append(h0_all)
    t = n_batch * seq
    out_shape = [jax.ShapeDtypeStruct((n_batch, seq, GDN_W), F32), jax.ShapeDtypeStruct((n_batch, seq, GDN_W), F32)]
    out_specs = [fwd(GDN_W), bwd(GDN_W)]
    aliases = {}
    if emit_state:
        aliases[len(args)] = len(out_shape)
        in_specs.append(pl.BlockSpec(memory_space=pl.ANY))
        args.append(states)
        out_shape.append(jax.ShapeDtypeStruct(states.shape, states.dtype))
        out_specs.append(pl.BlockSpec((ne, None, 2, HEADS, GDN_DK, GDN_DV), lambda b, c: (b, l, 0, 0, 0, 0)))
    outs = pl.pallas_call(
        functools.partial(_gdn_kernel, zero_init=zero_init, emit_state=emit_state, n_chunks=nc, n_elems=ne),
        out_shape=out_shape,
        grid=(n_batch // ne, nc),
        in_specs=in_specs,
        out_specs=out_specs,
        scratch_shapes=[pltpu.VMEM((ne, 2, PAIRS, PAIR_N, PAIR_N), F32)],
        input_output_aliases=aliases,
        compiler_params=_cparams(("parallel", "arbitrary")),
        name="gdn",
    )(*args)
    return outs[0].reshape(t, GDN_W), outs[1].reshape(t, GDN_W), (outs[2] if emit_state else None)


MERGE_TM = 512


def _merge_kernel(x_ref, mod_ref, oa_ref, gz_ref, conv_ref, cx_prev_ref, cx_next_ref, of_ref, ob_ref, mg_ref,
                  wpa_ref, wpb_ref, wpc_ref, wo_ref, cw_ref, gn_ref, ones_ref, fin_ref, o_ref, *,
                  seq, final_norm):
    edges = _seq_edges(x_ref.shape[0], seq, pl.program_id(0))
    o_a = oa_ref[...].astype(F32) * _silu(gz_ref[:, 0:MLA_W].astype(F32))
    pa = _dot(o_a.astype(BF16), wpa_ref[...])
    f32 = lambda ref, rows, c0: ref[rows, c0:c0 + CONV_W].astype(F32)
    every = slice(None)
    cx = f32(conv_ref, every, 0) * f32(conv_ref, every, CONV_W)
    pr, nr = slice(HALO - 1, HALO), slice(0, 1)
    prev_row = f32(cx_prev_ref, pr, 0) * f32(cx_prev_ref, pr, CONV_W)
    next_row = f32(cx_next_ref, nr, 0) * f32(cx_next_ref, nr, CONV_W)
    down, up = _shift_rows(cx, prev_row, next_row, edges)
    conv = down * cw_ref[0:1, :] + cx * cw_ref[1:2, :] + up * cw_ref[2:3, :]
    o_b = f32(conv_ref, every, 2 * CONV_W) * conv * _silu(f32(conv_ref, every, 3 * CONV_W))
    pb = _dot(o_b.astype(BF16), wpb_ref[...])
    og = of_ref[...] + ob_ref[...]
    ms = _head_sums(og * og, ones_ref[...]) * (1.0 / GDN_DV)
    o_c = og * lax.rsqrt(ms + EPS) * gn_ref[...] * _silu(gz_ref[:, MLA_W:W_GZ].astype(F32))
    pc = _dot(o_c.astype(BF16), wpc_ref[...])
    m = (_sigmoid(mg_ref[:, 0:D_MODEL].astype(F32)) * pa + _sigmoid(mg_ref[:, D_MODEL:2 * D_MODEL].astype(F32)) * pb
         + _sigmoid(mg_ref[:, 2 * D_MODEL:3 * D_MODEL].astype(F32)) * pc)
    r = _dot(m.astype(BF16), wo_ref[...])
    y = x_ref[...] + mod_ref[:, 2 * D_MODEL:3 * D_MODEL] * r
    if final_norm:
        y = _rms(y) * fin_ref[...]
    o_ref[...] = y


def _merge(x2d, mod, o_a, u_gz, u_conv, o_f, o_b, u_mg, lw, l, final_g, seq, cond_row0, cond_per_seq, final_norm):
    t = x2d.shape[0]
    tm = MERGE_TM
    assert t % tm == 0 and (seq % tm == 0 or tm % seq == 0) and (cond_per_seq == 0 or seq % tm == 0)
    tps = max(seq // tm, 1)
    row = lambda w: pl.BlockSpec((tm, w), lambda i: (i, 0))
    prev_spec, next_spec = _halo_specs(tm, 2 * CONV_W, t)
    names = ["w_pa", "w_pb", "w_pc", "w_o", "conv_b_w", "gdn_norm_lane"]
    return pl.pallas_call(
        functools.partial(_merge_kernel, seq=seq, final_norm=final_norm),
        out_shape=jax.ShapeDtypeStruct((t, D_MODEL), F32),
        grid=(t // tm,),
        in_specs=[
            row(D_MODEL),
            _mod_spec(l, lambda i: cond_row0 + (i // tps) * cond_per_seq),
            row(MLA_W), row(W_GZ), row(W_CONV), prev_spec, next_spec, row(GDN_W), row(GDN_W), row(W_MG),
        ] + [_layer_spec(lw[k], l) for k in names] + [_const_spec(lw["ones_bd"].shape), _const_spec((1, D_MODEL))],
        out_specs=row(D_MODEL),
        compiler_params=_cparams(("parallel",)),
        name="merge",
    )(x2d, mod, o_a, u_gz, u_conv, u_conv, u_conv, o_f, o_b, u_mg, *[lw[k] for k in names], lw["ones_bd"],
      final_g.reshape(1, D_MODEL))


_EVEN_ODD = list(range(0, QK_ROPE, 2)) + list(range(1, QK_ROPE, 2))


REPACK_TN = 256
_PACK_PIECES = ((_O_CQ, _O_KPE), None,
                (_O_Q, _O_Z), (_O_C, _O_GB), (_O_B, _O_C), (_O_GB, _O_Q), (_O_GA, _O_B), (_O_Z, _O_AB),
                (_O_MG, _O_END))


def _repack_kernel(wt_ref, small_ref, o_ref):
    off = 0
    for piece in _PACK_PIECES:
        xt = small_ref[...] if piece is None else wt_ref[piece[0]:piece[1], :]
        o_ref[:, off:off + xt.shape[0]] = xt.T.astype(BF16)
        off += xt.shape[0]


def _repack_w_in(w_in):
    depth = w_in.shape[0]
    wt = jnp.swapaxes(w_in, 1, 2)
    kpe = wt[:, _O_KPE:_O_GA]
    small = jnp.concatenate([kpe[:, 0::2], kpe[:, 1::2], kpe[:, 1::2], kpe[:, 0::2], wt[:, _O_AB:_O_MG], kpe], axis=1)
    return pl.pallas_call(
        _repack_kernel,
        out_shape=jax.ShapeDtypeStruct((depth, D_MODEL, W_IN_PACKED), BF16),
        grid=(depth, D_MODEL // REPACK_TN),
        in_specs=[pl.BlockSpec((None, _O_END, REPACK_TN), lambda l, i: (l, 0, i)),
                  pl.BlockSpec((None, LANES, REPACK_TN), lambda l, i: (l, 0, i))],
        out_specs=pl.BlockSpec((None, REPACK_TN, W_IN_PACKED), lambda l, i: (l, i, 0)),
        compiler_params=_cparams(("parallel", "parallel")),
        name="repack_w_in",
    )(wt, small)


def _pack_weights(w_in, q_norm_g, kv_norm_g, w_uq, w_ukv, conv_b_w, conv_qkv_w, a_log, dt_bias, gdn_norm_g,
                  w_pa, w_pb, w_pc, w_o, norm_g):
    depth = w_in.shape[0]
    eo = np.array(_EVEN_ODD)
    w_in_p = _repack_w_in(w_in)
    wq = w_uq.astype(BF16).reshape(depth, Q_LORA, HEADS, QK_NOPE + QK_ROPE)
    nope, rope = wq[..., :QK_NOPE], wq[..., QK_NOPE:]
    z32 = jnp.zeros((depth, Q_LORA, HEADS, HEAD_PAD - QK_NOPE - QK_ROPE), BF16)
    w_qa = jnp.concatenate([nope, rope[..., eo], z32], axis=-1).reshape(depth, Q_LORA, QK_W)
    wkv = w_ukv.astype(BF16).reshape(depth, KV_LORA, HEADS, QK_NOPE + V_HEAD)
    w_k = jnp.concatenate([wkv[..., :QK_NOPE], jnp.zeros((depth, KV_LORA, HEADS, HEAD_PAD - QK_NOPE), BF16)],
                          axis=-1).reshape(depth, KV_LORA, QK_W)
    w_v = wkv[..., QK_NOPE:].reshape(depth, KV_LORA, MLA_W)
    gate_pad = ((0, 0), (0, 0), (GATE_LANE, LANES - GATE_LANE - 2 * HEADS))
    hid = np.arange(GDN_W) // GDN_DK
    return {
        "w_in_p": w_in_p, "norm_g": norm_g.reshape(depth, 1, D_MODEL),
        "q_norm_g": q_norm_g.reshape(depth, 1, Q_LORA), "kv_norm_g": kv_norm_g.reshape(depth, 1, KV_LORA),
        "w_qa": w_qa, "w_k": w_k, "w_v": w_v,
        "conv_qkv_w": conv_qkv_w, "conv_b_w": conv_b_w,
        "alog_lane": jnp.pad(a_log.reshape(depth, 1, 2 * HEADS), gate_pad),
        "dtb_lane": jnp.pad(dt_bias.reshape(depth, 1, 2 * HEADS), gate_pad),
        "ones_bd": jnp.asarray(hid[:, None] == hid[None, :], BF16),
        "gdn_norm_lane": jnp.tile(gdn_norm_g, (1, HEADS)).reshape(depth, 1, GDN_W),
        "w_pa": w_pa.astype(BF16), "w_pb": w_pb.astype(BF16), "w_pc": w_pc.astype(BF16), "w_o": w_o.astype(BF16),
    }


def _rope_tables(n_tokens):
    t = np.arange(n_tokens)
    row = (t // GRID_W).astype(np.float32)
    col = (t % GRID_W).astype(np.float32)
    n_freq = QK_ROPE // 4
    inv_freq = (np.float32(ROPE_THETA) ** (-np.arange(n_freq, dtype=np.float32) / n_freq)).astype(np.float32)
    ang = np.concatenate([row[:, None] * inv_freq, col[:, None] * inv_freq], axis=-1).astype(np.float32)
    cos, sin = np.cos(ang), np.sin(ang)
    cc = np.concatenate([cos, cos], axis=-1)
    ss = np.concatenate([-sin, sin], axis=-1)
    one = np.ones((n_tokens, QK_NOPE), np.float32)
    z32 = np.zeros((n_tokens, QK_ROPE), np.float32)
    z64 = np.zeros((n_tokens, QK_NOPE), np.float32)
    m1 = np.concatenate([one, cc, z32], axis=-1) * np.float32(SM_SCALE)
    m2 = np.concatenate([z64, ss, z32], axis=-1) * np.float32(SM_SCALE)
    m3 = np.concatenate([cc, ss, z64], axis=-1)
    return tuple(jnp.asarray(m, F32) for m in (m1, m2, m3))


def _layer(x2d, mod, lw, l, final_g, n_batch, seq, cond_row0, cond_per_seq, final_norm, rope_tabs=None,
           ctx_kv=None, h0=None, tq=256, attn_heads=HEADS, new_caches=None):
    is_ctx = ctx_kv is None
    u_mla, u_qkv, u_conv, u_gz, u_mg = _inproj(x2d, mod, lw, l, seq, cond_row0, cond_per_seq)
    outs = _prep(u_mla, u_qkv, lw, l, seq, rope_tabs, new_caches[:2] if is_ctx else None)
    if is_ctx:
        qf, kf, vf, ckv_new, kpe_new, qg, kg, vg, gb = outs
        kv_parts = [(kf, vf, seq)]
    else:
        qf, kf, vf, qg, kg, vg, gb = outs
        kv_parts = [ctx_kv, (kf, vf, seq)]
    o_a = _attention(qf, kv_parts, n_batch, seq, tq, attn_heads)
    o_f, o_b, states_new = _gdn(qg, kg, vg, gb, h0, n_batch, seq, new_caches[2] if is_ctx else None, l)
    y = _merge(x2d, mod, o_a, u_gz, u_conv, o_f, o_b, u_mg, lw, l, final_g, seq, cond_row0, cond_per_seq, final_norm)
    return y, ((ckv_new, kpe_new, states_new) if is_ctx else None)


def kernel(x_prompt, x_sample, c, cache_ckv, cache_kpe, state_gdn, c_ctx, norm_g, w_ada, b_ada, w_in, q_norm_g,
           kv_norm_g, w_uq, w_ukv, conv_b_w, conv_qkv_w, a_log, dt_bias, gdn_norm_g, w_pa, w_pb, w_pc, w_o,
           final_norm_g):
    n_ctx, seq_ctx, _ = x_prompt.shape
    n_lat, seq_lat, _ = x_sample.shape
    past = cache_ckv.shape[2]
    depth = w_in.shape[0]
    assert n_lat + 1 <= COND_ROWS

    cond = jnp.concatenate([c, c_ctx[None, :], jnp.zeros((COND_ROWS - n_lat - 1, D_MODEL), F32)], axis=0)
    mod = _ada_mod(cond, w_ada, b_ada).reshape(depth, COND_ROWS, 1, 3 * D_MODEL)
    rope_tabs = _rope_tables(seq_lat)
    lw = _pack_weights(w_in, q_norm_g, kv_norm_g, w_uq, w_ukv, conv_b_w, conv_qkv_w, a_log, dt_bias, gdn_norm_g,
                       w_pa, w_pb, w_pc, w_o, norm_g)
    kpe_ctx = jnp.pad(cache_kpe[..., np.array(_EVEN_ODD)], ((0, 0),) * 3 + ((0, LANES - QK_ROPE),))

    xp = x_prompt.reshape(n_ctx * seq_ctx, D_MODEL)
    xs = x_sample.reshape(n_lat * seq_lat, D_MODEL)
    new_caches = (jnp.zeros((n_ctx, depth, seq_ctx, KV_LORA), F32), jnp.zeros((n_ctx, depth, seq_ctx, QK_ROPE), F32),
                  jnp.zeros((n_ctx, depth, 2, HEADS, GDN_DK, GDN_DV), F32))
    for l in range(depth):
        last = l == depth - 1
        xp, new_caches = _layer(xp, mod, lw, l, final_norm_g, n_ctx, seq_ctx, n_lat, 0, last, new_caches=new_caches)
        kc, vc = _ctx_kv(cache_ckv, kpe_ctx, lw, l)
        xs, _ = _layer(xs, mod, lw, l, final_norm_g, n_lat, seq_lat, 0, 1, last, rope_tabs=rope_tabs,
                       ctx_kv=(kc, vc, past), h0=(state_gdn, l), tq=1024, attn_heads=2)
    return (xp.reshape(n_ctx, seq_ctx, D_MODEL), xs.reshape(n_lat, seq_lat, D_MODEL)) + tuple(new_caches)
```
